```python
import jax, jax.numpy as jnp
from jax import lax
import numpy as np

D_MODEL = 1024
BATCH = 8
SEQ = 2048
DEPTH = 2

CHUNK = 64
Q_BLOCK = 128
EPS = 1e-6
N_HEADS = 6
NOPE_DIM = 128
ROPE_DIM = 64
V_DIM = 128
Q_LORA = 384
KV_LORA = 256
ROPE_THETA = 10000.0
ATTN_WIDTH = N_HEADS * V_DIM
POOL_WINDOWS = (2, 4, 8, 16)
N_POOL_GROUPS = len(POOL_WINDOWS)
POOL_GROUP_DIM = 64
POOL_WIDTH = N_POOL_GROUPS * POOL_GROUP_DIM
MIX_WIDTH = ATTN_WIDTH + POOL_WIDTH
IN_COLS = Q_LORA + KV_LORA + ROPE_DIM + POOL_WIDTH
D_FF = 2816
N_EXPERTS = 8
TOP_K = 2
D_FF_EXPERT = 3584
N_DENSE = (DEPTH + 1) // 2
N_MOE = DEPTH // 2

kernel_name = "hybrid_mla_pool_moe_sandwich"


def rms_norm(x, g):
    xf = x.astype(jnp.float32)
    y = xf * lax.rsqrt(jnp.mean(xf * xf, axis=-1, keepdims=True) + EPS)
    return (y * g.astype(jnp.float32)).astype(x.dtype)


def rope_tables(seq):
    inv = 1.0 / (ROPE_THETA ** (jnp.arange(0, ROPE_DIM, 2, dtype=jnp.float32) / ROPE_DIM))
    ang = jnp.arange(seq, dtype=jnp.float32)[:, None] * inv[None, :]
    return jnp.cos(ang), jnp.sin(ang)


def apply_rope(x, cos, sin):
    x1, x2 = jnp.split(x.astype(jnp.float32), 2, axis=-1)
    return jnp.concatenate([x1 * cos - x2 * sin, x2 * cos + x1 * sin], axis=-1).astype(x.dtype)


def chunk_causal_mla(q_nope, q_rope, k_nope, k_rope, v):
    seq = q_nope.shape[1]
    scale = (NOPE_DIM + ROPE_DIM) ** -0.5
    outs = []
    for blk in range(seq // Q_BLOCK):
        q0, q1 = blk * Q_BLOCK, (blk + 1) * Q_BLOCK
        s = (jnp.einsum('bqhd,bkhd->bhqk', q_nope[:, q0:q1], k_nope[:, :q1])
             + jnp.einsum('bqhr,bkr->bhqk', q_rope[:, q0:q1], k_rope[:, :q1])).astype(jnp.float32) * scale
        q_chunk = jnp.arange(q0, q1) // CHUNK
        k_chunk = jnp.arange(q1) // CHUNK
        mask = k_chunk[None, :] <= q_chunk[:, None]
        s = jnp.where(mask[None, None], s, -jnp.inf)
        p = jax.nn.softmax(s, axis=-1).astype(v.dtype)
        outs.append(jnp.einsum('bhqk,bkhd->bqhd', p, v[:, :q1]))
    return jnp.concatenate(outs, axis=1)


def multiscale_pool(u, w, scale):
    b, seq, _ = u.shape
    ug = u.reshape(b, seq, N_POOL_GROUPS, POOL_GROUP_DIM)
    cs = jnp.cumsum(ug.astype(jnp.float32), axis=1)
    cs = jnp.pad(cs, ((0, 0), (1, 0), (0, 0), (0, 0)))
    t = jnp.arange(seq)
    pooled = []
    for g, win in enumerate(POOL_WINDOWS):
        start = jnp.maximum(t + 1 - win, 0)
        sums = cs[:, 1:, g] - jnp.take(cs[:, :, g], start, axis=1)
        cnt = (t + 1 - start).astype(jnp.float32)
        pooled.append(sums / cnt[None, :, None])
    pooled = jnp.stack(pooled, axis=2).astype(u.dtype) - ug
    y = jnp.einsum('bsgc,gcd->bsgd', pooled, w)
    return y.reshape(b, seq, POOL_WIDTH) * scale


def swiglu(h, w_gate, w_up, w_down):
    return (jax.nn.silu(h @ w_gate) * (h @ w_up)) @ w_down


def moe_swiglu(h, w_router, w_gate, w_up, w_down):
    logits = jnp.einsum('bsd,de->bse', h, w_router).astype(jnp.float32)
    top_val, top_idx = lax.top_k(logits, TOP_K)
    gates = jax.nn.softmax(top_val, axis=-1)
    combine = jnp.sum(jax.nn.one_hot(top_idx, N_EXPERTS, dtype=jnp.float32) * gates[..., None],
                      axis=-2).astype(h.dtype)
    out = jnp.zeros_like(h)
    for e in range(N_EXPERTS):
        out = out + combine[..., e:e + 1] * swiglu(h, w_gate[e], w_up[e], w_down[e])
    return out


def setup_inputs(seed: int = 0) -> dict:
    key = jax.random.key(seed)
    ks = iter(jax.random.split(key, 32))
    f32 = jnp.float32

    def w(shape, fan_in):
        return jax.random.normal(next(ks), shape, f32) * (fan_in ** -0.5)

    def gain(shape):
        return 1.0 + 0.05 * jax.random.normal(next(ks), shape, f32)

    return {
        "x": jax.random.normal(next(ks), (BATCH, SEQ, D_MODEL), f32),
        "attn_pre_g": gain((DEPTH, D_MODEL)),
        "attn_post_g": gain((DEPTH, D_MODEL)),
        "ffn_pre_g": gain((DEPTH, D_MODEL)),
        "ffn_post_g": gain((DEPTH, D_MODEL)),
        "w_in": w((DEPTH, D_MODEL, IN_COLS), D_MODEL),
        "q_norm_g": gain((DEPTH, Q_LORA)),
        "kv_norm_g": gain((DEPTH, KV_LORA)),
        "w_uq": w((DEPTH, Q_LORA, N_HEADS, NOPE_DIM + ROPE_DIM), Q_LORA),
        "w_ukv": w((DEPTH, KV_LORA, N_HEADS, NOPE_DIM + V_DIM), KV_LORA),
        "pool_w": w((DEPTH, N_POOL_GROUPS, POOL_GROUP_DIM, POOL_GROUP_DIM), POOL_GROUP_DIM),
        "pool_scale": gain((DEPTH, POOL_WIDTH)),
        "w_out": w((DEPTH, MIX_WIDTH, D_MODEL), MIX_WIDTH),
        "w_gate_d": w((N_DENSE, D_MODEL, D_FF), D_MODEL),
        "w_up_d": w((N_DENSE, D_MODEL, D_FF), D_MODEL),
        "w_down_d": w((N_DENSE, D_FF, D_MODEL), D_FF),
        "w_router": w((N_MOE, D_MODEL, N_EXPERTS), D_MODEL),
        "w_gate_e": w((N_MOE, N_EXPERTS, D_MODEL, D_FF_EXPERT), D_MODEL),
        "w_up_e": w((N_MOE, N_EXPERTS, D_MODEL, D_FF_EXPERT), D_MODEL),
        "w_down_e": w((N_MOE, N_EXPERTS, D_FF_EXPERT, D_MODEL), D_FF_EXPERT),
    }


def reference(x, attn_pre_g, attn_post_g, ffn_pre_g, ffn_post_g, w_in, q_norm_g, kv_norm_g,
              w_uq, w_ukv, pool_w, pool_scale, w_out, w_gate_d, w_up_d, w_down_d,
              w_router, w_gate_e, w_up_e, w_down_e):
    b, seq, _ = x.shape
    cos, sin = rope_tables(seq)
    splits = [Q_LORA, Q_LORA + KV_LORA, Q_LORA + KV_LORA + ROPE_DIM]
    for l in range(DEPTH):
        h = rms_norm(x, attn_pre_g[l])
        z = h @ w_in[l]
        c_q, c_kv, k_rope, u_pool = jnp.split(z, splits, axis=-1)
        q = jnp.einsum('bsc,chd->bshd', rms_norm(c_q, q_norm_g[l]), w_uq[l])
        q_nope, q_rope = q[..., :NOPE_DIM], q[..., NOPE_DIM:]
        kv = jnp.einsum('bsc,chd->bshd', rms_norm(c_kv, kv_norm_g[l]), w_ukv[l])
        k_nope, v = kv[..., :NOPE_DIM], kv[..., NOPE_DIM:]
        q_rope = apply_rope(q_rope, cos[:, None, :], sin[:, None, :])
        k_rope = apply_rope(k_rope, cos, sin)
        a = chunk_causal_mla(q_nope, q_rope, k_nope, k_rope, v).reshape(b, seq, ATTN_WIDTH)
        p = multiscale_pool(u_pool, pool_w[l], pool_scale[l])
        mix = jnp.concatenate([a, p], axis=-1) @ w_out[l]
        x = x + rms_norm(mix, attn_post_g[l])
        h = rms_norm(x, ffn_pre_g[l])
        if l % 2 == 0:
            i = l // 2
            f = swiglu(h, w_gate_d[i], w_up_d[i], w_down_d[i])
        else:
            i = l // 2
            f = moe_swiglu(h, w_router[i], w_gate_e[i], w_up_e[i], w_down_e[i])
        x = x + rms_norm(f, ffn_post_g[l])
    return x
```

```python
import functools

import jax
import jax.numpy as jnp
from jax import lax
from jax.experimental import pallas as pl
from jax.experimental.pallas import tpu as pltpu

D_MODEL = 1024
CHUNK = 64
EPS = 1e-6
N_HEADS = 6
NOPE_DIM = 128
ROPE_DIM = 64
V_DIM = 128
Q_LORA = 384
KV_LORA = 256
ROPE_THETA = 10000.0
POOL_WINDOWS = (2, 4, 8, 16)
POOL_GROUP_DIM = 64
POOL_WIDTH = len(POOL_WINDOWS) * POOL_GROUP_DIM
N_EXPERTS = 8
MAX_WINDOW = max(POOL_WINDOWS)

LANES = 128
SUBLANES = 8
HEAD_W = 2 * LANES
ROW_SPLIT = D_MODEL // LANES

_C_Q = 0
_C_KV = Q_LORA
_C_KR = Q_LORA + KV_LORA
_C_KRS = _C_KR + LANES
_C_POOL = _C_KRS + LANES
IN_AUG = _C_POOL + POOL_WIDTH

TOKEN_TILE = 512
ATTN_TILE = 256
FF_SUB = 256
EXPERT_TILE = 512
EXPERT_FF_STEPS = 2

_F32 = jnp.float32
_BF16 = jnp.bfloat16
_MIB = 1024 * 1024


def _rms(x, g):
    return x * lax.rsqrt(jnp.mean(x * x, axis=-1, keepdims=True) + EPS) * g


def _silu_mul(g, u):
    return g * (1.0 / (1.0 + jnp.exp(-g))) * u


def _params(sem, vmem_mib):
    return pltpu.CompilerParams(dimension_semantics=sem, vmem_limit_bytes=vmem_mib * _MIB)


def _in_kernel(x_ref, g_ref, win_ref, qg_ref, kvg_ref, wq_ref, wkv_ref, tab_ref, pw_ref, ps_ref,
               q_ref, k_ref, v_ref, p_ref, ext_ref, *, tm, tiles_per_seq):
    i = pl.program_id(0)
    h = _rms(x_ref[...], g_ref[...]).astype(_BF16)
    z = jnp.dot(h, win_ref[...], preferred_element_type=_F32)
    cq = _rms(z[:, _C_Q:_C_Q + Q_LORA], qg_ref[...]).astype(_BF16)
    ckv = _rms(z[:, _C_KV:_C_KV + KV_LORA], kvg_ref[...]).astype(_BF16)
    q = jnp.dot(cq, wq_ref[...], preferred_element_type=_F32)
    kv = jnp.dot(ckv, wkv_ref[...], preferred_element_type=_F32)
    tab = tab_ref[...]
    qmul = tab[:, 0:HEAD_W]
    cc = tab[:, HEAD_W:HEAD_W + LANES]
    ss = tab[:, HEAD_W + LANES:HEAD_W + 2 * LANES]
    krr = (z[:, _C_KR:_C_KR + LANES] * cc + z[:, _C_KRS:_C_KRS + LANES] * ss).astype(_BF16)
    for hd in range(N_HEADS):
        q_ref[:, hd * HEAD_W:(hd + 1) * HEAD_W] = (q[:, hd * HEAD_W:(hd + 1) * HEAD_W] * qmul).astype(_BF16)
        k_ref[:, hd * HEAD_W:hd * HEAD_W + NOPE_DIM] = kv[:, hd * NOPE_DIM:(hd + 1) * NOPE_DIM].astype(_BF16)
        k_ref[:, hd * HEAD_W + NOPE_DIM:(hd + 1) * HEAD_W] = krr
    v_ref[...] = kv[:, N_HEADS * NOPE_DIM:].astype(_BF16)

    u = z[:, _C_POOL:_C_POOL + POOL_WIDTH]

    @pl.when(i % tiles_per_seq == 0)
    def _():
        ext_ref[0:MAX_WINDOW, :] = jnp.zeros((MAX_WINDOW, POOL_WIDTH), _F32)

    ext_ref[MAX_WINDOW:MAX_WINDOW + tm, :] = u

    def window_sum(k0, k1, lo):
        acc = ext_ref[MAX_WINDOW - k0:MAX_WINDOW - k0 + tm, lo:lo + LANES]
        for k in range(k0 + 1, k1):
            acc = acc + ext_ref[MAX_WINDOW - k:MAX_WINDOW - k + tm, lo:lo + LANES]
        return acc

    t_pos = (i % tiles_per_seq) * tm + lax.broadcasted_iota(jnp.int32, (tm, LANES), 0)
    cnt = (t_pos + 1).astype(_F32)
    first_group = lax.broadcasted_iota(jnp.int32, (tm, LANES), 1) < POOL_GROUP_DIM
    w0, w1, w2, w3 = POOL_WINDOWS
    s_a = window_sum(0, w0, 0)
    s_b = s_a + window_sum(w0, w1, 0)
    pooled_lo = jnp.where(first_group, s_a / jnp.minimum(cnt, float(w0)), s_b / jnp.minimum(cnt, float(w1)))
    s_c = window_sum(0, w2, LANES)
    s_d = s_c + window_sum(w2, w3, LANES)
    pooled_hi = jnp.where(first_group, s_c / jnp.minimum(cnt, float(w2)), s_d / jnp.minimum(cnt, float(w3)))
    d = jnp.concatenate([pooled_lo - u[:, :LANES], pooled_hi - u[:, LANES:]], axis=1).astype(_BF16)
    y = jnp.dot(d, pw_ref[...], preferred_element_type=_F32) * ps_ref[...]
    p_ref[...] = y.astype(_BF16)
    ext_ref[0:MAX_WINDOW, :] = ext_ref[tm:tm + MAX_WINDOW, :]


def _in_proj(x2, g, win, qg, kvg, wq, wkv, tab, pw, ps, *, seq):
    t = x2.shape[0]
    tm = TOKEN_TILE
    tps = seq // tm
    full = lambda a: pl.BlockSpec(a.shape, lambda i: (0,) * a.ndim)
    return pl.pallas_call(
        functools.partial(_in_kernel, tm=tm, tiles_per_seq=tps),
        grid=(t // tm,),
        in_specs=[pl.BlockSpec((tm, D_MODEL), lambda i: (i, 0)),
                  full(g), full(win), full(qg), full(kvg), full(wq), full(wkv),
                  pl.BlockSpec((tm, tab.shape[1]), lambda i: (i % tps, 0)),
                  full(pw), full(ps)],
        out_specs=[pl.BlockSpec((tm, N_HEADS * HEAD_W), lambda i: (i, 0)),
                   pl.BlockSpec((tm, N_HEADS * HEAD_W), lambda i: (i, 0)),
                   pl.BlockSpec((tm, N_HEADS * V_DIM), lambda i: (i, 0)),
                   pl.BlockSpec((tm, POOL_WIDTH), lambda i: (i, 0))],
        out_shape=[jax.ShapeDtypeStruct((t, N_HEADS * HEAD_W), _BF16),
                   jax.ShapeDtypeStruct((t, N_HEADS * HEAD_W), _BF16),
                   jax.ShapeDtypeStruct((t, N_HEADS * V_DIM), _BF16),
                   jax.ShapeDtypeStruct((t, POOL_WIDTH), _BF16)],
        scratch_shapes=[pltpu.VMEM((tm + MAX_WINDOW, POOL_WIDTH), _F32)],
        compiler_params=_params(("arbitrary",), 48),
        name="in_proj",
    )(x2, g, win, qg, kvg, wq, wkv, tab, pw, ps)


def _attn_kernel(q_ref, k_ref, v_ref, o_ref, *, tq):
    i = pl.program_id(2)
    q = q_ref[...]

    def scores(j0):
        k = k_ref[pl.ds(j0, tq), :]
        return lax.dot_general(q, k, (((1,), (1,)), ((), ())), preferred_element_type=_F32)

    d0 = pl.multiple_of(i * tq, tq)
    s = scores(d0)
    row_chunk = lax.broadcasted_iota(jnp.int32, (tq, tq), 0) // CHUNK
    col_chunk = lax.broadcasted_iota(jnp.int32, (tq, tq), 1) // CHUNK
    s = jnp.where(col_chunk <= row_chunk, s, -jnp.inf)
    m = jnp.max(s, axis=1, keepdims=True)
    p = jnp.exp(s - m)
    l = jnp.sum(p, axis=1, keepdims=True)
    acc = jnp.dot(p.astype(_BF16), v_ref[pl.ds(d0, tq), :], preferred_element_type=_F32)

    def body(j, carry):
        m, l, acc = carry
        j0 = pl.multiple_of(j * tq, tq)
        s = scores(j0)
        m_new = jnp.maximum(m, jnp.max(s, axis=1, keepdims=True))
        alpha = jnp.exp(m - m_new)
        p = jnp.exp(s - m_new)
        l = alpha * l + jnp.sum(p, axis=1, keepdims=True)
        acc = alpha * acc + jnp.dot(p.astype(_BF16), v_ref[pl.ds(j0, tq), :], preferred_element_type=_F32)
        return m_new, l, acc

    m, l, acc = lax.fori_loop(0, i, body, (m, l, acc))
    o_ref[...] = (acc / l).astype(_BF16)


def _attention(q, k, v):
    b, seq, _ = q.shape
    tq = ATTN_TILE
    return pl.pallas_call(
        functools.partial(_attn_kernel, tq=tq),
        grid=(b, N_HEADS, seq // tq),
        in_specs=[pl.BlockSpec((None, tq, HEAD_W), lambda bb, hh, ii: (bb, ii, hh)),
                  pl.BlockSpec((None, seq, HEAD_W), lambda bb, hh, ii: (bb, 0, hh)),
                  pl.BlockSpec((None, seq, V_DIM), lambda bb, hh, ii: (bb, 0, hh))],
        out_specs=pl.BlockSpec((None, tq, V_DIM), lambda bb, hh, ii: (bb, ii, hh)),
        out_shape=jax.ShapeDtypeStruct((b, seq, N_HEADS * V_DIM), _BF16),
        compiler_params=_params(("arbitrary", "arbitrary", "arbitrary"), 32),
        name="attention",
    )(q, k, v)


def _mix_residual(a_ref, p_ref, x_ref, wa_ref, wp_ref, g1_ref):
    mix = (jnp.dot(a_ref[...], wa_ref[...], preferred_element_type=_F32)
           + jnp.dot(p_ref[...], wp_ref[...], preferred_element_type=_F32))
    return x_ref[...] + _rms(mix, g1_ref[...])


def _out_dense_kernel(a_ref, p_ref, x_ref, wa_ref, wp_ref, g1_ref, g2_ref, xo_ref, ho_ref):
    xn = _mix_residual(a_ref, p_ref, x_ref, wa_ref, wp_ref, g1_ref)
    xo_ref[...] = xn
    ho_ref[...] = _rms(xn, g2_ref[...]).astype(_BF16)


def _out_moe_kernel(a_ref, p_ref, x_ref, wa_ref, wp_ref, g1_ref, g2_ref, wr_ref,
                    xo_ref, ho_ref, ri_ref, rf_ref, cnt_ref, carry_ref, *, tm):
    i = pl.program_id(0)
    xn = _mix_residual(a_ref, p_ref, x_ref, wa_ref, wp_ref, g1_ref)
    xo_ref[...] = xn
    h = _rms(xn, g2_ref[...])
    for s in range(ROW_SPLIT):
        ho_ref[pl.ds(s, tm, stride=ROW_SPLIT), :] = h[:, s * LANES:(s + 1) * LANES]

    logits = jnp.dot(h, wr_ref[...], preferred_element_type=_F32, precision=lax.Precision.HIGHEST)
    lane = lax.broadcasted_iota(jnp.int32, (tm, LANES), 1)
    logits = jnp.where(lane < N_EXPERTS, logits, -jnp.inf)
    v1 = jnp.max(logits, axis=1, keepdims=True)
    e1 = jnp.min(jnp.where(logits == v1, lane, LANES), axis=1, keepdims=True)
    rest = jnp.where(lane == e1, -jnp.inf, logits)
    v2 = jnp.max(rest, axis=1, keepdims=True)
    e2 = jnp.min(jnp.where(rest == v2, lane, LANES), axis=1, keepdims=True)
    ex = jnp.exp(v2 - v1)
    gate1 = 1.0 / (1.0 + ex)
    gate2 = ex * gate1

    @pl.when(i == 0)
    def _():
        carry_ref[...] = jnp.zeros_like(carry_ref)

    sel1 = lane == e1
    sel2 = lane == e2
    member = jnp.logical_or(sel1, sel2).astype(_F32)
    lower = (lax.broadcasted_iota(jnp.int32, (tm, tm), 1)
             < lax.broadcasted_iota(jnp.int32, (tm, tm), 0)).astype(_BF16)
    before = jnp.dot(lower, member.astype(_BF16), preferred_element_type=_F32) + carry_ref[0:1, :]
    r1 = jnp.sum(jnp.where(sel1, before, 0.0), axis=1, keepdims=True).astype(jnp.int32)
    r2 = jnp.sum(jnp.where(sel2, before, 0.0), axis=1, keepdims=True).astype(jnp.int32)
    total = carry_ref[0:1, :] + jnp.sum(member, axis=0, keepdims=True)
    carry_ref[...] = jnp.broadcast_to(total, carry_ref.shape)
    cnt_ref[...] = jnp.broadcast_to(total, cnt_ref.shape)

    ri_ref[...] = jnp.where(lane == 0, e1, jnp.where(lane == 1, e2, jnp.where(lane == 2, r1, r2)))
    rf_ref[...] = jnp.where(lane == 0, gate1, gate2)


def _out_proj(a2, p2, x2, wa, wp, g1, g2, w_router=None):
    t = x2.shape[0]
    tm = TOKEN_TILE
    full = lambda a: pl.BlockSpec(a.shape, lambda i: (0,) * a.ndim)
    row = lambda w: pl.BlockSpec((tm, w), lambda i: (i, 0))
    in_specs = [row(a2.shape[1]), row(p2.shape[1]), row(D_MODEL), full(wa), full(wp), full(g1), full(g2)]
    if w_router is None:
        return pl.pallas_call(
            _out_dense_kernel,
            grid=(t // tm,),
            in_specs=in_specs,
            out_specs=[row(D_MODEL), row(D_MODEL)],
            out_shape=[jax.ShapeDtypeStruct((t, D_MODEL), _F32), jax.ShapeDtypeStruct((t, D_MODEL), _BF16)],
            compiler_params=_params(("arbitrary",), 40),
            name="out_proj_dense",
        )(a2, p2, x2, wa, wp, g1, g2)
    return pl.pallas_call(
        functools.partial(_out_moe_kernel, tm=tm),
        grid=(t // tm,),
        in_specs=in_specs + [full(w_router)],
        out_specs=[row(D_MODEL),
                   pl.BlockSpec((tm * ROW_SPLIT, LANES), lambda i: (i, 0)),
                   row(LANES), row(LANES),
                   pl.BlockSpec((SUBLANES, LANES), lambda i: (0, 0))],
        out_shape=[jax.ShapeDtypeStruct((t, D_MODEL), _F32),
                   jax.ShapeDtypeStruct((t * ROW_SPLIT, LANES), _F32),
                   jax.ShapeDtypeStruct((t, LANES), jnp.int32),
                   jax.ShapeDtypeStruct((t, LANES), _F32),
                   jax.ShapeDtypeStruct((SUBLANES, LANES), _F32)],
        scratch_shapes=[pltpu.VMEM((SUBLANES, LANES), _F32)],
        compiler_params=_params(("arbitrary",), 48),
        name="out_proj_router",
    )(a2, p2, x2, wa, wp, g1, g2, w_router)


def _ffn_kernel(h_ref, x_ref, wgu_ref, wd_ref, g_ref, o_ref, acc_ref, *, nsub):
    h = h_ref[...]
    acc_ref[...] = jnp.zeros_like(acc_ref)

    def body(j, carry):
        gu = jnp.dot(h, wgu_ref[j], preferred_element_type=_F32)
        hm = _silu_mul(gu[:, :FF_SUB], gu[:, FF_SUB:]).astype(_BF16)
        acc_ref[...] += jnp.dot(hm, wd_ref[j], preferred_element_type=_F32)
        return carry

    lax.fori_loop(0, nsub, body, 0)
    o_ref[...] = x_ref[...] + _rms(acc_ref[...], g_ref[...])


def _dense_ffn(h2, x2, wgu, wd, g):
    t = x2.shape[0]
    tm = TOKEN_TILE
    nsub = wgu.shape[0]
    full = lambda a: pl.BlockSpec(a.shape, lambda i: (0,) * a.ndim)
    row = pl.BlockSpec((tm, D_MODEL), lambda i: (i, 0))
    return pl.pallas_call(
        functools.partial(_ffn_kernel, nsub=nsub),
        grid=(t // tm,),
        in_specs=[row, row, full(wgu), full(wd), full(g)],
        out_specs=row,
        out_shape=jax.ShapeDtypeStruct((t, D_MODEL), _F32),
        scratch_shapes=[pltpu.VMEM((tm, D_MODEL), _F32)],
        compiler_params=_params(("arbitrary",), 56),
        name="dense_ffn",
    )(h2, x2, wgu, wd, g)


def _expert_kernel(te_ref, nv_ref, src_ref, h_hbm, wgu_ref, wd_ref, y_ref,
                   buf_ref, xb_ref, acc_ref, sem, *, tm, nsub):
    i = pl.program_id(0)
    c = pl.program_id(1)
    nv = nv_ref[0]
    rows = tm * ROW_SPLIT

    def issue(tile, slot):
        base = tile * tm

        def body(r, carry):
            tok = src_ref[base + r]
            pltpu.make_async_copy(
                h_hbm.at[pl.ds(pl.multiple_of(tok * ROW_SPLIT, ROW_SPLIT), ROW_SPLIT), :],
                buf_ref.at[slot, pl.ds(pl.multiple_of(r * ROW_SPLIT, ROW_SPLIT), ROW_SPLIT), :],
                sem.at[slot]).start()
            return carry

        lax.fori_loop(0, tm, body, 0)

    @pl.when(jnp.logical_and(c == 0, i < nv))
    def _():
        slot = i % 2

        @pl.when(i == 0)
        def _():
            issue(0, 0)

        @pl.when(i + 1 < nv)
        def _():
            issue(i + 1, 1 - slot)

        pltpu.make_async_copy(h_hbm.at[pl.ds(0, rows), :], buf_ref.at[slot], sem.at[slot]).wait()
        for s in range(ROW_SPLIT):
            xb_ref[:, s * LANES:(s + 1) * LANES] = buf_ref[slot, pl.ds(s, tm, stride=ROW_SPLIT), :].astype(_BF16)
        acc_ref[...] = jnp.zeros_like(acc_ref)

    @pl.when(i < nv)
    def _():
        x = xb_ref[...]

        def body(j, carry):
            gu = jnp.dot(x, wgu_ref[j], preferred_element_type=_F32)
            hm = _silu_mul(gu[:, :FF_SUB], gu[:, FF_SUB:]).astype(_BF16)
            acc_ref[...] += jnp.dot(hm, wd_ref[j], preferred_element_type=_F32)
            return carry

        lax.fori_loop(0, nsub, body, 0)

    @pl.when(jnp.logical_and(c == pl.num_programs(1) - 1, i < nv))
    def _():
        for s in range(ROW_SPLIT):
            y_ref[pl.ds(s, tm, stride=ROW_SPLIT), :] = acc_ref[:, s * LANES:(s + 1) * LANES]

    @pl.when(jnp.logical_and(c == pl.num_programs(1) - 1, i >= nv))
    def _():
        y_ref[...] = jnp.zeros_like(y_ref)


def _expert_ffn(tile_expert, n_valid, src_tok, h_rows, wgu, wd):
    tm = EXPERT_TILE
    n_tiles = tile_expert.shape[0]
    n_chunks = wgu.shape[1]
    steps = EXPERT_FF_STEPS
    nsub = n_chunks // steps

    def w_map(i, c, te, nv, src):
        valid = i < nv[0]
        return (te[jnp.minimum(i, nv[0] - 1)], jnp.where(valid, c, steps - 1), 0, 0)

    grid_spec = pltpu.PrefetchScalarGridSpec(
        num_scalar_prefetch=3,
        grid=(n_tiles, steps),
        in_specs=[pl.BlockSpec(memory_space=pl.ANY),
                  pl.BlockSpec((None, nsub, D_MODEL, 2 * FF_SUB), w_map),
                  pl.BlockSpec((None, nsub, FF_SUB, D_MODEL), w_map)],
        out_specs=pl.BlockSpec((tm * ROW_SPLIT, LANES), lambda i, c, te, nv, src: (i, 0)),
        scratch_shapes=[pltpu.VMEM((2, tm * ROW_SPLIT, LANES), _F32),
                        pltpu.VMEM((tm, D_MODEL), _BF16),
                        pltpu.VMEM((tm, D_MODEL), _F32),
                        pltpu.SemaphoreType.DMA((2,))],
    )
    return pl.pallas_call(
        functools.partial(_expert_kernel, tm=tm, nsub=nsub),
        grid_spec=grid_spec,
        out_shape=jax.ShapeDtypeStruct((n_tiles * tm * ROW_SPLIT, LANES), _F32),
        compiler_params=_params(("arbitrary", "arbitrary"), 56),
        name="expert_ffn",
    )(tile_expert, n_valid, src_tok, h_rows, wgu, wd)


def _combine_kernel(p1_ref, p2_ref, y_hbm, x_ref, rf_ref, g_ref, o_ref, buf_ref, sem, *, tm):
    i = pl.program_id(0)
    n = pl.num_programs(0)
    rows = tm * ROW_SPLIT
    slot = i % 2

    def issue(tile, slot):
        base = tile * tm

        def body(r, carry):
            dst = pl.ds(pl.multiple_of(r * ROW_SPLIT, ROW_SPLIT), ROW_SPLIT)
            for k, pos_ref in enumerate((p1_ref, p2_ref)):
                pos = pos_ref[base + r]
                pltpu.make_async_copy(
                    y_hbm.at[pl.ds(pl.multiple_of(pos * ROW_SPLIT, ROW_SPLIT), ROW_SPLIT), :],
                    buf_ref.at[slot, k, dst, :], sem.at[slot]).start()
            return carry

        lax.fori_loop(0, tm, body, 0)

    @pl.when(i == 0)
    def _():
        issue(0, 0)

    @pl.when(i + 1 < n)
    def _():
        issue(i + 1, 1 - slot)

    for k in range(2):
        pltpu.make_async_copy(y_hbm.at[pl.ds(0, rows), :], buf_ref.at[slot, k], sem.at[slot]).wait()
    rf = rf_ref[...]
    g1 = rf[:, 0:1]
    g2 = rf[:, 1:2]
    parts = []
    for s in range(ROW_SPLIT):
        y1 = buf_ref[slot, 0, pl.ds(s, tm, stride=ROW_SPLIT), :]
        y2 = buf_ref[slot, 1, pl.ds(s, tm, stride=ROW_SPLIT), :]
        parts.append(g1 * y1 + g2 * y2)
    f = jnp.concatenate(parts, axis=1)
    o_ref[...] = x_ref[...] + _rms(f, g_ref[...])


def _combine(pos1, pos2, y_rows, x2, route_f, g):
    t = x2.shape[0]
    tm = TOKEN_TILE
    grid_spec = pltpu.PrefetchScalarGridSpec(
        num_scalar_prefetch=2,
        grid=(t // tm,),
        in_specs=[pl.BlockSpec(memory_space=pl.ANY),
                  pl.BlockSpec((tm, D_MODEL), lambda i, p1, p2: (i, 0)),
                  pl.BlockSpec((tm, LANES), lambda i, p1, p2: (i, 0)),
                  pl.BlockSpec(g.shape, lambda i, p1, p2: (0, 0))],
        out_specs=pl.BlockSpec((tm, D_MODEL), lambda i, p1, p2: (i, 0)),
        scratch_shapes=[pltpu.VMEM((2, 2, tm * ROW_SPLIT, LANES), _F32),
                        pltpu.SemaphoreType.DMA((2,))],
    )
    return pl.pallas_call(
        functools.partial(_combine_kernel, tm=tm),
        grid_spec=grid_spec,
        out_shape=jax.ShapeDtypeStruct((t, D_MODEL), _F32),
        compiler_params=_params(("arbitrary",), 40),
        name="combine",
    )(pos1, pos2, y_rows, x2, route_f, g)


def _swap_halves(w):
    half = w.shape[-1] // 2
    return jnp.concatenate([w[..., half:], w[..., :half]], axis=-1)


def _layer_weights(w_in, w_uq, w_ukv, pool_w, w_out):
    kr = w_in[:, _C_KR:_C_KR + ROPE_DIM]
    krs = _swap_halves(kr)
    win = jnp.concatenate([w_in[:, :_C_KR], kr, kr, krs, krs, w_in[:, _C_KR + ROPE_DIM:]], axis=1)
    q_rope = w_uq[:, :, NOPE_DIM:]
    wq = jnp.concatenate([w_uq[:, :, :NOPE_DIM], q_rope, _swap_halves(q_rope)], axis=2)
    wq = wq.reshape(Q_LORA, N_HEADS * HEAD_W)
    wkv = jnp.concatenate([w_ukv[:, :, :NOPE_DIM].reshape(KV_LORA, N_HEADS * NOPE_DIM),
                           w_ukv[:, :, NOPE_DIM:].reshape(KV_LORA, N_HEADS * V_DIM)], axis=1)
    pw = jnp.zeros((POOL_WIDTH, POOL_WIDTH), _F32)
    for gidx in range(len(POOL_WINDOWS)):
        lo = gidx * POOL_GROUP_DIM
        pw = pw.at[lo:lo + POOL_GROUP_DIM, lo:lo + POOL_GROUP_DIM].set(pool_w[gidx])
    wa = w_out[:N_HEADS * V_DIM]
    wp = w_out[N_HEADS * V_DIM:]
    return tuple(a.astype(_BF16) for a in (win, wq, wkv, pw, wa, wp))


def _gate_up_chunks(w_gate, w_up):
    lead = w_gate.shape[:-2]
    d, f = w_gate.shape[-2:]
    n = f // FF_SUB
    g = w_gate.astype(_BF16).reshape(*lead, d, n, FF_SUB)
    u = w_up.astype(_BF16).reshape(*lead, d, n, FF_SUB)
    gu = jnp.concatenate([g, u], axis=-1)
    return jnp.moveaxis(gu, -2, -3)


def _down_chunks(w_down):
    lead = w_down.shape[:-2]
    f, d = w_down.shape[-2:]
    return w_down.astype(_BF16).reshape(*lead, f // FF_SUB, FF_SUB, d)


def _rope_table(seq):
    inv = 1.0 / (ROPE_THETA ** (jnp.arange(0, ROPE_DIM, 2, dtype=_F32) / ROPE_DIM))
    ang = jnp.arange(seq, dtype=_F32)[:, None] * inv[None, :]
    cos, sin = jnp.cos(ang), jnp.sin(ang)
    cos2 = jnp.concatenate([cos, cos], axis=1)
    sin2 = jnp.concatenate([-sin, sin], axis=1)
    scale = (NOPE_DIM + ROPE_DIM) ** -0.5
    qmul = jnp.concatenate([jnp.full((seq, NOPE_DIM), scale, _F32), scale * cos2, scale * sin2], axis=1)
    return jnp.concatenate([qmul, cos2, cos2, sin2, sin2], axis=1)


def _routing_plan(route_i, counts, t):
    tm = EXPERT_TILE
    n_tiles = (2 * t) // tm + N_EXPERTS
    cnt = counts[0, :N_EXPERTS].astype(jnp.int32)
    padded = ((cnt + tm - 1) // tm) * tm
    ends = jnp.cumsum(padded)
    base = ends - padded
    n_valid = (ends[-1] // tm).astype(jnp.int32).reshape(1)
    tile_start = jnp.arange(n_tiles, dtype=jnp.int32) * tm
    tile_expert = jnp.minimum(jnp.sum(tile_start[:, None] >= ends[None, :], axis=1), N_EXPERTS - 1)
    e1, e2, r1, r2 = (route_i[:, k] for k in range(4))
    onehot = lambda e: (e[:, None] == jnp.arange(N_EXPERTS, dtype=jnp.int32)[None, :]).astype(jnp.int32)
    pos1 = jnp.sum(onehot(e1) * base[None, :], axis=1) + r1
    pos2 = jnp.sum(onehot(e2) * base[None, :], axis=1) + r2
    tok = jnp.arange(t, dtype=jnp.int32)
    src = jnp.zeros((n_tiles * tm,), jnp.int32).at[pos1].set(tok).at[pos2].set(tok)
    return tile_expert.astype(jnp.int32), n_valid, src, pos1.astype(jnp.int32), pos2.astype(jnp.int32)


def kernel(x, attn_pre_g, attn_post_g, ffn_pre_g, ffn_post_g, w_in, q_norm_g, kv_norm_g, w_uq, w_ukv,
           pool_w, pool_scale, w_out, w_gate_d, w_up_d, w_down_d, w_router, w_gate_e, w_up_e, w_down_e):
    b, seq, d = x.shape
    t = b * seq
    depth = w_in.shape[0]
    assert d == D_MODEL and seq % TOKEN_TILE == 0 and seq % ATTN_TILE == 0
    tab = _rope_table(seq)
    row = lambda a: a.reshape(1, -1)
    x2 = x.reshape(t, d)
    for l in range(depth):
        win, wq, wkv, pw, wa, wp = _layer_weights(w_in[l], w_uq[l], w_ukv[l], pool_w[l], w_out[l])
        q, k, v, p = _in_proj(x2, row(attn_pre_g[l]), win, row(q_norm_g[l]), row(kv_norm_g[l]),
                              wq, wkv, tab, pw, row(pool_scale[l]), seq=seq)
        a = _attention(q.reshape(b, seq, -1), k.reshape(b, seq, -1), v.reshape(b, seq, -1))
        a2 = a.reshape(t, -1)
        i = l // 2
        if l % 2 == 0:
            x2, h2 = _out_proj(a2, p, x2, wa, wp, row(attn_post_g[l]), row(ffn_pre_g[l]))
            x2 = _dense_ffn(h2, x2, _gate_up_chunks(w_gate_d[i], w_up_d[i]), _down_chunks(w_down_d[i]),
                            row(ffn_post_g[l]))
        else:
            wr = jnp.zeros((D_MODEL, LANES), _F32).at[:, :N_EXPERTS].set(w_router[i])
            x2, h_rows, route_i, route_f, counts = _out_proj(
                a2, p, x2, wa, wp, row(attn_post_g[l]), row(ffn_pre_g[l]), wr)
            tile_expert, n_valid, src, pos1, pos2 = _routing_plan(route_i, counts, t)
            y_rows = _expert_ffn(tile_expert, n_valid, src, h_rows,
                                 _gate_up_chunks(w_gate_e[i], w_up_e[i]), _down_chunks(w_down_e[i]))
            x2 = _combine(pos1, pos2, y_rows, x2, route_f, row(ffn_post_g[l]))
    return x2.reshape(b, seq, d)
```

```python
import functools

import jax
import jax.numpy as jnp
from jax import lax
from jax.experimental import pallas as pl
from jax.experimental.pallas import tpu as pltpu

D_MODEL = 1024
CHUNK = 64
EPS = 1e-6
N_HEADS = 6
NOPE_DIM = 128
ROPE_DIM = 64
V_DIM = 128
Q_LORA = 384
KV_LORA = 256
ROPE_THETA = 10000.0
POOL_WINDOWS = (2, 4, 8, 16)
POOL_GROUP_DIM = 64
POOL_WIDTH = len(POOL_WINDOWS) * POOL_GROUP_DIM
N_EXPERTS = 8
MAX_WINDOW = max(POOL_WINDOWS)

LANES = 128
SUBLANES = 8
HEAD_W = 2 * LANES
ROW_SPLIT = D_MODEL // LANES

_C_Q = 0
_C_KV = Q_LORA
_C_KR = Q_LORA + KV_LORA
_C_KRS = _C_KR + LANES
_C_POOL = _C_KRS + LANES
IN_AUG = _C_POOL + POOL_WIDTH

TOKEN_TILE = 512
ATTN_TILE = 256
EXPERT_TILE = 512
EXPERT_FF_STEPS = 2

_F32 = jnp.float32
_BF16 = jnp.bfloat16
_MIB = 1024 * 1024


def _rms(x, g):
    return x * lax.rsqrt(jnp.mean(x * x, axis=-1, keepdims=True) + EPS) * g


def _silu_mul(g, u):
    return g * (1.0 / (1.0 + jnp.exp(-g))) * u


def _params(sem, vmem_mib):
    return pltpu.CompilerParams(dimension_semantics=sem, vmem_limit_bytes=vmem_mib * _MIB)


def _in_kernel(x_ref, g_ref, win_ref, qg_ref, kvg_ref, wq_ref, wkv_ref, tab_ref, pw_ref, ps_ref,
               q_ref, k_ref, v_ref, p_ref, ext_ref, *, tm, tiles_per_seq):
    i = pl.program_id(0)
    h = _rms(x_ref[...], g_ref[...]).astype(_BF16)
    z = jnp.dot(h, win_ref[...], preferred_element_type=_F32)
    cq = _rms(z[:, _C_Q:_C_Q + Q_LORA], qg_ref[...]).astype(_BF16)
    ckv = _rms(z[:, _C_KV:_C_KV + KV_LORA], kvg_ref[...]).astype(_BF16)
    q = jnp.dot(cq, wq_ref[...], preferred_element_type=_F32)
    kv = jnp.dot(ckv, wkv_ref[...], preferred_element_type=_F32)
    tab = tab_ref[...]
    qmul = tab[:, 0:HEAD_W]
    cc = tab[:, HEAD_W:HEAD_W + LANES]
    ss = tab[:, HEAD_W + LANES:HEAD_W + 2 * LANES]
    krr = (z[:, _C_KR:_C_KR + LANES] * cc + z[:, _C_KRS:_C_KRS + LANES] * ss).astype(_BF16)
    for hd in range(N_HEADS):
        q_ref[:, hd * HEAD_W:(hd + 1) * HEAD_W] = (q[:, hd * HEAD_W:(hd + 1) * HEAD_W] * qmul).astype(_BF16)
        k_ref[:, hd * HEAD_W:hd * HEAD_W + NOPE_DIM] = kv[:, hd * NOPE_DIM:(hd + 1) * NOPE_DIM].astype(_BF16)
        k_ref[:, hd * HEAD_W + NOPE_DIM:(hd + 1) * HEAD_W] = krr
    v_ref[...] = kv[:, N_HEADS * NOPE_DIM:].astype(_BF16)

    u = z[:, _C_POOL:_C_POOL + POOL_WIDTH]

    @pl.when(i % tiles_per_seq == 0)
    def _():
        ext_ref[0:MAX_WINDOW, :] = jnp.zeros((MAX_WINDOW, POOL_WIDTH), _F32)

    ext_ref[MAX_WINDOW:MAX_WINDOW + tm, :] = u

    def window_sum(k0, k1, lo):
        acc = ext_ref[MAX_WINDOW - k0:MAX_WINDOW - k0 + tm, lo:lo + LANES]
        for k in range(k0 + 1, k1):
            acc = acc + ext_ref[MAX_WINDOW - k:MAX_WINDOW - k + tm, lo:lo + LANES]
        return acc

    t_pos = (i % tiles_per_seq) * tm + lax.broadcasted_iota(jnp.int32, (tm, LANES), 0)
    cnt = (t_pos + 1).astype(_F32)
    first_group = lax.broadcasted_iota(jnp.int32, (tm, LANES), 1) < POOL_GROUP_DIM
    w0, w1, w2, w3 = POOL_WINDOWS
    s_a = window_sum(0, w0, 0)
    s_b = s_a + window_sum(w0, w1, 0)
    pooled_lo = jnp.where(first_group, s_a / jnp.minimum(cnt, float(w0)), s_b / jnp.minimum(cnt, float(w1)))
    s_c = window_sum(0, w2, LANES)
    s_d = s_c + window_sum(w2, w3, LANES)
    pooled_hi = jnp.where(first_group, s_c / jnp.minimum(cnt, float(w2)), s_d / jnp.minimum(cnt, float(w3)))
    d = jnp.concatenate([pooled_lo - u[:, :LANES], pooled_hi - u[:, LANES:]], axis=1).astype(_BF16)
    y = jnp.dot(d, pw_ref[...], preferred_element_type=_F32) * ps_ref[...]
    p_ref[...] = y.astype(_BF16)
    ext_ref[0:MAX_WINDOW, :] = ext_ref[tm:tm + MAX_WINDOW, :]


def _in_proj(x2, g, win, qg, kvg, wq, wkv, tab, pw, ps, *, seq):
    t = x2.shape[0]
    tm = TOKEN_TILE
    tps = seq // tm
    full = lambda a: pl.BlockSpec(a.shape, lambda i: (0,) * a.ndim)
    return pl.pallas_call(
        functools.partial(_in_kernel, tm=tm, tiles_per_seq=tps),
        grid=(t // tm,),
        in_specs=[pl.BlockSpec((tm, D_MODEL), lambda i: (i, 0)),
                  full(g), full(win), full(qg), full(kvg), full(wq), full(wkv),
                  pl.BlockSpec((tm, tab.shape[1]), lambda i: (i % tps, 0)),
                  full(pw), full(ps)],
        out_specs=[pl.BlockSpec((tm, N_HEADS * HEAD_W), lambda i: (i, 0)),
                   pl.BlockSpec((tm, N_HEADS * HEAD_W), lambda i: (i, 0)),
                   pl.BlockSpec((tm, N_HEADS * V_DIM), lambda i: (i, 0)),
                   pl.BlockSpec((tm, POOL_WIDTH), lambda i: (i, 0))],
        out_shape=[jax.ShapeDtypeStruct((t, N_HEADS * HEAD_W), _BF16),
                   jax.ShapeDtypeStruct((t, N_HEADS * HEAD_W), _BF16),
                   jax.ShapeDtypeStruct((t, N_HEADS * V_DIM), _BF16),
                   jax.ShapeDtypeStruct((t, POOL_WIDTH), _BF16)],
        scratch_shapes=[pltpu.VMEM((tm + MAX_WINDOW, POOL_WIDTH), _F32)],
        compiler_params=_params(("arbitrary",), 48),
        name="in_proj",
    )(x2, g, win, qg, kvg, wq, wkv, tab, pw, ps)


def _attn_kernel(q_ref, k_ref, v_ref, o_ref, vaug_ref, *, tq, nq):
    seq = v_ref.shape[0]
    vaug_ref[:, 0:V_DIM] = v_ref[...]
    ones_col = lax.broadcasted_iota(jnp.int32, (seq, LANES), 1) == 0
    vaug_ref[:, V_DIM:V_DIM + LANES] = jnp.where(ones_col, 1.0, 0.0).astype(_BF16)
    row_chunk = lax.broadcasted_iota(jnp.int32, (tq, tq), 0) // CHUNK
    col_chunk = lax.broadcasted_iota(jnp.int32, (tq, tq), 1) // CHUNK
    visible = col_chunk <= row_chunk
    nt = (((1,), (1,)), ((), ()))
    for i in range(nq):
        d0 = i * tq
        q = q_ref[d0:d0 + tq, :]
        s_diag = lax.dot_general(q, k_ref[d0:d0 + tq, :], nt, preferred_element_type=_F32)
        s_diag = jnp.where(visible, s_diag, -jnp.inf)
        m = jnp.max(s_diag, axis=1, keepdims=True)
        if i > 0:
            s_past = lax.dot_general(q, k_ref[0:d0, :], nt, preferred_element_type=_F32)
            m = jnp.maximum(m, jnp.max(s_past, axis=1, keepdims=True))
        o = jnp.dot(jnp.exp(s_diag - m).astype(_BF16), vaug_ref[d0:d0 + tq, :], preferred_element_type=_F32)
        if i > 0:
            o = o + jnp.dot(jnp.exp(s_past - m).astype(_BF16), vaug_ref[0:d0, :], preferred_element_type=_F32)
        o_ref[d0:d0 + tq, :] = (o[:, 0:V_DIM] / o[:, V_DIM:V_DIM + 1]).astype(_BF16)


def _attention(q, k, v):
    b, seq, _ = q.shape
    tq = ATTN_TILE
    head = lambda w: pl.BlockSpec((None, seq, w), lambda bb, hh: (bb, 0, hh))
    return pl.pallas_call(
        functools.partial(_attn_kernel, tq=tq, nq=seq // tq),
        grid=(b, N_HEADS),
        in_specs=[head(HEAD_W), head(HEAD_W), head(V_DIM)],
        out_specs=head(V_DIM),
        out_shape=jax.ShapeDtypeStruct((b, seq, N_HEADS * V_DIM), _BF16),
        scratch_shapes=[pltpu.VMEM((seq, V_DIM + LANES), _BF16)],
        compiler_params=_params(("arbitrary", "arbitrary"), 40),
        name="attention",
    )(q, k, v)


def _mix_residual(a_ref, p_ref, x_ref, wa_ref, wp_ref, g1_ref):
    mix = (jnp.dot(a_ref[...], wa_ref[...], preferred_element_type=_F32)
           + jnp.dot(p_ref[...], wp_ref[...], preferred_element_type=_F32))
    return x_ref[...] + _rms(mix, g1_ref[...])


def _out_dense_kernel(a_ref, p_ref, x_ref, wa_ref, wp_ref, g1_ref, g2_ref, xo_ref, ho_ref):
    xn = _mix_residual(a_ref, p_ref, x_ref, wa_ref, wp_ref, g1_ref)
    xo_ref[...] = xn
    ho_ref[...] = _rms(xn, g2_ref[...]).astype(_BF16)


def _out_moe_kernel(a_ref, p_ref, x_ref, wa_ref, wp_ref, g1_ref, g2_ref, wr_ref,
                    xo_ref, ho_ref, ri_ref, rf_ref, cnt_ref, carry_ref, *, tm):
    i = pl.program_id(0)
    xn = _mix_residual(a_ref, p_ref, x_ref, wa_ref, wp_ref, g1_ref)
    xo_ref[...] = xn
    h = _rms(xn, g2_ref[...])
    for s in range(ROW_SPLIT):
        ho_ref[pl.ds(s, tm, stride=ROW_SPLIT), :] = h[:, s * LANES:(s + 1) * LANES]

    h_hi = h.astype(_BF16)
    h_lo = (h - h_hi.astype(_F32)).astype(_BF16)
    wr = wr_ref[...]
    both = jnp.dot(h_hi, wr, preferred_element_type=_F32)
    logits = both[:, :LANES] + both[:, LANES:] + jnp.dot(h_lo, wr[:, :LANES], preferred_element_type=_F32)
    lane = lax.broadcasted_iota(jnp.int32, (tm, LANES), 1)
    logits = jnp.where(lane < N_EXPERTS, logits, -jnp.inf)
    v1 = jnp.max(logits, axis=1, keepdims=True)
    e1 = jnp.min(jnp.where(logits == v1, lane, LANES), axis=1, keepdims=True)
    rest = jnp.where(lane == e1, -jnp.inf, logits)
    v2 = jnp.max(rest, axis=1, keepdims=True)
    e2 = jnp.min(jnp.where(rest == v2, lane, LANES), axis=1, keepdims=True)
    ex = jnp.exp(v2 - v1)
    gate1 = 1.0 / (1.0 + ex)
    gate2 = ex * gate1

    @pl.when(i == 0)
    def _():
        carry_ref[...] = jnp.zeros_like(carry_ref)

    sel1 = lane == e1
    sel2 = lane == e2
    member = jnp.logical_or(sel1, sel2).astype(_F32)
    lower = (lax.broadcasted_iota(jnp.int32, (tm, tm), 1)
             < lax.broadcasted_iota(jnp.int32, (tm, tm), 0)).astype(_BF16)
    before = jnp.dot(lower, member.astype(_BF16), preferred_element_type=_F32) + carry_ref[0:1, :]
    r1 = jnp.sum(jnp.where(sel1, before, 0.0), axis=1, keepdims=True).astype(jnp.int32)
    r2 = jnp.sum(jnp.where(sel2, before, 0.0), axis=1, keepdims=True).astype(jnp.int32)
    total = carry_ref[0:1, :] + jnp.sum(member, axis=0, keepdims=True)
    carry_ref[...] = jnp.broadcast_to(total, carry_ref.shape)
    cnt_ref[...] = jnp.broadcast_to(total, cnt_ref.shape)

    ri_ref[...] = jnp.where(lane == 0, e1, jnp.where(lane == 1, e2, jnp.where(lane == 2, r1, r2)))
    rf_ref[...] = jnp.where(lane == 0, gate1, gate2)


def _out_proj(a2, p2, x2, wa, wp, g1, g2, w_router=None):
    t = x2.shape[0]
    tm = TOKEN_TILE
    full = lambda a: pl.BlockSpec(a.shape, lambda i: (0,) * a.ndim)
    row = lambda w: pl.BlockSpec((tm, w), lambda i: (i, 0))
    in_specs = [row(a2.shape[1]), row(p2.shape[1]), row(D_MODEL), full(wa), full(wp), full(g1), full(g2)]
    if w_router is None:
        return pl.pallas_call(
            _out_dense_kernel,
            grid=(t // tm,),
            in_specs=in_specs,
            out_specs=[row(D_MODEL), row(D_MODEL)],
            out_shape=[jax.ShapeDtypeStruct((t, D_MODEL), _F32), jax.ShapeDtypeStruct((t, D_MODEL), _BF16)],
            compiler_params=_params(("arbitrary",), 40),
            name="out_proj_dense",
        )(a2, p2, x2, wa, wp, g1, g2)
    return pl.pallas_call(
        functools.partial(_out_moe_kernel, tm=tm),
        grid=(t // tm,),
        in_specs=in_specs + [full(w_router)],
        out_specs=[row(D_MODEL),
                   pl.BlockSpec((tm * ROW_SPLIT, LANES), lambda i: (i, 0)),
                   row(LANES), row(LANES),
                   pl.BlockSpec((SUBLANES, LANES), lambda i: (0, 0))],
        out_shape=[jax.ShapeDtypeStruct((t, D_MODEL), _F32),
                   jax.ShapeDtypeStruct((t * ROW_SPLIT, LANES), _F32),
                   jax.ShapeDtypeStruct((t, LANES), jnp.int32),
                   jax.ShapeDtypeStruct((t, LANES), _F32),
                   jax.ShapeDtypeStruct((SUBLANES, LANES), _F32)],
        scratch_shapes=[pltpu.VMEM((SUBLANES, LANES), _F32)],
        compiler_params=_params(("arbitrary",), 48),
        name="out_proj_router",
    )(a2, p2, x2, wa, wp, g1, g2, w_router)


def _ffn_kernel(h_ref, x_ref, wg_ref, wu_ref, wd_ref, g_ref, o_ref):
    h = h_ref[...]
    gate = jnp.dot(h, wg_ref[...], preferred_element_type=_F32)
    up = jnp.dot(h, wu_ref[...], preferred_element_type=_F32)
    hm = _silu_mul(gate, up).astype(_BF16)
    f = jnp.dot(hm, wd_ref[...], preferred_element_type=_F32)
    o_ref[...] = x_ref[...] + _rms(f, g_ref[...])


def _dense_ffn(h2, x2, wg, wu, wd, g):
    t = x2.shape[0]
    tm = TOKEN_TILE
    full = lambda a: pl.BlockSpec(a.shape, lambda i: (0,) * a.ndim)
    row = pl.BlockSpec((tm, D_MODEL), lambda i: (i, 0))
    return pl.pallas_call(
        _ffn_kernel,
        grid=(t // tm,),
        in_specs=[row, row, full(wg), full(wu), full(wd), full(g)],
        out_specs=row,
        out_shape=jax.ShapeDtypeStruct((t, D_MODEL), _F32),
        compiler_params=_params(("arbitrary",), 56),
        name="dense_ffn",
    )(h2, x2, wg, wu, wd, g)


def _expert_kernel(te_ref, nv_ref, src_ref, h_hbm, wg_ref, wu_ref, wd_ref, y_ref,
                   buf_ref, xb_ref, acc_ref, sem, *, tm):
    i = pl.program_id(0)
    c = pl.program_id(1)
    last_c = pl.num_programs(1) - 1
    nv = nv_ref[0]
    rows = tm * ROW_SPLIT

    def issue(tile, slot):
        base = tile * tm

        def body(r, carry):
            tok = src_ref[base + r]
            pltpu.make_async_copy(
                h_hbm.at[pl.ds(pl.multiple_of(tok * ROW_SPLIT, ROW_SPLIT), ROW_SPLIT), :],
                buf_ref.at[slot, pl.ds(pl.multiple_of(r * ROW_SPLIT, ROW_SPLIT), ROW_SPLIT), :],
                sem.at[slot]).start()
            return carry

        lax.fori_loop(0, tm, body, 0)

    def fetch(slot):
        if slot == 0:
            @pl.when(i == 0)
            def _():
                issue(0, 0)

        @pl.when(i + 1 < nv)
        def _():
            issue(i + 1, 1 - slot)

        pltpu.make_async_copy(h_hbm.at[pl.ds(0, rows), :], buf_ref.at[slot], sem.at[slot]).wait()
        for s in range(ROW_SPLIT):
            xb_ref[:, s * LANES:(s + 1) * LANES] = buf_ref[slot, pl.ds(s, tm, stride=ROW_SPLIT), :].astype(_BF16)

    for slot in range(2):
        @pl.when(jnp.logical_and(jnp.logical_and(c == 0, i < nv), i % 2 == slot))
        def _():
            fetch(slot)

    @pl.when(i < nv)
    def _():
        x = xb_ref[...]
        gate = jnp.dot(x, wg_ref[...], preferred_element_type=_F32)
        up = jnp.dot(x, wu_ref[...], preferred_element_type=_F32)
        hm = _silu_mul(gate, up).astype(_BF16)
        part = jnp.dot(hm, wd_ref[...], preferred_element_type=_F32)

        @pl.when(c == 0)
        def _():
            acc_ref[...] = part

        @pl.when(jnp.logical_and(c > 0, c < last_c))
        def _():
            acc_ref[...] += part

        @pl.when(c == last_c)
        def _():
            total = acc_ref[...] + part
            for s in range(ROW_SPLIT):
                y_ref[pl.ds(s, tm, stride=ROW_SPLIT), :] = total[:, s * LANES:(s + 1) * LANES]

    @pl.when(jnp.logical_and(c == last_c, i >= nv))
    def _():
        y_ref[...] = jnp.zeros_like(y_ref)


def _expert_ffn(tile_expert, n_valid, src_tok, h_rows, wg, wu, wd):
    tm = EXPERT_TILE
    n_tiles = tile_expert.shape[0]
    steps = EXPERT_FF_STEPS
    assert steps >= 2
    fc = wg.shape[2] // steps

    def step_of(i, c, nv):
        return jnp.where(i < nv[0], c, steps - 1)

    def tile_expert_of(i, te, nv):
        return te[jnp.minimum(i, nv[0] - 1)]

    col_map = lambda i, c, te, nv, src: (tile_expert_of(i, te, nv), 0, step_of(i, c, nv))
    row_map = lambda i, c, te, nv, src: (tile_expert_of(i, te, nv), step_of(i, c, nv), 0)
    grid_spec = pltpu.PrefetchScalarGridSpec(
        num_scalar_prefetch=3,
        grid=(n_tiles, steps),
        in_specs=[pl.BlockSpec(memory_space=pl.ANY),
                  pl.BlockSpec((None, D_MODEL, fc), col_map),
                  pl.BlockSpec((None, D_MODEL, fc), col_map),
                  pl.BlockSpec((None, fc, D_MODEL), row_map)],
        out_specs=pl.BlockSpec((tm * ROW_SPLIT, LANES), lambda i, c, te, nv, src: (i, 0)),
        scratch_shapes=[pltpu.VMEM((2, tm * ROW_SPLIT, LANES), _F32),
                        pltpu.VMEM((tm, D_MODEL), _BF16),
                        pltpu.VMEM((tm, D_MODEL), _F32),
                        pltpu.SemaphoreType.DMA((2,))],
    )
    return pl.pallas_call(
        functools.partial(_expert_kernel, tm=tm),
        grid_spec=grid_spec,
        out_shape=jax.ShapeDtypeStruct((n_tiles * tm * ROW_SPLIT, LANES), _F32),
        compiler_params=_params(("arbitrary", "arbitrary"), 56),
        name="expert_ffn",
    )(tile_expert, n_valid, src_tok, h_rows, wg, wu, wd)


def _combine_kernel(p1_ref, p2_ref, y_hbm, x_ref, rf_ref, g_ref, o_ref, buf_ref, sem, *, tm):
    i = pl.program_id(0)
    n = pl.num_programs(0)
    rows = tm * ROW_SPLIT
    slot = i % 2

    def issue(tile, slot):
        base = tile * tm

        def body(r, carry):
            dst = pl.ds(pl.multiple_of(r * ROW_SPLIT, ROW_SPLIT), ROW_SPLIT)
            for k, pos_ref in enumerate((p1_ref, p2_ref)):
                pos = pos_ref[base + r]
                pltpu.make_async_copy(
                    y_hbm.at[pl.ds(pl.multiple_of(pos * ROW_SPLIT, ROW_SPLIT), ROW_SPLIT), :],
                    buf_ref.at[slot, k, dst, :], sem.at[slot]).start()
            return carry

        lax.fori_loop(0, tm, body, 0)

    @pl.when(i == 0)
    def _():
        issue(0, 0)

    @pl.when(i + 1 < n)
    def _():
        issue(i + 1, 1 - slot)

    for k in range(2):
        pltpu.make_async_copy(y_hbm.at[pl.ds(0, rows), :], buf_ref.at[slot, k], sem.at[slot]).wait()
    rf = rf_ref[...]
    g1 = rf[:, 0:1]
    g2 = rf[:, 1:2]
    parts = []
    for s in range(ROW_SPLIT):
        y1 = buf_ref[slot, 0, pl.ds(s, tm, stride=ROW_SPLIT), :]
        y2 = buf_ref[slot, 1, pl.ds(s, tm, stride=ROW_SPLIT), :]
        parts.append(g1 * y1 + g2 * y2)
    f = jnp.concatenate(parts, axis=1)
    o_ref[...] = x_ref[...] + _rms(f, g_ref[...])


def _combine(pos1, pos2, y_rows, x2, route_f, g):
    t = x2.shape[0]
    tm = TOKEN_TILE
    grid_spec = pltpu.PrefetchScalarGridSpec(
        num_scalar_prefetch=2,
        grid=(t // tm,),
        in_specs=[pl.BlockSpec(memory_space=pl.ANY),
                  pl.BlockSpec((tm, D_MODEL), lambda i, p1, p2: (i, 0)),
                  pl.BlockSpec((tm, LANES), lambda i, p1, p2: (i, 0)),
                  pl.BlockSpec(g.shape, lambda i, p1, p2: (0, 0))],
        out_specs=pl.BlockSpec((tm, D_MODEL), lambda i, p1, p2: (i, 0)),
        scratch_shapes=[pltpu.VMEM((2, 2, tm * ROW_SPLIT, LANES), _F32),
                        pltpu.SemaphoreType.DMA((2,))],
    )
    return pl.pallas_call(
        functools.partial(_combine_kernel, tm=tm),
        grid_spec=grid_spec,
        out_shape=jax.ShapeDtypeStruct((t, D_MODEL), _F32),
        compiler_params=_params(("arbitrary",), 40),
        name="combine",
    )(pos1, pos2, y_rows, x2, route_f, g)


def _swap_halves(w):
    half = w.shape[-1] // 2
    return jnp.concatenate([w[..., half:], w[..., :half]], axis=-1)


def _layer_weights(w_in, w_uq, w_ukv, pool_w, w_out):
    kr = w_in[:, _C_KR:_C_KR + ROPE_DIM]
    krs = _swap_halves(kr)
    win = jnp.concatenate([w_in[:, :_C_KR], kr, kr, krs, krs, w_in[:, _C_KR + ROPE_DIM:]], axis=1)
    q_rope = w_uq[:, :, NOPE_DIM:]
    wq = jnp.concatenate([w_uq[:, :, :NOPE_DIM], q_rope, _swap_halves(q_rope)], axis=2)
    wq = wq.reshape(Q_LORA, N_HEADS * HEAD_W)
    wkv = jnp.concatenate([w_ukv[:, :, :NOPE_DIM].reshape(KV_LORA, N_HEADS * NOPE_DIM),
                           w_ukv[:, :, NOPE_DIM:].reshape(KV_LORA, N_HEADS * V_DIM)], axis=1)
    pw = jnp.zeros((POOL_WIDTH, POOL_WIDTH), _F32)
    for gidx in range(len(POOL_WINDOWS)):
        lo = gidx * POOL_GROUP_DIM
        pw = pw.at[lo:lo + POOL_GROUP_DIM, lo:lo + POOL_GROUP_DIM].set(pool_w[gidx])
    wa = w_out[:N_HEADS * V_DIM]
    wp = w_out[N_HEADS * V_DIM:]
    return tuple(a.astype(_BF16) for a in (win, wq, wkv, pw, wa, wp))


def _rope_table(seq):
    inv = 1.0 / (ROPE_THETA ** (jnp.arange(0, ROPE_DIM, 2, dtype=_F32) / ROPE_DIM))
    ang = jnp.arange(seq, dtype=_F32)[:, None] * inv[None, :]
    cos, sin = jnp.cos(ang), jnp.sin(ang)
    cos2 = jnp.concatenate([cos, cos], axis=1)
    sin2 = jnp.concatenate([-sin, sin], axis=1)
    scale = (NOPE_DIM + ROPE_DIM) ** -0.5
    qmul = jnp.concatenate([jnp.full((seq, NOPE_DIM), scale, _F32), scale * cos2, scale * sin2], axis=1)
    return jnp.concatenate([qmul, cos2, cos2, sin2, sin2], axis=1)


def _routing_plan(route_i, counts, t):
    tm = EXPERT_TILE
    n_tiles = (2 * t) // tm + N_EXPERTS
    cnt = counts[0, :N_EXPERTS].astype(jnp.int32)
    padded = ((cnt + tm - 1) // tm) * tm
    ends = jnp.cumsum(padded)
    base = ends - padded
    n_valid = (ends[-1] // tm).astype(jnp.int32).reshape(1)
    tile_start = jnp.arange(n_tiles, dtype=jnp.int32) * tm
    tile_expert = jnp.minimum(jnp.sum(tile_start[:, None] >= ends[None, :], axis=1), N_EXPERTS - 1)
    e1, e2, r1, r2 = (route_i[:, k] for k in range(4))
    onehot = lambda e: (e[:, None] == jnp.arange(N_EXPERTS, dtype=jnp.int32)[None, :]).astype(jnp.int32)
    pos1 = jnp.sum(onehot(e1) * base[None, :], axis=1) + r1
    pos2 = jnp.sum(onehot(e2) * base[None, :], axis=1) + r2
    tok = jnp.arange(t, dtype=jnp.int32)
    src = jnp.zeros((n_tiles * tm,), jnp.int32).at[pos1].set(tok).at[pos2].set(tok)
    return tile_expert.astype(jnp.int32), n_valid, src, pos1.astype(jnp.int32), pos2.astype(jnp.int32)


def kernel(x, attn_pre_g, attn_post_g, ffn_pre_g, ffn_post_g, w_in, q_norm_g, kv_norm_g, w_uq, w_ukv,
           pool_w, pool_scale, w_out, w_gate_d, w_up_d, w_down_d, w_router, w_gate_e, w_up_e, w_down_e):
    b, seq, d = x.shape
    t = b * seq
    depth = w_in.shape[0]
    assert d == D_MODEL and seq % TOKEN_TILE == 0 and seq % ATTN_TILE == 0
    tab = _rope_table(seq)
    row = lambda a: a.reshape(1, -1)
    x2 = x.reshape(t, d)
    for l in range(depth):
        win, wq, wkv, pw, wa, wp = _layer_weights(w_in[l], w_uq[l], w_ukv[l], pool_w[l], w_out[l])
        q, k, v, p = _in_proj(x2, row(attn_pre_g[l]), win, row(q_norm_g[l]), row(kv_norm_g[l]),
                              wq, wkv, tab, pw, row(pool_scale[l]), seq=seq)
        a = _attention(q.reshape(b, seq, -1), k.reshape(b, seq, -1), v.reshape(b, seq, -1))
        a2 = a.reshape(t, -1)
        i = l // 2
        if l % 2 == 0:
            x2, h2 = _out_proj(a2, p, x2, wa, wp, row(attn_post_g[l]), row(ffn_pre_g[l]))
            x2 = _dense_ffn(h2, x2, w_gate_d[i].astype(_BF16), w_up_d[i].astype(_BF16),
                            w_down_d[i].astype(_BF16), row(ffn_post_g[l]))
        else:
            wr32 = jnp.zeros((D_MODEL, LANES), _F32).at[:, :N_EXPERTS].set(w_router[i])
            wr_hi = wr32.astype(_BF16)
            wr = jnp.concatenate([wr_hi, (wr32 - wr_hi.astype(_F32)).astype(_BF16)], axis=1)
            x2, h_rows, route_i, route_f, counts = _out_proj(
                a2, p, x2, wa, wp, row(attn_post_g[l]), row(ffn_pre_g[l]), wr)
            tile_expert, n_valid, src, pos1, pos2 = _routing_plan(route_i, counts, t)
            y_rows = _expert_ffn(tile_expert, n_valid, src, h_rows, w_gate_e[i].astype(_BF16),
                                 w_up_e[i].astype(_BF16), w_down_e[i].astype(_BF16))
            x2 = _combine(pos1, pos2, y_rows, x2, route_f, row(ffn_post_g[l]))
    return x2.reshape(b, seq, d)
```

```python
import functools

import jax
import jax.numpy as jnp
from jax import lax
from jax.experimental import pallas as pl
from jax.experimental.pallas import tpu as pltpu

D_MODEL = 1024
CHUNK = 64
EPS = 1e-6
N_HEADS = 6
NOPE_DIM = 128
ROPE_DIM = 64
V_DIM = 128
Q_LORA = 384
KV_LORA = 256
ROPE_THETA = 10000.0
POOL_WINDOWS = (2, 4, 8, 16)
POOL_GROUP_DIM = 64
POOL_WIDTH = len(POOL_WINDOWS) * POOL_GROUP_DIM
N_EXPERTS = 8
MAX_WINDOW = max(POOL_WINDOWS)

LANES = 128
SUBLANES = 8
HEAD_W = 2 * LANES
ROW_SPLIT = D_MODEL // LANES

_C_Q = 0
_C_KV = Q_LORA
_C_KR = Q_LORA + KV_LORA
_C_KRS = _C_KR + LANES
_C_POOL = _C_KRS + LANES
IN_AUG = _C_POOL + POOL_WIDTH

TOKEN_TILE = 512
ATTN_TILE = 256
EXPERT_TILE = 512
EXPERT_FF_STEPS = 2
EXPERT_FF_SUB = 256

_F32 = jnp.float32
_BF16 = jnp.bfloat16
_MIB = 1024 * 1024


def _rms(x, g):
    return x * lax.rsqrt(jnp.mean(x * x, axis=-1, keepdims=True) + EPS) * g


def _silu_mul(g, u):
    hg = 0.5 * g
    return (hg + hg * jnp.tanh(hg)) * u


def _params(sem, vmem_mib):
    return pltpu.CompilerParams(dimension_semantics=sem, vmem_limit_bytes=vmem_mib * _MIB)


def _in_kernel(x_ref, g_ref, win_ref, qg_ref, kvg_ref, wq_ref, wkv_ref, tab_ref, pw_ref, ps_ref,
               q_ref, k_ref, v_ref, p_ref, ext_ref, *, tm, tiles_per_seq):
    i = pl.program_id(0)
    h = _rms(x_ref[...], g_ref[...]).astype(_BF16)
    z = jnp.dot(h, win_ref[...], preferred_element_type=_F32)
    cq = _rms(z[:, _C_Q:_C_Q + Q_LORA], qg_ref[...]).astype(_BF16)
    ckv = _rms(z[:, _C_KV:_C_KV + KV_LORA], kvg_ref[...]).astype(_BF16)
    q = jnp.dot(cq, wq_ref[...], preferred_element_type=_F32)
    kv = jnp.dot(ckv, wkv_ref[...], preferred_element_type=_F32)
    tab = tab_ref[...]
    qmul = tab[:, 0:HEAD_W]
    cc = tab[:, HEAD_W:HEAD_W + LANES]
    ss = tab[:, HEAD_W + LANES:HEAD_W + 2 * LANES]
    krr = (z[:, _C_KR:_C_KR + LANES] * cc + z[:, _C_KRS:_C_KRS + LANES] * ss).astype(_BF16)
    for hd in range(N_HEADS):
        q_ref[:, hd * HEAD_W:(hd + 1) * HEAD_W] = (q[:, hd * HEAD_W:(hd + 1) * HEAD_W] * qmul).astype(_BF16)
        k_ref[:, hd * HEAD_W:hd * HEAD_W + NOPE_DIM] = kv[:, hd * NOPE_DIM:(hd + 1) * NOPE_DIM].astype(_BF16)
        k_ref[:, hd * HEAD_W + NOPE_DIM:(hd + 1) * HEAD_W] = krr
    v_ref[...] = kv[:, N_HEADS * NOPE_DIM:].astype(_BF16)

    u = z[:, _C_POOL:_C_POOL + POOL_WIDTH]

    @pl.when(i % tiles_per_seq == 0)
    def _():
        ext_ref[0:MAX_WINDOW, :] = jnp.zeros((MAX_WINDOW, POOL_WIDTH), _F32)

    ext_ref[MAX_WINDOW:MAX_WINDOW + tm, :] = u

    def window_sum(k0, k1, lo):
        acc = ext_ref[MAX_WINDOW - k0:MAX_WINDOW - k0 + tm, lo:lo + LANES]
        for k in range(k0 + 1, k1):
            acc = acc + ext_ref[MAX_WINDOW - k:MAX_WINDOW - k + tm, lo:lo + LANES]
        return acc

    t_pos = (i % tiles_per_seq) * tm + lax.broadcasted_iota(jnp.int32, (tm, LANES), 0)
    cnt = (t_pos + 1).astype(_F32)
    first_group = lax.broadcasted_iota(jnp.int32, (tm, LANES), 1) < POOL_GROUP_DIM
    w0, w1, w2, w3 = POOL_WINDOWS
    s_a = window_sum(0, w0, 0)
    s_b = s_a + window_sum(w0, w1, 0)
    pooled_lo = jnp.where(first_group, s_a / jnp.minimum(cnt, float(w0)), s_b / jnp.minimum(cnt, float(w1)))
    s_c = window_sum(0, w2, LANES)
    s_d = s_c + window_sum(w2, w3, LANES)
    pooled_hi = jnp.where(first_group, s_c / jnp.minimum(cnt, float(w2)), s_d / jnp.minimum(cnt, float(w3)))
    d = jnp.concatenate([pooled_lo - u[:, :LANES], pooled_hi - u[:, LANES:]], axis=1).astype(_BF16)
    y = jnp.dot(d, pw_ref[...], preferred_element_type=_F32) * ps_ref[...]
    p_ref[...] = y.astype(_BF16)
    ext_ref[0:MAX_WINDOW, :] = ext_ref[tm:tm + MAX_WINDOW, :]


def _in_proj(x2, g, win, qg, kvg, wq, wkv, tab, pw, ps, *, seq):
    t = x2.shape[0]
    tm = TOKEN_TILE
    tps = seq // tm
    full = lambda a: pl.BlockSpec(a.shape, lambda i: (0,) * a.ndim)
    return pl.pallas_call(
        functools.partial(_in_kernel, tm=tm, tiles_per_seq=tps),
        grid=(t // tm,),
        in_specs=[pl.BlockSpec((tm, D_MODEL), lambda i: (i, 0)),
                  full(g), full(win), full(qg), full(kvg), full(wq), full(wkv),
                  pl.BlockSpec((tm, tab.shape[1]), lambda i: (i % tps, 0)),
                  full(pw), full(ps)],
        out_specs=[pl.BlockSpec((tm, N_HEADS * HEAD_W), lambda i: (i, 0)),
                   pl.BlockSpec((tm, N_HEADS * HEAD_W), lambda i: (i, 0)),
                   pl.BlockSpec((tm, N_HEADS * V_DIM), lambda i: (i, 0)),
                   pl.BlockSpec((tm, POOL_WIDTH), lambda i: (i, 0))],
        out_shape=[jax.ShapeDtypeStruct((t, N_HEADS * HEAD_W), _BF16),
                   jax.ShapeDtypeStruct((t, N_HEADS * HEAD_W), _BF16),
                   jax.ShapeDtypeStruct((t, N_HEADS * V_DIM), _BF16),
                   jax.ShapeDtypeStruct((t, POOL_WIDTH), _BF16)],
        scratch_shapes=[pltpu.VMEM((tm + MAX_WINDOW, POOL_WIDTH), _F32)],
        compiler_params=_params(("arbitrary",), 48),
        name="in_proj",
    )(x2, g, win, qg, kvg, wq, wkv, tab, pw, ps)


def _attn_kernel(q_ref, k_ref, v_ref, o_ref, vaug_ref, *, tq, nq):
    seq = v_ref.shape[0]
    vaug_ref[:, 0:V_DIM] = v_ref[...]
    ones_col = lax.broadcasted_iota(jnp.int32, (seq, LANES), 1) == 0
    vaug_ref[:, V_DIM:V_DIM + LANES] = jnp.where(ones_col, 1.0, 0.0).astype(_BF16)
    row_chunk = lax.broadcasted_iota(jnp.int32, (tq, tq), 0) // CHUNK
    col_chunk = lax.broadcasted_iota(jnp.int32, (tq, tq), 1) // CHUNK
    visible = col_chunk <= row_chunk
    nt = (((1,), (1,)), ((), ()))
    for i in range(nq):
        d0 = i * tq
        q = q_ref[d0:d0 + tq, :]
        s_diag = lax.dot_general(q, k_ref[d0:d0 + tq, :], nt, preferred_element_type=_F32)
        s_diag = jnp.where(visible, s_diag, -jnp.inf)
        m = jnp.max(s_diag, axis=1, keepdims=True)
        if i > 0:
            s_past = lax.dot_general(q, k_ref[0:d0, :], nt, preferred_element_type=_F32)
            m = jnp.maximum(m, jnp.max(s_past, axis=1, keepdims=True))
        o = jnp.dot(jnp.exp(s_diag - m).astype(_BF16), vaug_ref[d0:d0 + tq, :], preferred_element_type=_F32)
        if i > 0:
            o = o + jnp.dot(jnp.exp(s_past - m).astype(_BF16), vaug_ref[0:d0, :], preferred_element_type=_F32)
        o_ref[d0:d0 + tq, :] = (o[:, 0:V_DIM] / o[:, V_DIM:V_DIM + 1]).astype(_BF16)


def _attention(q, k, v):
    b, seq, _ = q.shape
    tq = ATTN_TILE
    head = lambda w: pl.BlockSpec((None, seq, w), lambda bb, hh: (bb, 0, hh))
    return pl.pallas_call(
        functools.partial(_attn_kernel, tq=tq, nq=seq // tq),
        grid=(b, N_HEADS),
        in_specs=[head(HEAD_W), head(HEAD_W), head(V_DIM)],
        out_specs=head(V_DIM),
        out_shape=jax.ShapeDtypeStruct((b, seq, N_HEADS * V_DIM), _BF16),
        scratch_shapes=[pltpu.VMEM((seq, V_DIM + LANES), _BF16)],
        compiler_params=_params(("arbitrary", "arbitrary"), 40),
        name="attention",
    )(q, k, v)


def _mix_residual(a_ref, p_ref, x_ref, wa_ref, wp_ref, g1_ref):
    mix = (jnp.dot(a_ref[...], wa_ref[...], preferred_element_type=_F32)
           + jnp.dot(p_ref[...], wp_ref[...], preferred_element_type=_F32))
    return x_ref[...] + _rms(mix, g1_ref[...])


def _out_dense_kernel(a_ref, p_ref, x_ref, wa_ref, wp_ref, g1_ref, g2_ref, xo_ref, ho_ref):
    xn = _mix_residual(a_ref, p_ref, x_ref, wa_ref, wp_ref, g1_ref)
    xo_ref[...] = xn
    ho_ref[...] = _rms(xn, g2_ref[...]).astype(_BF16)


def _out_moe_kernel(a_ref, p_ref, x_ref, wa_ref, wp_ref, g1_ref, g2_ref, wr_ref,
                    xo_ref, ho_ref, ri_ref, rf_ref, cnt_ref, carry_ref, *, tm):
    i = pl.program_id(0)
    xn = _mix_residual(a_ref, p_ref, x_ref, wa_ref, wp_ref, g1_ref)
    xo_ref[...] = xn
    h = _rms(xn, g2_ref[...])
    for s in range(ROW_SPLIT):
        ho_ref[pl.ds(s, tm, stride=ROW_SPLIT), :] = h[:, s * LANES:(s + 1) * LANES]

    h_hi = h.astype(_BF16)
    h_lo = (h - h_hi.astype(_F32)).astype(_BF16)
    wr = wr_ref[...]
    both = jnp.dot(h_hi, wr, preferred_element_type=_F32)
    logits = both[:, :LANES] + both[:, LANES:] + jnp.dot(h_lo, wr[:, :LANES], preferred_element_type=_F32)
    lane = lax.broadcasted_iota(jnp.int32, (tm, LANES), 1)
    logits = jnp.where(lane < N_EXPERTS, logits, -jnp.inf)
    v1 = jnp.max(logits, axis=1, keepdims=True)
    e1 = jnp.min(jnp.where(logits == v1, lane, LANES), axis=1, keepdims=True)
    rest = jnp.where(lane == e1, -jnp.inf, logits)
    v2 = jnp.max(rest, axis=1, keepdims=True)
    e2 = jnp.min(jnp.where(rest == v2, lane, LANES), axis=1, keepdims=True)
    ex = jnp.exp(v2 - v1)
    gate1 = 1.0 / (1.0 + ex)
    gate2 = ex * gate1

    @pl.when(i == 0)
    def _():
        carry_ref[...] = jnp.zeros_like(carry_ref)

    sel1 = lane == e1
    sel2 = lane == e2
    member = jnp.logical_or(sel1, sel2).astype(_F32)
    lower = (lax.broadcasted_iota(jnp.int32, (tm, tm), 1)
             < lax.broadcasted_iota(jnp.int32, (tm, tm), 0)).astype(_BF16)
    before = jnp.dot(lower, member.astype(_BF16), preferred_element_type=_F32) + carry_ref[0:1, :]
    r1 = jnp.sum(jnp.where(sel1, before, 0.0), axis=1, keepdims=True).astype(jnp.int32)
    r2 = jnp.sum(jnp.where(sel2, before, 0.0), axis=1, keepdims=True).astype(jnp.int32)
    total = carry_ref[0:1, :] + jnp.sum(member, axis=0, keepdims=True)
    carry_ref[...] = jnp.broadcast_to(total, carry_ref.shape)
    cnt_ref[...] = jnp.broadcast_to(total, cnt_ref.shape)

    ri_ref[...] = jnp.where(lane == 0, e1, jnp.where(lane == 1, e2, jnp.where(lane == 2, r1, r2)))
    rf_ref[...] = jnp.where(lane == 0, gate1, gate2)


def _out_proj(a2, p2, x2, wa, wp, g1, g2, w_router=None):
    t = x2.shape[0]
    tm = TOKEN_TILE
    full = lambda a: pl.BlockSpec(a.shape, lambda i: (0,) * a.ndim)
    row = lambda w: pl.BlockSpec((tm, w), lambda i: (i, 0))
    in_specs = [row(a2.shape[1]), row(p2.shape[1]), row(D_MODEL), full(wa), full(wp), full(g1), full(g2)]
    if w_router is None:
        return pl.pallas_call(
            _out_dense_kernel,
            grid=(t // tm,),
            in_specs=in_specs,
            out_specs=[row(D_MODEL), row(D_MODEL)],
            out_shape=[jax.ShapeDtypeStruct((t, D_MODEL), _F32), jax.ShapeDtypeStruct((t, D_MODEL), _BF16)],
            compiler_params=_params(("arbitrary",), 40),
            name="out_proj_dense",
        )(a2, p2, x2, wa, wp, g1, g2)
    return pl.pallas_call(
        functools.partial(_out_moe_kernel, tm=tm),
        grid=(t // tm,),
        in_specs=in_specs + [full(w_router)],
        out_specs=[row(D_MODEL),
                   pl.BlockSpec((tm * ROW_SPLIT, LANES), lambda i: (i, 0)),
                   row(LANES), row(LANES),
                   pl.BlockSpec((SUBLANES, LANES), lambda i: (0, 0))],
        out_shape=[jax.ShapeDtypeStruct((t, D_MODEL), _F32),
                   jax.ShapeDtypeStruct((t * ROW_SPLIT, LANES), _F32),
                   jax.ShapeDtypeStruct((t, LANES), jnp.int32),
                   jax.ShapeDtypeStruct((t, LANES), _F32),
                   jax.ShapeDtypeStruct((SUBLANES, LANES), _F32)],
        scratch_shapes=[pltpu.VMEM((SUBLANES, LANES), _F32)],
        compiler_params=_params(("arbitrary",), 48),
        name="out_proj_router",
    )(a2, p2, x2, wa, wp, g1, g2, w_router)


def _ffn_kernel(h_ref, x_ref, wg_ref, wu_ref, wd_ref, g_ref, o_ref):
    h = h_ref[...]
    gate = jnp.dot(h, wg_ref[...], preferred_element_type=_F32)
    up = jnp.dot(h, wu_ref[...], preferred_element_type=_F32)
    hm = _silu_mul(gate, up).astype(_BF16)
    f = jnp.dot(hm, wd_ref[...], preferred_element_type=_F32)
    o_ref[...] = x_ref[...] + _rms(f, g_ref[...])


def _dense_ffn(h2, x2, wg, wu, wd, g):
    t = x2.shape[0]
    tm = TOKEN_TILE
    full = lambda a: pl.BlockSpec(a.shape, lambda i: (0,) * a.ndim)
    row = pl.BlockSpec((tm, D_MODEL), lambda i: (i, 0))
    return pl.pallas_call(
        _ffn_kernel,
        grid=(t // tm,),
        in_specs=[row, row, full(wg), full(wu), full(wd), full(g)],
        out_specs=row,
        out_shape=jax.ShapeDtypeStruct((t, D_MODEL), _F32),
        compiler_params=_params(("arbitrary",), 56),
        name="dense_ffn",
    )(h2, x2, wg, wu, wd, g)


def _expert_kernel(te_ref, nv_ref, src_ref, h_hbm, wg_ref, wu_ref, wd_ref, y_ref,
                   buf_ref, xb_ref, acc_ref, sem, *, tm, n_tiles, n_steps):
    i = pl.program_id(0)
    c = pl.program_id(1)
    last_c = n_steps - 1
    nv = nv_ref[0]
    rows = tm * ROW_SPLIT
    share = tm // n_steps

    def start_row(tile_base, slot, r):
        tok = src_ref[tile_base + r]
        pltpu.make_async_copy(
            h_hbm.at[pl.ds(pl.multiple_of(tok * ROW_SPLIT, ROW_SPLIT), ROW_SPLIT), :],
            buf_ref.at[slot, pl.ds(pl.multiple_of(r * ROW_SPLIT, ROW_SPLIT), ROW_SPLIT), :],
            sem.at[slot]).start()

    def wait_slot(slot):
        pltpu.make_async_copy(h_hbm.at[pl.ds(0, rows), :], buf_ref.at[slot], sem.at[slot]).wait()

    @pl.when(jnp.logical_and(i == 0, c == 0))
    def _():
        def body(r, carry):
            start_row(0, 0, r)
            return carry

        lax.fori_loop(0, tm, body, 0)

    for slot in range(2):
        @pl.when(jnp.logical_and(jnp.logical_and(c == 0, i <= nv), i % 2 == slot))
        def _():
            wait_slot(slot)
            for s in range(ROW_SPLIT):
                xb_ref[:, s * LANES:(s + 1) * LANES] = (
                    buf_ref[slot, pl.ds(s, tm, stride=ROW_SPLIT), :].astype(_BF16))

    @pl.when(i < nv)
    def _():
        next_base = jnp.minimum(i + 1, n_tiles - 1) * tm
        next_slot = (i + 1) % 2
        for r in range(share):
            start_row(next_base, next_slot, c * share + r)
        x = xb_ref[...]
        part = None
        for k0 in range(0, wg_ref.shape[1], EXPERT_FF_SUB):
            k1 = k0 + EXPERT_FF_SUB
            gate = jnp.dot(x, wg_ref[:, k0:k1], preferred_element_type=_F32)
            up = jnp.dot(x, wu_ref[:, k0:k1], preferred_element_type=_F32)
            hm = _silu_mul(gate, up).astype(_BF16)
            down = jnp.dot(hm, wd_ref[k0:k1, :], preferred_element_type=_F32)
            part = down if part is None else part + down

        @pl.when(c == 0)
        def _():
            acc_ref[...] = part

        @pl.when(jnp.logical_and(c > 0, c < last_c))
        def _():
            acc_ref[...] += part

        @pl.when(c == last_c)
        def _():
            total = acc_ref[...] + part
            for s in range(ROW_SPLIT):
                y_ref[pl.ds(s, tm, stride=ROW_SPLIT), :] = total[:, s * LANES:(s + 1) * LANES]

    @pl.when(jnp.logical_and(c == last_c, i >= nv))
    def _():
        y_ref[...] = jnp.zeros_like(y_ref)

    @pl.when(jnp.logical_and(jnp.logical_and(i == n_tiles - 1, c == last_c), nv == n_tiles))
    def _():
        wait_slot(n_tiles % 2)


def _expert_ffn(tile_expert, n_valid, src_tok, h_rows, wg, wu, wd):
    tm = EXPERT_TILE
    n_tiles = tile_expert.shape[0]
    steps = EXPERT_FF_STEPS
    assert steps >= 2
    fc = wg.shape[2] // steps

    def step_of(i, c, nv):
        return jnp.where(i < nv[0], c, steps - 1)

    def tile_expert_of(i, te, nv):
        return te[jnp.minimum(i, nv[0] - 1)]

    col_map = lambda i, c, te, nv, src: (tile_expert_of(i, te, nv), 0, step_of(i, c, nv))
    row_map = lambda i, c, te, nv, src: (tile_expert_of(i, te, nv), step_of(i, c, nv), 0)
    grid_spec = pltpu.PrefetchScalarGridSpec(
        num_scalar_prefetch=3,
        grid=(n_tiles, steps),
        in_specs=[pl.BlockSpec(memory_space=pl.ANY),
                  pl.BlockSpec((None, D_MODEL, fc), col_map),
                  pl.BlockSpec((None, D_MODEL, fc), col_map),
                  pl.BlockSpec((None, fc, D_MODEL), row_map)],
        out_specs=pl.BlockSpec((tm * ROW_SPLIT, LANES), lambda i, c, te, nv, src: (i, 0)),
        scratch_shapes=[pltpu.VMEM((2, tm * ROW_SPLIT, LANES), _F32),
                        pltpu.VMEM((tm, D_MODEL), _BF16),
                        pltpu.VMEM((tm, D_MODEL), _F32),
                        pltpu.SemaphoreType.DMA((2,))],
    )
    return pl.pallas_call(
        functools.partial(_expert_kernel, tm=tm, n_tiles=n_tiles, n_steps=steps),
        grid_spec=grid_spec,
        out_shape=jax.ShapeDtypeStruct((n_tiles * tm * ROW_SPLIT, LANES), _F32),
        compiler_params=_params(("arbitrary", "arbitrary"), 56),
        name="expert_ffn",
    )(tile_expert, n_valid, src_tok, h_rows, wg, wu, wd)


def _combine_kernel(p1_ref, p2_ref, y_hbm, x_ref, rf_ref, g_ref, o_ref, buf_ref, sem, *, tm, n_tiles):
    i = pl.program_id(0)
    rows = tm * ROW_SPLIT

    def start_pair(tile_base, slot, r):
        if isinstance(r, int):
            dst = pl.ds(r * ROW_SPLIT, ROW_SPLIT)
        else:
            dst = pl.ds(pl.multiple_of(r * ROW_SPLIT, ROW_SPLIT), ROW_SPLIT)
        for k, pos_ref in enumerate((p1_ref, p2_ref)):
            pos = pos_ref[tile_base + r]
            pltpu.make_async_copy(
                y_hbm.at[pl.ds(pl.multiple_of(pos * ROW_SPLIT, ROW_SPLIT), ROW_SPLIT), :],
                buf_ref.at[slot, k, dst, :], sem.at[slot]).start()

    def wait_slot(slot):
        for k in range(2):
            pltpu.make_async_copy(y_hbm.at[pl.ds(0, rows), :], buf_ref.at[slot, k], sem.at[slot]).wait()

    @pl.when(i == 0)
    def _():
        def body(r, carry):
            start_pair(0, 0, r)
            return carry

        lax.fori_loop(0, tm, body, 0)

    for slot in range(2):
        @pl.when(i % 2 == slot)
        def _():
            next_base = jnp.minimum(i + 1, n_tiles - 1) * tm
            for r in range(tm):
                start_pair(next_base, 1 - slot, r)
            wait_slot(slot)
            rf = rf_ref[...]
            g1 = rf[:, 0:1]
            g2 = rf[:, 1:2]
            parts = []
            for s in range(ROW_SPLIT):
                y1 = buf_ref[slot, 0, pl.ds(s, tm, stride=ROW_SPLIT), :]
                y2 = buf_ref[slot, 1, pl.ds(s, tm, stride=ROW_SPLIT), :]
                parts.append(g1 * y1 + g2 * y2)
            f = jnp.concatenate(parts, axis=1)
            o_ref[...] = x_ref[...] + _rms(f, g_ref[...])

    @pl.when(i == n_tiles - 1)
    def _():
        wait_slot(n_tiles % 2)


def _combine(pos1, pos2, y_rows, x2, route_f, g):
    t = x2.shape[0]
    tm = TOKEN_TILE
    grid_spec = pltpu.PrefetchScalarGridSpec(
        num_scalar_prefetch=2,
        grid=(t // tm,),
        in_specs=[pl.BlockSpec(memory_space=pl.ANY),
                  pl.BlockSpec((tm, D_MODEL), lambda i, p1, p2: (i, 0)),
                  pl.BlockSpec((tm, LANES), lambda i, p1, p2: (i, 0)),
                  pl.BlockSpec(g.shape, lambda i, p1, p2: (0, 0))],
        out_specs=pl.BlockSpec((tm, D_MODEL), lambda i, p1, p2: (i, 0)),
        scratch_shapes=[pltpu.VMEM((2, 2, tm * ROW_SPLIT, LANES), _F32),
                        pltpu.SemaphoreType.DMA((2,))],
    )
    return pl.pallas_call(
        functools.partial(_combine_kernel, tm=tm, n_tiles=t // tm),
        grid_spec=grid_spec,
        out_shape=jax.ShapeDtypeStruct((t, D_MODEL), _F32),
        compiler_params=_params(("arbitrary",), 40),
        name="combine",
    )(pos1, pos2, y_rows, x2, route_f, g)


def _swap_halves(w):
    half = w.shape[-1] // 2
    return jnp.concatenate([w[..., half:], w[..., :half]], axis=-1)


def _layer_weights(w_in, w_uq, w_ukv, pool_w, w_out):
    kr = w_in[:, _C_KR:_C_KR + ROPE_DIM]
    krs = _swap_halves(kr)
    win = jnp.concatenate([w_in[:, :_C_KR], kr, kr, krs, krs, w_in[:, _C_KR + ROPE_DIM:]], axis=1)
    q_rope = w_uq[:, :, NOPE_DIM:]
    wq = jnp.concatenate([w_uq[:, :, :NOPE_DIM], q_rope, _swap_halves(q_rope)], axis=2)
    wq = wq.reshape(Q_LORA, N_HEADS * HEAD_W)
    wkv = jnp.concatenate([w_ukv[:, :, :NOPE_DIM].reshape(KV_LORA, N_HEADS * NOPE_DIM),
                           w_ukv[:, :, NOPE_DIM:].reshape(KV_LORA, N_HEADS * V_DIM)], axis=1)
    pw = jnp.zeros((POOL_WIDTH, POOL_WIDTH), _F32)
    for gidx in range(len(POOL_WINDOWS)):
        lo = gidx * POOL_GROUP_DIM
        pw = pw.at[lo:lo + POOL_GROUP_DIM, lo:lo + POOL_GROUP_DIM].set(pool_w[gidx])
    wa = w_out[:N_HEADS * V_DIM]
    wp = w_out[N_HEADS * V_DIM:]
    return tuple(a.astype(_BF16) for a in (win, wq, wkv, pw, wa, wp))


def _rope_table(seq):
    inv = 1.0 / (ROPE_THETA ** (jnp.arange(0, ROPE_DIM, 2, dtype=_F32) / ROPE_DIM))
    ang = jnp.arange(seq, dtype=_F32)[:, None] * inv[None, :]
    cos, sin = jnp.cos(ang), jnp.sin(ang)
    cos2 = jnp.concatenate([cos, cos], axis=1)
    sin2 = jnp.concatenate([-sin, sin], axis=1)
    scale = (NOPE_DIM + ROPE_DIM) ** -0.5
    qmul = jnp.concatenate([jnp.full((seq, NOPE_DIM), scale, _F32), scale * cos2, scale * sin2], axis=1)
    return jnp.concatenate([qmul, cos2, cos2, sin2, sin2], axis=1)


def _routing_plan(route_i, counts, t):
    tm = EXPERT_TILE
    n_tiles = (2 * t) // tm + N_EXPERTS
    cnt = counts[0, :N_EXPERTS].astype(jnp.int32)
    padded = ((cnt + tm - 1) // tm) * tm
    ends = jnp.cumsum(padded)
    base = ends - padded
    n_valid = (ends[-1] // tm).astype(jnp.int32).reshape(1)
    tile_start = jnp.arange(n_tiles, dtype=jnp.int32) * tm
    tile_expert = jnp.minimum(jnp.sum(tile_start[:, None] >= ends[None, :], axis=1), N_EXPERTS - 1)
    e1, e2, r1, r2 = (route_i[:, k] for k in range(4))
    onehot = lambda e: (e[:, None] == jnp.arange(N_EXPERTS, dtype=jnp.int32)[None, :]).astype(jnp.int32)
    pos1 = jnp.sum(onehot(e1) * base[None, :], axis=1) + r1
    pos2 = jnp.sum(onehot(e2) * base[None, :], axis=1) + r2
    tok = jnp.arange(t, dtype=jnp.int32)
    src = jnp.zeros((n_tiles * tm,), jnp.int32).at[jnp.concatenate([pos1, pos2])].set(
        jnp.concatenate([tok, tok]), unique_indices=True)
    return tile_expert.astype(jnp.int32), n_valid, src, pos1.astype(jnp.int32), pos2.astype(jnp.int32)


def kernel(x, attn_pre_g, attn_post_g, ffn_pre_g, ffn_post_g, w_in, q_norm_g, kv_norm_g, w_uq, w_ukv,
           pool_w, pool_scale, w_out, w_gate_d, w_up_d, w_down_d, w_router, w_gate_e, w_up_e, w_down_e):
    b, seq, d = x.shape
    t = b * seq
    depth = w_in.shape[0]
    assert d == D_MODEL and seq % TOKEN_TILE == 0 and seq % ATTN_TILE == 0
    tab = _rope_table(seq)
    row = lambda a: a.reshape(1, -1)
    x2 = x.reshape(t, d)
    for l in range(depth):
        win, wq, wkv, pw, wa, wp = _layer_weights(w_in[l], w_uq[l], w_ukv[l], pool_w[l], w_out[l])
        q, k, v, p = _in_proj(x2, row(attn_pre_g[l]), win, row(q_norm_g[l]), row(kv_norm_g[l]),
                              wq, wkv, tab, pw, row(pool_scale[l]), seq=seq)
        a = _attention(q.reshape(b, seq, -1), k.reshape(b, seq, -1), v.reshape(b, seq, -1))
        a2 = a.reshape(t, -1)
        i = l // 2
        if l % 2 == 0:
            x2, h2 = _out_proj(a2, p, x2, wa, wp, row(attn_post_g[l]), row(ffn_pre_g[l]))
            x2 = _dense_ffn(h2, x2, w_gate_d[i].astype(_BF16), w_up_d[i].astype(_BF16),
                            w_down_d[i].astype(_BF16), row(ffn_post_g[l]))
        else:
            wr32 = jnp.zeros((D_MODEL, LANES), _F32).at[:, :N_EXPERTS].set(w_router[i])
            wr_hi = wr32.astype(_BF16)
            wr = jnp.concatenate([wr_hi, (wr32 - wr_hi.astype(_F32)).astype(_BF16)], axis=1)
            x2, h_rows, route_i, route_f, counts = _out_proj(
                a2, p, x2, wa, wp, row(attn_post_g[l]), row(ffn_pre_g[l]), wr)
            tile_expert, n_valid, src, pos1, pos2 = _routing_plan(route_i, counts, t)
            y_rows = _expert_ffn(tile_expert, n_valid, src, h_rows, w_gate_e[i].astype(_BF16),
                                 w_up_e[i].astype(_BF16), w_down_e[i].astype(_BF16))
            x2 = _combine(pos1, pos2, y_rows, x2, route_f, row(ffn_post_g[l]))
    return x2.reshape(b, seq, d)
```

```python
import functools

import jax
import jax.numpy as jnp
from jax import lax
from jax.experimental import pallas as pl
from jax.experimental.pallas import tpu as pltpu

D_MODEL = 1024
CHUNK = 64
EPS = 1e-6
N_HEADS = 6
NOPE_DIM = 128
ROPE_DIM = 64
V_DIM = 128
Q_LORA = 384
KV_LORA = 256
ROPE_THETA = 10000.0
POOL_WINDOWS = (2, 4, 8, 16)
POOL_GROUP_DIM = 64
POOL_WIDTH = len(POOL_WINDOWS) * POOL_GROUP_DIM
N_EXPERTS = 8
MAX_WINDOW = max(POOL_WINDOWS)

LANES = 128
SUBLANES = 8
HEAD_W = 2 * LANES
ROW_SPLIT = D_MODEL // LANES

_C_Q = 0
_C_KV = Q_LORA
_C_KR = Q_LORA + KV_LORA
_C_KRS = _C_KR + LANES
_C_POOL = _C_KRS + LANES
IN_AUG = _C_POOL + POOL_WIDTH

TOKEN_TILE = 512
ATTN_TILE = 256
EXPERT_TILE = 512
EXPERT_FF_STEPS = 2
EXPERT_FF_SUB = 256
CAST_HEADS_WIDE = (0, 4)
CAST_HEADS_REST = (4, 6)

_F32 = jnp.float32
_BF16 = jnp.bfloat16
_MIB = 1024 * 1024


def _rms(x, g):
    return x * lax.rsqrt(jnp.mean(x * x, axis=-1, keepdims=True) + EPS) * g


def _silu_mul(g, u):
    hg = 0.5 * g
    return (hg + hg * jnp.tanh(hg)) * u


def _params(sem, vmem_mib):
    return pltpu.CompilerParams(dimension_semantics=sem, vmem_limit_bytes=vmem_mib * _MIB)


def _in_kernel(x_ref, g_ref, win_ref, qg_ref, kvg_ref, wq_ref, wkv_ref, tab_ref, pw_ref, ps_ref,
               q_ref, k_ref, v_ref, p_ref, ext_ref, *, tm, tiles_per_seq):
    i = pl.program_id(0)
    h = _rms(x_ref[...], g_ref[...]).astype(_BF16)
    z = jnp.dot(h, win_ref[...], preferred_element_type=_F32)
    cq = _rms(z[:, _C_Q:_C_Q + Q_LORA], qg_ref[...]).astype(_BF16)
    ckv = _rms(z[:, _C_KV:_C_KV + KV_LORA], kvg_ref[...]).astype(_BF16)
    q = jnp.dot(cq, wq_ref[...], preferred_element_type=_F32)
    kv = jnp.dot(ckv, wkv_ref[...], preferred_element_type=_F32)
    tab = tab_ref[...]
    qmul = tab[:, 0:HEAD_W]
    cc = tab[:, HEAD_W:HEAD_W + LANES]
    ss = tab[:, HEAD_W + LANES:HEAD_W + 2 * LANES]
    krr = (z[:, _C_KR:_C_KR + LANES] * cc + z[:, _C_KRS:_C_KRS + LANES] * ss).astype(_BF16)
    for hd in range(N_HEADS):
        q_ref[:, hd * HEAD_W:(hd + 1) * HEAD_W] = (q[:, hd * HEAD_W:(hd + 1) * HEAD_W] * qmul).astype(_BF16)
        k_ref[:, hd * HEAD_W:hd * HEAD_W + NOPE_DIM] = kv[:, hd * NOPE_DIM:(hd + 1) * NOPE_DIM].astype(_BF16)
        k_ref[:, hd * HEAD_W + NOPE_DIM:(hd + 1) * HEAD_W] = krr
    v_ref[...] = kv[:, N_HEADS * NOPE_DIM:].astype(_BF16)

    u = z[:, _C_POOL:_C_POOL + POOL_WIDTH]

    @pl.when(i % tiles_per_seq == 0)
    def _():
        ext_ref[0:MAX_WINDOW, :] = jnp.zeros((MAX_WINDOW, POOL_WIDTH), _F32)

    ext_ref[MAX_WINDOW:MAX_WINDOW + tm, :] = u

    def window_sum(k0, k1, lo):
        acc = ext_ref[MAX_WINDOW - k0:MAX_WINDOW - k0 + tm, lo:lo + LANES]
        for k in range(k0 + 1, k1):
            acc = acc + ext_ref[MAX_WINDOW - k:MAX_WINDOW - k + tm, lo:lo + LANES]
        return acc

    t_pos = (i % tiles_per_seq) * tm + lax.broadcasted_iota(jnp.int32, (tm, LANES), 0)
    cnt = (t_pos + 1).astype(_F32)
    first_group = lax.broadcasted_iota(jnp.int32, (tm, LANES), 1) < POOL_GROUP_DIM
    w0, w1, w2, w3 = POOL_WINDOWS
    s_a = window_sum(0, w0, 0)
    s_b = s_a + window_sum(w0, w1, 0)
    pooled_lo = jnp.where(first_group, s_a / jnp.minimum(cnt, float(w0)), s_b / jnp.minimum(cnt, float(w1)))
    s_c = window_sum(0, w2, LANES)
    s_d = s_c + window_sum(w2, w3, LANES)
    pooled_hi = jnp.where(first_group, s_c / jnp.minimum(cnt, float(w2)), s_d / jnp.minimum(cnt, float(w3)))
    d = jnp.concatenate([pooled_lo - u[:, :LANES], pooled_hi - u[:, LANES:]], axis=1).astype(_BF16)
    y = jnp.dot(d, pw_ref[...], preferred_element_type=_F32) * ps_ref[...]
    p_ref[...] = y.astype(_BF16)
    ext_ref[0:MAX_WINDOW, :] = ext_ref[tm:tm + MAX_WINDOW, :]


def _in_proj(x2, g, win, qg, kvg, wq, wkv, tab, pw, ps, *, seq):
    t = x2.shape[0]
    tm = TOKEN_TILE
    tps = seq // tm
    full = lambda a: pl.BlockSpec(a.shape, lambda i: (0,) * a.ndim)
    return pl.pallas_call(
        functools.partial(_in_kernel, tm=tm, tiles_per_seq=tps),
        grid=(t // tm,),
        in_specs=[pl.BlockSpec((tm, D_MODEL), lambda i: (i, 0)),
                  full(g), full(win), full(qg), full(kvg), full(wq), full(wkv),
                  pl.BlockSpec((tm, tab.shape[1]), lambda i: (i % tps, 0)),
                  full(pw), full(ps)],
        out_specs=[pl.BlockSpec((tm, N_HEADS * HEAD_W), lambda i: (i, 0)),
                   pl.BlockSpec((tm, N_HEADS * HEAD_W), lambda i: (i, 0)),
                   pl.BlockSpec((tm, N_HEADS * V_DIM), lambda i: (i, 0)),
                   pl.BlockSpec((tm, POOL_WIDTH), lambda i: (i, 0))],
        out_shape=[jax.ShapeDtypeStruct((t, N_HEADS * HEAD_W), _BF16),
                   jax.ShapeDtypeStruct((t, N_HEADS * HEAD_W), _BF16),
                   jax.ShapeDtypeStruct((t, N_HEADS * V_DIM), _BF16),
                   jax.ShapeDtypeStruct((t, POOL_WIDTH), _BF16)],
        scratch_shapes=[pltpu.VMEM((tm + MAX_WINDOW, POOL_WIDTH), _F32)],
        compiler_params=_params(("arbitrary",), 48),
        name="in_proj",
    )(x2, g, win, qg, kvg, wq, wkv, tab, pw, ps)


def _attn_kernel(*refs, tq, nq, cast_heads):
    n_cast = len(cast_heads)
    q_ref, k_ref, v_ref = refs[0:3]
    o_ref = refs[3 + n_cast]
    vaug_ref = refs[-1]
    head = pl.program_id(1)
    for (lo, hi), src_ref, dst_ref in zip(cast_heads, refs[3:3 + n_cast], refs[4 + n_cast:4 + 2 * n_cast]):
        @pl.when(jnp.logical_and(head >= lo, head < hi))
        def _():
            dst_ref[...] = src_ref[...].astype(_BF16)

    seq = v_ref.shape[0]
    vaug_ref[:, 0:V_DIM] = v_ref[...]
    ones_col = lax.broadcasted_iota(jnp.int32, (seq, LANES), 1) == 0
    vaug_ref[:, V_DIM:V_DIM + LANES] = jnp.where(ones_col, 1.0, 0.0).astype(_BF16)
    row_chunk = lax.broadcasted_iota(jnp.int32, (tq, tq), 0) // CHUNK
    col_chunk = lax.broadcasted_iota(jnp.int32, (tq, tq), 1) // CHUNK
    visible = col_chunk <= row_chunk
    nt = (((1,), (1,)), ((), ()))
    for i in range(nq):
        d0 = i * tq
        q = q_ref[d0:d0 + tq, :]
        s_diag = lax.dot_general(q, k_ref[d0:d0 + tq, :], nt, preferred_element_type=_F32)
        s_diag = jnp.where(visible, s_diag, -jnp.inf)
        m = jnp.max(s_diag, axis=1, keepdims=True)
        if i > 0:
            s_past = lax.dot_general(q, k_ref[0:d0, :], nt, preferred_element_type=_F32)
            m = jnp.maximum(m, jnp.max(s_past, axis=1, keepdims=True))
        o = jnp.dot(jnp.exp(s_diag - m).astype(_BF16), vaug_ref[d0:d0 + tq, :], preferred_element_type=_F32)
        if i > 0:
            o = o + jnp.dot(jnp.exp(s_past - m).astype(_BF16), vaug_ref[0:d0, :], preferred_element_type=_F32)
        o_ref[d0:d0 + tq, :] = (o[:, 0:V_DIM] / o[:, V_DIM:V_DIM + 1]).astype(_BF16)


def _attention(q, k, v, casts=()):
    b, seq, _ = q.shape
    tq = ATTN_TILE
    head = lambda w: pl.BlockSpec((None, seq, w), lambda bb, hh: (bb, 0, hh))
    cast_specs, cast_shapes = [], []
    for w, lo, hi in casts:
        n_h = hi - lo
        rows, cols = w.shape
        assert rows % (b * n_h * 2 * SUBLANES) == 0
        block_of = lambda bb, hh, lo=lo, n_h=n_h: (bb * n_h + jnp.clip(hh - lo, 0, n_h - 1), 0)
        cast_specs.append(pl.BlockSpec((rows // (b * n_h), cols), block_of))
        cast_shapes.append(jax.ShapeDtypeStruct(w.shape, _BF16))
    outs = pl.pallas_call(
        functools.partial(_attn_kernel, tq=tq, nq=seq // tq, cast_heads=tuple((lo, hi) for _, lo, hi in casts)),
        grid=(b, N_HEADS),
        in_specs=[head(HEAD_W), head(HEAD_W), head(V_DIM)] + cast_specs,
        out_specs=[head(V_DIM)] + cast_specs,
        out_shape=[jax.ShapeDtypeStruct((b, seq, N_HEADS * V_DIM), _BF16)] + cast_shapes,
        scratch_shapes=[pltpu.VMEM((seq, V_DIM + LANES), _BF16)],
        compiler_params=_params(("arbitrary", "arbitrary"), 56),
        name="attention",
    )(q, k, v, *(w for w, _, _ in casts))
    return outs[0], outs[1:]


def _mix_residual(a_ref, p_ref, x_ref, wa_ref, wp_ref, g1_ref):
    mix = (jnp.dot(a_ref[...], wa_ref[...], preferred_element_type=_F32)
           + jnp.dot(p_ref[...], wp_ref[...], preferred_element_type=_F32))
    return x_ref[...] + _rms(mix, g1_ref[...])


def _out_dense_kernel(a_ref, p_ref, x_ref, wa_ref, wp_ref, g1_ref, g2_ref, xo_ref, ho_ref):
    xn = _mix_residual(a_ref, p_ref, x_ref, wa_ref, wp_ref, g1_ref)
    xo_ref[...] = xn
    ho_ref[...] = _rms(xn, g2_ref[...]).astype(_BF16)


def _out_moe_kernel(a_ref, p_ref, x_ref, wa_ref, wp_ref, g1_ref, g2_ref, wr_ref,
                    xo_ref, ho_ref, ri_ref, rf_ref, cnt_ref, carry_ref, *, tm):
    i = pl.program_id(0)
    xn = _mix_residual(a_ref, p_ref, x_ref, wa_ref, wp_ref, g1_ref)
    xo_ref[...] = xn
    h = _rms(xn, g2_ref[...])
    for s in range(ROW_SPLIT):
        ho_ref[pl.ds(s, tm, stride=ROW_SPLIT), :] = h[:, s * LANES:(s + 1) * LANES]

    h_hi = h.astype(_BF16)
    h_lo = (h - h_hi.astype(_F32)).astype(_BF16)
    wr = wr_ref[...]
    nt = (((1,), (1,)), ((), ()))
    by_hi = lax.dot_general(wr, h_hi, nt, preferred_element_type=_F32)
    by_lo = lax.dot_general(wr, h_lo, nt, preferred_element_type=_F32)
    logits = by_hi[0:N_EXPERTS] + by_hi[N_EXPERTS:] + by_lo[0:N_EXPERTS]
    sub = lax.broadcasted_iota(jnp.int32, (N_EXPERTS, tm), 0)
    v1 = jnp.max(logits, axis=0, keepdims=True)
    e1 = jnp.min(jnp.where(logits == v1, sub, N_EXPERTS), axis=0, keepdims=True)
    rest = jnp.where(sub == e1, -jnp.inf, logits)
    v2 = jnp.max(rest, axis=0, keepdims=True)
    e2 = jnp.min(jnp.where(rest == v2, sub, N_EXPERTS), axis=0, keepdims=True)
    ex = jnp.exp(v2 - v1)
    gate1 = 1.0 / (1.0 + ex)
    gate2 = ex * gate1

    @pl.when(i == 0)
    def _():
        carry_ref[...] = jnp.zeros_like(carry_ref)

    sel1 = sub == e1
    sel2 = sub == e2
    member = jnp.logical_or(sel1, sel2).astype(_F32)
    earlier = (lax.broadcasted_iota(jnp.int32, (tm, tm), 0)
               < lax.broadcasted_iota(jnp.int32, (tm, tm), 1)).astype(_BF16)
    member_lhs = jnp.concatenate([member, jnp.zeros_like(member)], axis=0).astype(_BF16)
    carry = carry_ref[:, 0:1]
    before = jnp.dot(member_lhs, earlier, preferred_element_type=_F32)[0:N_EXPERTS] + carry
    r1 = jnp.sum(jnp.where(sel1, before, 0.0), axis=0, keepdims=True).astype(jnp.int32)
    r2 = jnp.sum(jnp.where(sel2, before, 0.0), axis=0, keepdims=True).astype(jnp.int32)
    total = carry + jnp.sum(member, axis=1, keepdims=True)
    carry_ref[...] = jnp.broadcast_to(total, carry_ref.shape)
    cnt_ref[...] = jnp.broadcast_to(total, cnt_ref.shape)

    ri_ref[...] = jnp.where(sub == 0, e1, jnp.where(sub == 1, e2, jnp.where(sub == 2, r1, r2)))
    rf_ref[...] = jnp.where(sub == 0, gate1, gate2)


def _out_proj(a2, p2, x2, wa, wp, g1, g2, w_router=None):
    t = x2.shape[0]
    tm = TOKEN_TILE
    full = lambda a: pl.BlockSpec(a.shape, lambda i: (0,) * a.ndim)
    row = lambda w: pl.BlockSpec((tm, w), lambda i: (i, 0))
    in_specs = [row(a2.shape[1]), row(p2.shape[1]), row(D_MODEL), full(wa), full(wp), full(g1), full(g2)]
    if w_router is None:
        return pl.pallas_call(
            _out_dense_kernel,
            grid=(t // tm,),
            in_specs=in_specs,
            out_specs=[row(D_MODEL), row(D_MODEL)],
            out_shape=[jax.ShapeDtypeStruct((t, D_MODEL), _F32), jax.ShapeDtypeStruct((t, D_MODEL), _BF16)],
            compiler_params=_params(("arbitrary",), 40),
            name="out_proj_dense",
        )(a2, p2, x2, wa, wp, g1, g2)
    return pl.pallas_call(
        functools.partial(_out_moe_kernel, tm=tm),
        grid=(t // tm,),
        in_specs=in_specs + [full(w_router)],
        out_specs=[row(D_MODEL),
                   pl.BlockSpec((tm * ROW_SPLIT, LANES), lambda i: (i, 0)),
                   pl.BlockSpec((N_EXPERTS, tm), lambda i: (0, i)),
                   pl.BlockSpec((N_EXPERTS, tm), lambda i: (0, i)),
                   pl.BlockSpec((N_EXPERTS, LANES), lambda i: (0, 0))],
        out_shape=[jax.ShapeDtypeStruct((t, D_MODEL), _F32),
                   jax.ShapeDtypeStruct((t * ROW_SPLIT, LANES), _F32),
                   jax.ShapeDtypeStruct((N_EXPERTS, t), jnp.int32),
                   jax.ShapeDtypeStruct((N_EXPERTS, t), _F32),
                   jax.ShapeDtypeStruct((N_EXPERTS, LANES), _F32)],
        scratch_shapes=[pltpu.VMEM((N_EXPERTS, LANES), _F32)],
        compiler_params=_params(("arbitrary",), 48),
        name="out_proj_router",
    )(a2, p2, x2, wa, wp, g1, g2, w_router)


def _ffn_kernel(h_ref, x_ref, wg_ref, wu_ref, wd_ref, g_ref, o_ref):
    h = h_ref[...]
    gate = jnp.dot(h, wg_ref[...], preferred_element_type=_F32)
    up = jnp.dot(h, wu_ref[...], preferred_element_type=_F32)
    hm = _silu_mul(gate, up).astype(_BF16)
    f = jnp.dot(hm, wd_ref[...], preferred_element_type=_F32)
    o_ref[...] = x_ref[...] + _rms(f, g_ref[...])


def _dense_ffn(h2, x2, wg, wu, wd, g):
    t = x2.shape[0]
    tm = TOKEN_TILE
    full = lambda a: pl.BlockSpec(a.shape, lambda i: (0,) * a.ndim)
    row = pl.BlockSpec((tm, D_MODEL), lambda i: (i, 0))
    return pl.pallas_call(
        _ffn_kernel,
        grid=(t // tm,),
        in_specs=[row, row, full(wg), full(wu), full(wd), full(g)],
        out_specs=row,
        out_shape=jax.ShapeDtypeStruct((t, D_MODEL), _F32),
        compiler_params=_params(("arbitrary",), 56),
        name="dense_ffn",
    )(h2, x2, wg, wu, wd, g)


def _expert_kernel(te_ref, nv_ref, src_ref, h_hbm, wg_ref, wu_ref, wd_ref, y_ref,
                   buf_ref, xb_ref, acc_ref, sem, *, tm, n_tiles, n_steps):
    i = pl.program_id(0)
    c = pl.program_id(1)
    last_c = n_steps - 1
    nv = nv_ref[0]
    rows = tm * ROW_SPLIT
    share = tm // n_steps

    def start_row(tile_base, slot, r):
        tok = src_ref[tile_base + r]
        pltpu.make_async_copy(
            h_hbm.at[pl.ds(pl.multiple_of(tok * ROW_SPLIT, ROW_SPLIT), ROW_SPLIT), :],
            buf_ref.at[slot, pl.ds(pl.multiple_of(r * ROW_SPLIT, ROW_SPLIT), ROW_SPLIT), :],
            sem.at[slot]).start()

    def wait_slot(slot):
        pltpu.make_async_copy(h_hbm.at[pl.ds(0, rows), :], buf_ref.at[slot], sem.at[slot]).wait()

    @pl.when(jnp.logical_and(i == 0, c == 0))
    def _():
        def body(r, carry):
            start_row(0, 0, r)
            return carry

        lax.fori_loop(0, tm, body, 0)

    for slot in range(2):
        @pl.when(jnp.logical_and(jnp.logical_and(c == 0, i <= nv), i % 2 == slot))
        def _():
            wait_slot(slot)
            for s in range(ROW_SPLIT):
                xb_ref[:, s * LANES:(s + 1) * LANES] = (
                    buf_ref[slot, pl.ds(s, tm, stride=ROW_SPLIT), :].astype(_BF16))

    @pl.when(i < nv)
    def _():
        next_base = jnp.minimum(i + 1, n_tiles - 1) * tm
        next_slot = (i + 1) % 2
        for r in range(share):
            start_row(next_base, next_slot, c * share + r)
        x = xb_ref[...]
        part = None
        for k0 in range(0, wg_ref.shape[1], EXPERT_FF_SUB):
            k1 = k0 + EXPERT_FF_SUB
            gate = jnp.dot(x, wg_ref[:, k0:k1], preferred_element_type=_F32)
            up = jnp.dot(x, wu_ref[:, k0:k1], preferred_element_type=_F32)
            hm = _silu_mul(gate, up).astype(_BF16)
            down = jnp.dot(hm, wd_ref[k0:k1, :], preferred_element_type=_F32)
            part = down if part is None else part + down

        @pl.when(c == 0)
        def _():
            acc_ref[...] = part

        @pl.when(jnp.logical_and(c > 0, c < last_c))
        def _():
            acc_ref[...] += part

        @pl.when(c == last_c)
        def _():
            total = acc_ref[...] + part
            for s in range(ROW_SPLIT):
                y_ref[pl.ds(s, tm, stride=ROW_SPLIT), :] = total[:, s * LANES:(s + 1) * LANES]

    @pl.when(jnp.logical_and(c == last_c, i >= nv))
    def _():
        y_ref[...] = jnp.zeros_like(y_ref)

    @pl.when(jnp.logical_and(jnp.logical_and(i == n_tiles - 1, c == last_c), nv == n_tiles))
    def _():
        wait_slot(n_tiles % 2)


def _expert_ffn(tile_expert, n_valid, src_tok, h_rows, wg, wu, wd):
    tm = EXPERT_TILE
    n_tiles = tile_expert.shape[0]
    steps = EXPERT_FF_STEPS
    assert steps >= 2
    fc = wg.shape[2] // steps

    def step_of(i, c, nv):
        return jnp.where(i < nv[0], c, steps - 1)

    def tile_expert_of(i, te, nv):
        return te[jnp.minimum(i, nv[0] - 1)]

    col_map = lambda i, c, te, nv, src: (tile_expert_of(i, te, nv), 0, step_of(i, c, nv))
    row_map = lambda i, c, te, nv, src: (tile_expert_of(i, te, nv), step_of(i, c, nv), 0)
    grid_spec = pltpu.PrefetchScalarGridSpec(
        num_scalar_prefetch=3,
        grid=(n_tiles, steps),
        in_specs=[pl.BlockSpec(memory_space=pl.ANY),
                  pl.BlockSpec((None, D_MODEL, fc), col_map),
                  pl.BlockSpec((None, D_MODEL, fc), col_map),
                  pl.BlockSpec((None, fc, D_MODEL), row_map)],
        out_specs=pl.BlockSpec((tm * ROW_SPLIT, LANES), lambda i, c, te, nv, src: (i, 0)),
        scratch_shapes=[pltpu.VMEM((2, tm * ROW_SPLIT, LANES), _F32),
                        pltpu.VMEM((tm, D_MODEL), _BF16),
                        pltpu.VMEM((tm, D_MODEL), _F32),
                        pltpu.SemaphoreType.DMA((2,))],
    )
    return pl.pallas_call(
        functools.partial(_expert_kernel, tm=tm, n_tiles=n_tiles, n_steps=steps),
        grid_spec=grid_spec,
        out_shape=jax.ShapeDtypeStruct((n_tiles * tm * ROW_SPLIT, LANES), _F32),
        compiler_params=_params(("arbitrary", "arbitrary"), 56),
        name="expert_ffn",
    )(tile_expert, n_valid, src_tok, h_rows, wg, wu, wd)


def _combine_kernel(p1_ref, p2_ref, y_hbm, x_ref, rf_ref, g_ref, o_ref, buf_ref, sem, *, tm, n_tiles):
    i = pl.program_id(0)
    rows = tm * ROW_SPLIT

    def start_pair(tile_base, slot, r):
        if isinstance(r, int):
            dst = pl.ds(r * ROW_SPLIT, ROW_SPLIT)
        else:
            dst = pl.ds(pl.multiple_of(r * ROW_SPLIT, ROW_SPLIT), ROW_SPLIT)
        for k, pos_ref in enumerate((p1_ref, p2_ref)):
            pos = pos_ref[tile_base + r]
            pltpu.make_async_copy(
                y_hbm.at[pl.ds(pl.multiple_of(pos * ROW_SPLIT, ROW_SPLIT), ROW_SPLIT), :],
                buf_ref.at[slot, k, dst, :], sem.at[slot]).start()

    def wait_slot(slot):
        for k in range(2):
            pltpu.make_async_copy(y_hbm.at[pl.ds(0, rows), :], buf_ref.at[slot, k], sem.at[slot]).wait()

    @pl.when(i == 0)
    def _():
        def body(r, carry):
            start_pair(0, 0, r)
            return carry

        lax.fori_loop(0, tm, body, 0)

    for slot in range(2):
        @pl.when(i % 2 == slot)
        def _():
            next_base = jnp.minimum(i + 1, n_tiles - 1) * tm
            for r in range(tm):
                start_pair(next_base, 1 - slot, r)
            wait_slot(slot)
            rf = rf_ref[...]
            g1 = rf[:, 0:1]
            g2 = rf[:, 1:2]
            parts = []
            for s in range(ROW_SPLIT):
                y1 = buf_ref[slot, 0, pl.ds(s, tm, stride=ROW_SPLIT), :]
                y2 = buf_ref[slot, 1, pl.ds(s, tm, stride=ROW_SPLIT), :]
                parts.append(g1 * y1 + g2 * y2)
            f = jnp.concatenate(parts, axis=1)
            o_ref[...] = x_ref[...] + _rms(f, g_ref[...])

    @pl.when(i == n_tiles - 1)
    def _():
        wait_slot(n_tiles % 2)


def _combine(pos1, pos2, y_rows, x2, route_f, g):
    t = x2.shape[0]
    tm = TOKEN_TILE
    grid_spec = pltpu.PrefetchScalarGridSpec(
        num_scalar_prefetch=2,
        grid=(t // tm,),
        in_specs=[pl.BlockSpec(memory_space=pl.ANY),
                  pl.BlockSpec((tm, D_MODEL), lambda i, p1, p2: (i, 0)),
                  pl.BlockSpec((tm, route_f.shape[1]), lambda i, p1, p2: (i, 0)),
                  pl.BlockSpec(g.shape, lambda i, p1, p2: (0, 0))],
        out_specs=pl.BlockSpec((tm, D_MODEL), lambda i, p1, p2: (i, 0)),
        scratch_shapes=[pltpu.VMEM((2, 2, tm * ROW_SPLIT, LANES), _F32),
                        pltpu.SemaphoreType.DMA((2,))],
    )
    return pl.pallas_call(
        functools.partial(_combine_kernel, tm=tm, n_tiles=t // tm),
        grid_spec=grid_spec,
        out_shape=jax.ShapeDtypeStruct((t, D_MODEL), _F32),
        compiler_params=_params(("arbitrary",), 40),
        name="combine",
    )(pos1, pos2, y_rows, x2, route_f, g)


def _swap_halves(w):
    half = w.shape[-1] // 2
    return jnp.concatenate([w[..., half:], w[..., :half]], axis=-1)


def _layer_weights(w_in, w_uq, w_ukv, pool_w, w_out):
    kr = w_in[:, _C_KR:_C_KR + ROPE_DIM]
    krs = _swap_halves(kr)
    win = jnp.concatenate([w_in[:, :_C_KR], kr, kr, krs, krs, w_in[:, _C_KR + ROPE_DIM:]], axis=1)
    q_rope = w_uq[:, :, NOPE_DIM:]
    wq = jnp.concatenate([w_uq[:, :, :NOPE_DIM], q_rope, _swap_halves(q_rope)], axis=2)
    wq = wq.reshape(Q_LORA, N_HEADS * HEAD_W)
    wkv = jnp.concatenate([w_ukv[:, :, :NOPE_DIM].reshape(KV_LORA, N_HEADS * NOPE_DIM),
                           w_ukv[:, :, NOPE_DIM:].reshape(KV_LORA, N_HEADS * V_DIM)], axis=1)
    pw = jnp.zeros((POOL_WIDTH, POOL_WIDTH), _F32)
    for gidx in range(len(POOL_WINDOWS)):
        lo = gidx * POOL_GROUP_DIM
        pw = pw.at[lo:lo + POOL_GROUP_DIM, lo:lo + POOL_GROUP_DIM].set(pool_w[gidx])
    wa = w_out[:N_HEADS * V_DIM]
    wp = w_out[N_HEADS * V_DIM:]
    return tuple(a.astype(_BF16) for a in (win, wq, wkv, pw, wa, wp))


def _rope_table(seq):
    inv = 1.0 / (ROPE_THETA ** (jnp.arange(0, ROPE_DIM, 2, dtype=_F32) / ROPE_DIM))
    ang = jnp.arange(seq, dtype=_F32)[:, None] * inv[None, :]
    cos, sin = jnp.cos(ang), jnp.sin(ang)
    cos2 = jnp.concatenate([cos, cos], axis=1)
    sin2 = jnp.concatenate([-sin, sin], axis=1)
    scale = (NOPE_DIM + ROPE_DIM) ** -0.5
    qmul = jnp.concatenate([jnp.full((seq, NOPE_DIM), scale, _F32), scale * cos2, scale * sin2], axis=1)
    return jnp.concatenate([qmul, cos2, cos2, sin2, sin2], axis=1)


def _routing_plan(route_i, counts, t):
    tm = EXPERT_TILE
    n_tiles = (2 * t) // tm + N_EXPERTS
    cnt = counts[:, 0].astype(jnp.int32)
    padded = ((cnt + tm - 1) // tm) * tm
    ends = jnp.cumsum(padded)
    base = ends - padded
    n_valid = (ends[-1] // tm).astype(jnp.int32).reshape(1)
    tile_start = jnp.arange(n_tiles, dtype=jnp.int32) * tm
    tile_expert = jnp.minimum(jnp.sum(tile_start[:, None] >= ends[None, :], axis=1), N_EXPERTS - 1)
    e1, e2, pos1, pos2 = (route_i[k] for k in range(4))
    for k in range(N_EXPERTS):
        pos1 = pos1 + jnp.where(e1 == k, base[k], 0)
        pos2 = pos2 + jnp.where(e2 == k, base[k], 0)
    src = _invert_positions(pos1, pos2, n_tiles * tm)
    return tile_expert.astype(jnp.int32), n_valid, src, pos1, pos2


def _invert_kernel(p1_ref, p2_ref, zeros_hbm, src_ref, *, unroll):
    pltpu.sync_copy(zeros_hbm, src_ref)

    def fill(j, carry):
        for u in range(unroll):
            tok = j * unroll + u
            src_ref[p1_ref[tok]] = tok
            src_ref[p2_ref[tok]] = tok
        return carry

    lax.fori_loop(0, p1_ref.shape[0] // unroll, fill, 0)


def _invert_positions(pos1, pos2, n_slots):
    unroll = 8
    assert n_slots % unroll == 0 and pos1.shape[0] % unroll == 0
    smem = pl.BlockSpec(memory_space=pltpu.SMEM)
    return pl.pallas_call(
        functools.partial(_invert_kernel, unroll=unroll),
        in_specs=[smem, smem, pl.BlockSpec(memory_space=pl.ANY)],
        out_specs=smem,
        out_shape=jax.ShapeDtypeStruct((n_slots,), jnp.int32),
        name="invert_positions",
    )(pos1, pos2, jnp.zeros((n_slots,), jnp.int32))


def kernel(x, attn_pre_g, attn_post_g, ffn_pre_g, ffn_post_g, w_in, q_norm_g, kv_norm_g, w_uq, w_ukv,
           pool_w, pool_scale, w_out, w_gate_d, w_up_d, w_down_d, w_router, w_gate_e, w_up_e, w_down_e):
    b, seq, d = x.shape
    t = b * seq
    depth = w_in.shape[0]
    assert d == D_MODEL and seq % TOKEN_TILE == 0 and seq % ATTN_TILE == 0
    tab = _rope_table(seq)
    row = lambda a: a.reshape(1, -1)
    x2 = x.reshape(t, d)
    for l in range(depth):
        win, wq, wkv, pw, wa, wp = _layer_weights(w_in[l], w_uq[l], w_ukv[l], pool_w[l], w_out[l])
        q, k, v, p = _in_proj(x2, row(attn_pre_g[l]), win, row(q_norm_g[l]), row(kv_norm_g[l]),
                              wq, wkv, tab, pw, row(pool_scale[l]), seq=seq)
        i = l // 2
        if l % 2 == 1:
            casts = ((w_up_e[i].reshape(-1, w_up_e.shape[-1]),) + CAST_HEADS_WIDE,
                     (w_down_e[i].reshape(-1, d),) + CAST_HEADS_REST)
        elif l + 1 < depth:
            casts = ((w_gate_e[(l + 1) // 2].reshape(-1, w_gate_e.shape[-1]),) + CAST_HEADS_WIDE,)
        else:
            casts = ()
        a, converted = _attention(q.reshape(b, seq, -1), k.reshape(b, seq, -1), v.reshape(b, seq, -1), casts)
        a2 = a.reshape(t, -1)
        if l % 2 == 0:
            x2, h2 = _out_proj(a2, p, x2, wa, wp, row(attn_post_g[l]), row(ffn_pre_g[l]))
            x2 = _dense_ffn(h2, x2, w_gate_d[i].astype(_BF16), w_up_d[i].astype(_BF16),
                            w_down_d[i].astype(_BF16), row(ffn_post_g[l]))
            next_gate = converted[0].reshape(w_gate_e.shape[1:]) if converted else None
        else:
            wr_t = w_router[i].T
            wr_hi = wr_t.astype(_BF16)
            wr = jnp.concatenate([wr_hi, (wr_t - wr_hi.astype(_F32)).astype(_BF16)], axis=0)
            x2, h_rows, route_i, route_f, counts = _out_proj(
                a2, p, x2, wa, wp, row(attn_post_g[l]), row(ffn_pre_g[l]), wr)
            tile_expert, n_valid, src, pos1, pos2 = _routing_plan(route_i, counts, t)
            y_rows = _expert_ffn(tile_expert, n_valid, src, h_rows, next_gate,
                                 converted[0].reshape(w_up_e.shape[1:]),
                                 converted[1].reshape(w_down_e.shape[1:]))
            x2 = _combine(pos1, pos2, y_rows, x2, route_f[0:2].T, row(ffn_post_g[l]))
    return x2.reshape(b, seq, d)
```

```python
import functools

import jax
import jax.numpy as jnp
from jax import lax
from jax.experimental import pallas as pl
from jax.experimental.pallas import tpu as pltpu

D_MODEL = 1024
CHUNK = 64
EPS = 1e-6
N_HEADS = 6
NOPE_DIM = 128
ROPE_DIM = 64
V_DIM = 128
Q_LORA = 384
KV_LORA = 256
ROPE_THETA = 10000.0
POOL_WINDOWS = (2, 4, 8, 16)
POOL_GROUP_DIM = 64
POOL_WIDTH = len(POOL_WINDOWS) * POOL_GROUP_DIM
N_EXPERTS = 8
MAX_WINDOW = max(POOL_WINDOWS)

LANES = 128
SUBLANES = 8
HEAD_W = 2 * LANES
ROW_SPLIT = D_MODEL // LANES

_C_Q = 0
_C_KV = Q_LORA
_C_KR = Q_LORA + KV_LORA
_C_KRS = _C_KR + LANES
_C_POOL = _C_KRS + LANES
IN_AUG = _C_POOL + POOL_WIDTH

TOKEN_TILE = 512
IN_SUB_ROWS = 256
ATTN_TILE = 256
EXPERT_TILE = 512
EXPERT_FF_STEPS = 2
EXPERT_FF_SUB = 256
CAST_HEADS_WIDE = (0, 4)
CAST_HEADS_REST = (4, 6)

_F32 = jnp.float32
_BF16 = jnp.bfloat16
_MIB = 1024 * 1024


def _rms(x, g):
    return x * lax.rsqrt(jnp.mean(x * x, axis=-1, keepdims=True) + EPS) * g


def _silu_mul(g, u):
    hg = 0.5 * g
    return (hg + hg * jnp.tanh(hg)) * u


def _params(sem, vmem_mib):
    return pltpu.CompilerParams(dimension_semantics=sem, vmem_limit_bytes=vmem_mib * _MIB)


def _in_kernel(x_ref, g_ref, win_ref, qg_ref, kvg_ref, wq_ref, wkv_ref, tab_ref, pw_ref, ps_ref,
               q_ref, k_ref, v_ref, p_ref, ext_ref, *, tm, tiles_per_seq):
    i = pl.program_id(0)

    @pl.when(i % tiles_per_seq == 0)
    def _():
        ext_ref[0:MAX_WINDOW, :] = jnp.zeros((MAX_WINDOW, POOL_WIDTH), _F32)

    for r0 in range(0, tm, IN_SUB_ROWS):
        rows = slice(r0, r0 + IN_SUB_ROWS)
        h = _rms(x_ref[rows, :], g_ref[...]).astype(_BF16)
        z = jnp.dot(h, win_ref[...], preferred_element_type=_F32)
        cq = _rms(z[:, _C_Q:_C_Q + Q_LORA], qg_ref[...]).astype(_BF16)
        ckv = _rms(z[:, _C_KV:_C_KV + KV_LORA], kvg_ref[...]).astype(_BF16)
        q = jnp.dot(cq, wq_ref[...], preferred_element_type=_F32)
        kv = jnp.dot(ckv, wkv_ref[...], preferred_element_type=_F32)
        tab = tab_ref[rows, :]
        qmul = tab[:, 0:HEAD_W]
        cc = tab[:, HEAD_W:HEAD_W + LANES]
        ss = tab[:, HEAD_W + LANES:HEAD_W + 2 * LANES]
        krr = (z[:, _C_KR:_C_KR + LANES] * cc + z[:, _C_KRS:_C_KRS + LANES] * ss).astype(_BF16)
        for hd in range(N_HEADS):
            q_ref[rows, hd * HEAD_W:(hd + 1) * HEAD_W] = (q[:, hd * HEAD_W:(hd + 1) * HEAD_W] * qmul).astype(_BF16)
            k_ref[rows, hd * HEAD_W:hd * HEAD_W + NOPE_DIM] = kv[:, hd * NOPE_DIM:(hd + 1) * NOPE_DIM].astype(_BF16)
            k_ref[rows, hd * HEAD_W + NOPE_DIM:(hd + 1) * HEAD_W] = krr
        v_ref[rows, :] = kv[:, N_HEADS * NOPE_DIM:].astype(_BF16)
        ext_ref[MAX_WINDOW + r0:MAX_WINDOW + r0 + IN_SUB_ROWS, :] = z[:, _C_POOL:_C_POOL + POOL_WIDTH]

    u = ext_ref[MAX_WINDOW:MAX_WINDOW + tm, :]

    def window_sum(k0, k1, lo):
        acc = ext_ref[MAX_WINDOW - k0:MAX_WINDOW - k0 + tm, lo:lo + LANES]
        for k in range(k0 + 1, k1):
            acc = acc + ext_ref[MAX_WINDOW - k:MAX_WINDOW - k + tm, lo:lo + LANES]
        return acc

    t_pos = (i % tiles_per_seq) * tm + lax.broadcasted_iota(jnp.int32, (tm, LANES), 0)
    cnt = (t_pos + 1).astype(_F32)
    first_group = lax.broadcasted_iota(jnp.int32, (tm, LANES), 1) < POOL_GROUP_DIM
    w0, w1, w2, w3 = POOL_WINDOWS
    s_a = window_sum(0, w0, 0)
    s_b = s_a + window_sum(w0, w1, 0)
    pooled_lo = jnp.where(first_group, s_a / jnp.minimum(cnt, float(w0)), s_b / jnp.minimum(cnt, float(w1)))
    s_c = window_sum(0, w2, LANES)
    s_d = s_c + window_sum(w2, w3, LANES)
    pooled_hi = jnp.where(first_group, s_c / jnp.minimum(cnt, float(w2)), s_d / jnp.minimum(cnt, float(w3)))
    d = jnp.concatenate([pooled_lo - u[:, :LANES], pooled_hi - u[:, LANES:]], axis=1).astype(_BF16)
    y = jnp.dot(d, pw_ref[...], preferred_element_type=_F32) * ps_ref[...]
    p_ref[...] = y.astype(_BF16)
    ext_ref[0:MAX_WINDOW, :] = ext_ref[tm:tm + MAX_WINDOW, :]


def _in_proj(x2, g, win, qg, kvg, wq, wkv, tab, pw, ps, *, seq):
    t = x2.shape[0]
    tm = TOKEN_TILE
    tps = seq // tm
    full = lambda a: pl.BlockSpec(a.shape, lambda i: (0,) * a.ndim)
    return pl.pallas_call(
        functools.partial(_in_kernel, tm=tm, tiles_per_seq=tps),
        grid=(t // tm,),
        in_specs=[pl.BlockSpec((tm, D_MODEL), lambda i: (i, 0)),
                  full(g), full(win), full(qg), full(kvg), full(wq), full(wkv),
                  pl.BlockSpec((tm, tab.shape[1]), lambda i: (i % tps, 0)),
                  full(pw), full(ps)],
        out_specs=[pl.BlockSpec((tm, N_HEADS * HEAD_W), lambda i: (i, 0)),
                   pl.BlockSpec((tm, N_HEADS * HEAD_W), lambda i: (i, 0)),
                   pl.BlockSpec((tm, N_HEADS * V_DIM), lambda i: (i, 0)),
                   pl.BlockSpec((tm, POOL_WIDTH), lambda i: (i, 0))],
        out_shape=[jax.ShapeDtypeStruct((t, N_HEADS * HEAD_W), _BF16),
                   jax.ShapeDtypeStruct((t, N_HEADS * HEAD_W), _BF16),
                   jax.ShapeDtypeStruct((t, N_HEADS * V_DIM), _BF16),
                   jax.ShapeDtypeStruct((t, POOL_WIDTH), _BF16)],
        scratch_shapes=[pltpu.VMEM((tm + MAX_WINDOW, POOL_WIDTH), _F32)],
        compiler_params=_params(("arbitrary",), 48),
        name="in_proj",
    )(x2, g, win, qg, kvg, wq, wkv, tab, pw, ps)


def _attn_kernel(*refs, tq, nq, cast_heads):
    n_cast = len(cast_heads)
    q_ref, k_ref, v_ref = refs[0:3]
    o_ref = refs[3 + n_cast]
    vaug_ref = refs[-1]
    head = pl.program_id(1)
    for (lo, hi), src_ref, dst_ref in zip(cast_heads, refs[3:3 + n_cast], refs[4 + n_cast:4 + 2 * n_cast]):
        @pl.when(jnp.logical_and(head >= lo, head < hi))
        def _():
            dst_ref[...] = src_ref[...].astype(_BF16)

    seq = v_ref.shape[0]
    vaug_ref[:, 0:V_DIM] = v_ref[...]
    ones_col = lax.broadcasted_iota(jnp.int32, (seq, LANES), 1) == 0
    vaug_ref[:, V_DIM:V_DIM + LANES] = jnp.where(ones_col, 1.0, 0.0).astype(_BF16)
    row_chunk = lax.broadcasted_iota(jnp.int32, (tq, tq), 0) // CHUNK
    col_chunk = lax.broadcasted_iota(jnp.int32, (tq, tq), 1) // CHUNK
    visible = col_chunk <= row_chunk
    nt = (((1,), (1,)), ((), ()))
    for i in range(nq):
        d0 = i * tq
        q = q_ref[d0:d0 + tq, :]
        s_diag = lax.dot_general(q, k_ref[d0:d0 + tq, :], nt, preferred_element_type=_F32)
        s_diag = jnp.where(visible, s_diag, -jnp.inf)
        m = jnp.max(s_diag, axis=1, keepdims=True)
        if i > 0:
            s_past = lax.dot_general(q, k_ref[0:d0, :], nt, preferred_element_type=_F32)
            m = jnp.maximum(m, jnp.max(s_past, axis=1, keepdims=True))
        o = jnp.dot(jnp.exp(s_diag - m).astype(_BF16), vaug_ref[d0:d0 + tq, :], preferred_element_type=_F32)
        if i > 0:
            o = o + jnp.dot(jnp.exp(s_past - m).astype(_BF16), vaug_ref[0:d0, :], preferred_element_type=_F32)
        o_ref[d0:d0 + tq, :] = (o[:, 0:V_DIM] / o[:, V_DIM:V_DIM + 1]).astype(_BF16)


def _attention(q, k, v, casts=()):
    b, seq, _ = q.shape
    tq = ATTN_TILE
    head = lambda w: pl.BlockSpec((None, seq, w), lambda bb, hh: (bb, 0, hh))
    cast_specs, cast_shapes = [], []
    for w, lo, hi in casts:
        n_h = hi - lo
        rows, cols = w.shape
        assert rows % (b * n_h * 2 * SUBLANES) == 0
        block_of = lambda bb, hh, lo=lo, n_h=n_h: (bb * n_h + jnp.clip(hh - lo, 0, n_h - 1), 0)
        cast_specs.append(pl.BlockSpec((rows // (b * n_h), cols), block_of))
        cast_shapes.append(jax.ShapeDtypeStruct(w.shape, _BF16))
    outs = pl.pallas_call(
        functools.partial(_attn_kernel, tq=tq, nq=seq // tq, cast_heads=tuple((lo, hi) for _, lo, hi in casts)),
        grid=(b, N_HEADS),
        in_specs=[head(HEAD_W), head(HEAD_W), head(V_DIM)] + cast_specs,
        out_specs=[head(V_DIM)] + cast_specs,
        out_shape=[jax.ShapeDtypeStruct((b, seq, N_HEADS * V_DIM), _BF16)] + cast_shapes,
        scratch_shapes=[pltpu.VMEM((seq, V_DIM + LANES), _BF16)],
        compiler_params=_params(("arbitrary", "arbitrary"), 56),
        name="attention",
    )(q, k, v, *(w for w, _, _ in casts))
    return outs[0], outs[1:]


def _mix_residual(a_ref, p_ref, x_ref, wa_ref, wp_ref, g1_ref):
    mix = (jnp.dot(a_ref[...], wa_ref[...], preferred_element_type=_F32)
           + jnp.dot(p_ref[...], wp_ref[...], preferred_element_type=_F32))
    return x_ref[...] + _rms(mix, g1_ref[...])


def _out_dense_kernel(a_ref, p_ref, x_ref, wa_ref, wp_ref, g1_ref, g2_ref, xo_ref, ho_ref):
    xn = _mix_residual(a_ref, p_ref, x_ref, wa_ref, wp_ref, g1_ref)
    xo_ref[...] = xn
    ho_ref[...] = _rms(xn, g2_ref[...]).astype(_BF16)


def _out_moe_kernel(a_ref, p_ref, x_ref, wa_ref, wp_ref, g1_ref, g2_ref, wr_ref,
                    xo_ref, ho_ref, ri_ref, rf_ref, cnt_ref, carry_ref, *, tm):
    i = pl.program_id(0)
    xn = _mix_residual(a_ref, p_ref, x_ref, wa_ref, wp_ref, g1_ref)
    xo_ref[...] = xn
    h = _rms(xn, g2_ref[...])
    for s in range(ROW_SPLIT):
        ho_ref[pl.ds(s, tm, stride=ROW_SPLIT), :] = h[:, s * LANES:(s + 1) * LANES]

    h_hi = h.astype(_BF16)
    h_lo = (h - h_hi.astype(_F32)).astype(_BF16)
    wr = wr_ref[...]
    nt = (((1,), (1,)), ((), ()))
    by_hi = lax.dot_general(wr, h_hi, nt, preferred_element_type=_F32)
    by_lo = lax.dot_general(wr, h_lo, nt, preferred_element_type=_F32)
    logits = by_hi[0:N_EXPERTS] + by_hi[N_EXPERTS:] + by_lo[0:N_EXPERTS]
    sub = lax.broadcasted_iota(jnp.int32, (N_EXPERTS, tm), 0)
    v1 = jnp.max(logits, axis=0, keepdims=True)
    e1 = jnp.min(jnp.where(logits == v1, sub, N_EXPERTS), axis=0, keepdims=True)
    rest = jnp.where(sub == e1, -jnp.inf, logits)
    v2 = jnp.max(rest, axis=0, keepdims=True)
    e2 = jnp.min(jnp.where(rest == v2, sub, N_EXPERTS), axis=0, keepdims=True)
    ex = jnp.exp(v2 - v1)
    gate1 = 1.0 / (1.0 + ex)
    gate2 = ex * gate1

    @pl.when(i == 0)
    def _():
        carry_ref[...] = jnp.zeros_like(carry_ref)

    sel1 = sub == e1
    sel2 = sub == e2
    member = jnp.logical_or(sel1, sel2).astype(_F32)
    earlier = (lax.broadcasted_iota(jnp.int32, (tm, tm), 0)
               < lax.broadcasted_iota(jnp.int32, (tm, tm), 1)).astype(_BF16)
    member_lhs = jnp.concatenate([member, jnp.zeros_like(member)], axis=0).astype(_BF16)
    carry = carry_ref[:, 0:1]
    before = jnp.dot(member_lhs, earlier, preferred_element_type=_F32)[0:N_EXPERTS] + carry
    r1 = jnp.sum(jnp.where(sel1, before, 0.0), axis=0, keepdims=True).astype(jnp.int32)
    r2 = jnp.sum(jnp.where(sel2, before, 0.0), axis=0, keepdims=True).astype(jnp.int32)
    total = carry + jnp.sum(member, axis=1, keepdims=True)
    carry_ref[...] = jnp.broadcast_to(total, carry_ref.shape)
    cnt_ref[...] = jnp.broadcast_to(total, cnt_ref.shape)

    ri_ref[...] = jnp.where(sub == 0, e1, jnp.where(sub == 1, e2, jnp.where(sub == 2, r1, r2)))
    rf_ref[...] = jnp.where(sub == 0, gate1, gate2)


def _out_proj(a2, p2, x2, wa, wp, g1, g2, w_router=None):
    t = x2.shape[0]
    tm = TOKEN_TILE
    full = lambda a: pl.BlockSpec(a.shape, lambda i: (0,) * a.ndim)
    row = lambda w: pl.BlockSpec((tm, w), lambda i: (i, 0))
    in_specs = [row(a2.shape[1]), row(p2.shape[1]), row(D_MODEL), full(wa), full(wp), full(g1), full(g2)]
    if w_router is None:
        return pl.pallas_call(
            _out_dense_kernel,
            grid=(t // tm,),
            in_specs=in_specs,
            out_specs=[row(D_MODEL), row(D_MODEL)],
            out_shape=[jax.ShapeDtypeStruct((t, D_MODEL), _F32), jax.ShapeDtypeStruct((t, D_MODEL), _BF16)],
            compiler_params=_params(("arbitrary",), 40),
            name="out_proj_dense",
        )(a2, p2, x2, wa, wp, g1, g2)
    return pl.pallas_call(
        functools.partial(_out_moe_kernel, tm=tm),
        grid=(t // tm,),
        in_specs=in_specs + [full(w_router)],
        out_specs=[row(D_MODEL),
                   pl.BlockSpec((tm * ROW_SPLIT, LANES), lambda i: (i, 0)),
                   pl.BlockSpec((N_EXPERTS, tm), lambda i: (0, i)),
                   pl.BlockSpec((N_EXPERTS, tm), lambda i: (0, i)),
                   pl.BlockSpec((N_EXPERTS, LANES), lambda i: (0, 0))],
        out_shape=[jax.ShapeDtypeStruct((t, D_MODEL), _F32),
                   jax.ShapeDtypeStruct((t * ROW_SPLIT, LANES), _F32),
                   jax.ShapeDtypeStruct((N_EXPERTS, t), jnp.int32),
                   jax.ShapeDtypeStruct((N_EXPERTS, t), _F32),
                   jax.ShapeDtypeStruct((N_EXPERTS, LANES), _F32)],
        scratch_shapes=[pltpu.VMEM((N_EXPERTS, LANES), _F32)],
        compiler_params=_params(("arbitrary",), 48),
        name="out_proj_router",
    )(a2, p2, x2, wa, wp, g1, g2, w_router)


def _ffn_kernel(h_ref, x_ref, wg_ref, wu_ref, wd_ref, g_ref, o_ref):
    h = h_ref[...]
    gate = jnp.dot(h, wg_ref[...], preferred_element_type=_F32)
    up = jnp.dot(h, wu_ref[...], preferred_element_type=_F32)
    hm = _silu_mul(gate, up).astype(_BF16)
    f = jnp.dot(hm, wd_ref[...], preferred_element_type=_F32)
    o_ref[...] = x_ref[...] + _rms(f, g_ref[...])


def _dense_ffn(h2, x2, wg, wu, wd, g):
    t = x2.shape[0]
    tm = TOKEN_TILE
    full = lambda a: pl.BlockSpec(a.shape, lambda i: (0,) * a.ndim)
    row = pl.BlockSpec((tm, D_MODEL), lambda i: (i, 0))
    return pl.pallas_call(
        _ffn_kernel,
        grid=(t // tm,),
        in_specs=[row, row, full(wg), full(wu), full(wd), full(g)],
        out_specs=row,
        out_shape=jax.ShapeDtypeStruct((t, D_MODEL), _F32),
        compiler_params=_params(("arbitrary",), 56),
        name="dense_ffn",
    )(h2, x2, wg, wu, wd, g)


def _expert_kernel(te_ref, nv_ref, src_ref, h_hbm, wg_ref, wu_ref, wd_ref, y_ref,
                   buf_ref, xb_ref, acc_ref, sem, *, tm, n_tiles, n_steps):
    i = pl.program_id(0)
    c = pl.program_id(1)
    last_c = n_steps - 1
    nv = nv_ref[0]
    rows = tm * ROW_SPLIT

    def start_row(tile_base, slot, r):
        tok = src_ref[tile_base + r]
        pltpu.make_async_copy(
            h_hbm.at[pl.ds(pl.multiple_of(tok * ROW_SPLIT, ROW_SPLIT), ROW_SPLIT), :],
            buf_ref.at[slot, pl.ds(pl.multiple_of(r * ROW_SPLIT, ROW_SPLIT), ROW_SPLIT), :],
            sem.at[slot]).start()

    def wait_slot(slot):
        pltpu.make_async_copy(h_hbm.at[pl.ds(0, rows), :], buf_ref.at[slot], sem.at[slot]).wait()

    @pl.when(jnp.logical_and(i == 0, c == 0))
    def _():
        def body(r, carry):
            start_row(0, 0, r)
            return carry

        lax.fori_loop(0, tm, body, 0)

    for slot in range(2):
        @pl.when(jnp.logical_and(jnp.logical_and(c == 0, i <= nv), i % 2 == slot))
        def _():
            wait_slot(slot)
            for s in range(ROW_SPLIT):
                xb_ref[:, s * LANES:(s + 1) * LANES] = (
                    buf_ref[slot, pl.ds(s, tm, stride=ROW_SPLIT), :].astype(_BF16))

    def swiglu_part():
        x = xb_ref[...]
        part = None
        for k0 in range(0, wg_ref.shape[1], EXPERT_FF_SUB):
            k1 = k0 + EXPERT_FF_SUB
            gate = jnp.dot(x, wg_ref[:, k0:k1], preferred_element_type=_F32)
            up = jnp.dot(x, wu_ref[:, k0:k1], preferred_element_type=_F32)
            hm = _silu_mul(gate, up).astype(_BF16)
            down = jnp.dot(hm, wd_ref[k0:k1, :], preferred_element_type=_F32)
            part = down if part is None else part + down
        return part

    @pl.when(jnp.logical_and(i < nv, c == 0))
    def _():
        next_base = jnp.minimum(i + 1, n_tiles - 1) * tm
        next_slot = (i + 1) % 2
        for r in range(tm):
            start_row(next_base, next_slot, r)
        acc_ref[...] = swiglu_part()

    if n_steps > 2:
        @pl.when(jnp.logical_and(i < nv, jnp.logical_and(c > 0, c < last_c)))
        def _():
            acc_ref[...] += swiglu_part()

    @pl.when(jnp.logical_and(i < nv, c == last_c))
    def _():
        total = acc_ref[...] + swiglu_part()
        for s in range(ROW_SPLIT):
            y_ref[pl.ds(s, tm, stride=ROW_SPLIT), :] = total[:, s * LANES:(s + 1) * LANES]

    @pl.when(jnp.logical_and(c == last_c, i >= nv))
    def _():
        y_ref[...] = jnp.zeros_like(y_ref)

    @pl.when(jnp.logical_and(jnp.logical_and(i == n_tiles - 1, c == last_c), nv == n_tiles))
    def _():
        wait_slot(n_tiles % 2)


def _expert_ffn(tile_expert, n_valid, src_tok, h_rows, wg, wu, wd):
    tm = EXPERT_TILE
    n_tiles = tile_expert.shape[0]
    steps = EXPERT_FF_STEPS
    assert steps >= 2
    fc = wg.shape[2] // steps

    def step_of(i, c, nv):
        return jnp.where(i < nv[0], c, steps - 1)

    def tile_expert_of(i, te, nv):
        return te[jnp.minimum(i, nv[0] - 1)]

    col_map = lambda i, c, te, nv, src: (tile_expert_of(i, te, nv), 0, step_of(i, c, nv))
    row_map = lambda i, c, te, nv, src: (tile_expert_of(i, te, nv), step_of(i, c, nv), 0)
    grid_spec = pltpu.PrefetchScalarGridSpec(
        num_scalar_prefetch=3,
        grid=(n_tiles, steps),
        in_specs=[pl.BlockSpec(memory_space=pl.ANY),
                  pl.BlockSpec((None, D_MODEL, fc), col_map),
                  pl.BlockSpec((None, D_MODEL, fc), col_map),
                  pl.BlockSpec((None, fc, D_MODEL), row_map)],
        out_specs=pl.BlockSpec((tm * ROW_SPLIT, LANES), lambda i, c, te, nv, src: (i, 0)),
        scratch_shapes=[pltpu.VMEM((2, tm * ROW_SPLIT, LANES), _F32),
                        pltpu.VMEM((tm, D_MODEL), _BF16),
                        pltpu.VMEM((tm, D_MODEL), _F32),
                        pltpu.SemaphoreType.DMA((2,))],
    )
    return pl.pallas_call(
        functools.partial(_expert_kernel, tm=tm, n_tiles=n_tiles, n_steps=steps),
        grid_spec=grid_spec,
        out_shape=jax.ShapeDtypeStruct((n_tiles * tm * ROW_SPLIT, LANES), _F32),
        compiler_params=_params(("arbitrary", "arbitrary"), 56),
        name="expert_ffn",
    )(tile_expert, n_valid, src_tok, h_rows, wg, wu, wd)


def _combine_kernel(p1_ref, p2_ref, y_hbm, x_ref, rf_ref, g_ref, o_ref, buf_ref, sem, *, tm, n_tiles):
    i = pl.program_id(0)
    rows = tm * ROW_SPLIT

    def start_pair(tile_base, slot, r):
        if isinstance(r, int):
            dst = pl.ds(r * ROW_SPLIT, ROW_SPLIT)
        else:
            dst = pl.ds(pl.multiple_of(r * ROW_SPLIT, ROW_SPLIT), ROW_SPLIT)
        for k, pos_ref in enumerate((p1_ref, p2_ref)):
            pos = pos_ref[tile_base + r]
            pltpu.make_async_copy(
                y_hbm.at[pl.ds(pl.multiple_of(pos * ROW_SPLIT, ROW_SPLIT), ROW_SPLIT), :],
                buf_ref.at[slot, k, dst, :], sem.at[slot]).start()

    def wait_slot(slot):
        for k in range(2):
            pltpu.make_async_copy(y_hbm.at[pl.ds(0, rows), :], buf_ref.at[slot, k], sem.at[slot]).wait()

    @pl.when(i == 0)
    def _():
        def body(r, carry):
            start_pair(0, 0, r)
            return carry

        lax.fori_loop(0, tm, body, 0)

    for slot in range(2):
        @pl.when(i % 2 == slot)
        def _():
            next_base = jnp.minimum(i + 1, n_tiles - 1) * tm
            for r in range(tm):
                start_pair(next_base, 1 - slot, r)
            wait_slot(slot)
            rf = rf_ref[...]
            g1 = rf[:, 0:1]
            g2 = rf[:, 1:2]
            parts = []
            for s in range(ROW_SPLIT):
                y1 = buf_ref[slot, 0, pl.ds(s, tm, stride=ROW_SPLIT), :]
                y2 = buf_ref[slot, 1, pl.ds(s, tm, stride=ROW_SPLIT), :]
                parts.append(g1 * y1 + g2 * y2)
            f = jnp.concatenate(parts, axis=1)
            o_ref[...] = x_ref[...] + _rms(f, g_ref[...])

    @pl.when(i == n_tiles - 1)
    def _():
        wait_slot(n_tiles % 2)


def _combine(pos1, pos2, y_rows, x2, route_f, g):
    t = x2.shape[0]
    tm = TOKEN_TILE
    grid_spec = pltpu.PrefetchScalarGridSpec(
        num_scalar_prefetch=2,
        grid=(t // tm,),
        in_specs=[pl.BlockSpec(memory_space=pl.ANY),
                  pl.BlockSpec((tm, D_MODEL), lambda i, p1, p2: (i, 0)),
                  pl.BlockSpec((tm, route_f.shape[1]), lambda i, p1, p2: (i, 0)),
                  pl.BlockSpec(g.shape, lambda i, p1, p2: (0, 0))],
        out_specs=pl.BlockSpec((tm, D_MODEL), lambda i, p1, p2: (i, 0)),
        scratch_shapes=[pltpu.VMEM((2, 2, tm * ROW_SPLIT, LANES), _F32),
                        pltpu.SemaphoreType.DMA((2,))],
    )
    return pl.pallas_call(
        functools.partial(_combine_kernel, tm=tm, n_tiles=t // tm),
        grid_spec=grid_spec,
        out_shape=jax.ShapeDtypeStruct((t, D_MODEL), _F32),
        compiler_params=_params(("arbitrary",), 40),
        name="combine",
    )(pos1, pos2, y_rows, x2, route_f, g)


def _swap_halves(w):
    half = w.shape[-1] // 2
    return jnp.concatenate([w[..., half:], w[..., :half]], axis=-1)


def _layer_weights(w_in, w_uq, w_ukv, pool_w, w_out):
    kr = w_in[:, _C_KR:_C_KR + ROPE_DIM]
    krs = _swap_halves(kr)
    win = jnp.concatenate([w_in[:, :_C_KR], kr, kr, krs, krs, w_in[:, _C_KR + ROPE_DIM:]], axis=1)
    q_rope = w_uq[:, :, NOPE_DIM:]
    wq = jnp.concatenate([w_uq[:, :, :NOPE_DIM], q_rope, _swap_halves(q_rope)], axis=2)
    wq = wq.reshape(Q_LORA, N_HEADS * HEAD_W)
    wkv = jnp.concatenate([w_ukv[:, :, :NOPE_DIM].reshape(KV_LORA, N_HEADS * NOPE_DIM),
                           w_ukv[:, :, NOPE_DIM:].reshape(KV_LORA, N_HEADS * V_DIM)], axis=1)
    pw = jnp.zeros((POOL_WIDTH, POOL_WIDTH), _F32)
    for gidx in range(len(POOL_WINDOWS)):
        lo = gidx * POOL_GROUP_DIM
        pw = pw.at[lo:lo + POOL_GROUP_DIM, lo:lo + POOL_GROUP_DIM].set(pool_w[gidx])
    wa = w_out[:N_HEADS * V_DIM]
    wp = w_out[N_HEADS * V_DIM:]
    return tuple(a.astype(_BF16) for a in (win, wq, wkv, pw, wa, wp))


def _rope_table(seq):
    inv = 1.0 / (ROPE_THETA ** (jnp.arange(0, ROPE_DIM, 2, dtype=_F32) / ROPE_DIM))
    ang = jnp.arange(seq, dtype=_F32)[:, None] * inv[None, :]
    cos, sin = jnp.cos(ang), jnp.sin(ang)
    cos2 = jnp.concatenate([cos, cos], axis=1)
    sin2 = jnp.concatenate([-sin, sin], axis=1)
    scale = (NOPE_DIM + ROPE_DIM) ** -0.5
    qmul = jnp.concatenate([jnp.full((seq, NOPE_DIM), scale, _F32), scale * cos2, scale * sin2], axis=1)
    return jnp.concatenate([qmul, cos2, cos2, sin2, sin2], axis=1)


def _routing_plan(route_i, counts, t):
    tm = EXPERT_TILE
    n_tiles = (2 * t) // tm + N_EXPERTS
    cnt = counts[:, 0].astype(jnp.int32)
    padded = ((cnt + tm - 1) // tm) * tm
    ends = jnp.cumsum(padded)
    base = ends - padded
    n_valid = (ends[-1] // tm).astype(jnp.int32).reshape(1)
    tile_start = jnp.arange(n_tiles, dtype=jnp.int32) * tm
    tile_expert = jnp.minimum(jnp.sum(tile_start[:, None] >= ends[None, :], axis=1), N_EXPERTS - 1)
    e1, e2, pos1, pos2 = (route_i[k] for k in range(4))
    for k in range(N_EXPERTS):
        pos1 = pos1 + jnp.where(e1 == k, base[k], 0)
        pos2 = pos2 + jnp.where(e2 == k, base[k], 0)
    src = _invert_positions(pos1, pos2, n_tiles * tm)
    return tile_expert.astype(jnp.int32), n_valid, src, pos1, pos2


def _invert_kernel(p1_ref, p2_ref, zeros_hbm, src_ref, *, unroll):
    pltpu.sync_copy(zeros_hbm, src_ref)

    def fill(j, carry):
        for u in range(unroll):
            tok = j * unroll + u
            src_ref[p1_ref[tok]] = tok
            src_ref[p2_ref[tok]] = tok
        return carry

    lax.fori_loop(0, p1_ref.shape[0] // unroll, fill, 0)


def _invert_positions(pos1, pos2, n_slots):
    unroll = 8
    assert n_slots % unroll == 0 and pos1.shape[0] % unroll == 0
    smem = pl.BlockSpec(memory_space=pltpu.SMEM)
    return pl.pallas_call(
        functools.partial(_invert_kernel, unroll=unroll),
        in_specs=[smem, smem, pl.BlockSpec(memory_space=pl.ANY)],
        out_specs=smem,
        out_shape=jax.ShapeDtypeStruct((n_slots,), jnp.int32),
        name="invert_positions",
    )(pos1, pos2, jnp.zeros((n_slots,), jnp.int32))


def kernel(x, attn_pre_g, attn_post_g, ffn_pre_g, ffn_post_g, w_in, q_norm_g, kv_norm_g, w_uq, w_ukv,
           pool_w, pool_scale, w_out, w_gate_d, w_up_d, w_down_d, w_router, w_gate_e, w_up_e, w_down_e):
    b, seq, d = x.shape
    t = b * seq
    depth = w_in.shape[0]
    assert d == D_MODEL and seq % TOKEN_TILE == 0 and seq % ATTN_TILE == 0
    tab = _rope_table(seq)
    row = lambda a: a.reshape(1, -1)
    x2 = x.reshape(t, d)
    for l in range(depth):
        win, wq, wkv, pw, wa, wp = _layer_weights(w_in[l], w_uq[l], w_ukv[l], pool_w[l], w_out[l])
        q, k, v, p = _in_proj(x2, row(attn_pre_g[l]), win, row(q_norm_g[l]), row(kv_norm_g[l]),
                              wq, wkv, tab, pw, row(pool_scale[l]), seq=seq)
        i = l // 2
        if l % 2 == 1:
            casts = ((w_up_e[i].reshape(-1, w_up_e.shape[-1]),) + CAST_HEADS_WIDE,
                     (w_down_e[i].reshape(-1, d),) + CAST_HEADS_REST)
        elif l + 1 < depth:
            casts = ((w_gate_e[(l + 1) // 2].reshape(-1, w_gate_e.shape[-1]),) + CAST_HEADS_WIDE,)
        else:
            casts = ()
        a, converted = _attention(q.reshape(b, seq, -1), k.reshape(b, seq, -1), v.reshape(b, seq, -1), casts)
        a2 = a.reshape(t, -1)
        if l % 2 == 0:
            x2, h2 = _out_proj(a2, p, x2, wa, wp, row(attn_post_g[l]), row(ffn_pre_g[l]))
            x2 = _dense_ffn(h2, x2, w_gate_d[i].astype(_BF16), w_up_d[i].astype(_BF16),
                            w_down_d[i].astype(_BF16), row(ffn_post_g[l]))
            next_gate = converted[0].reshape(w_gate_e.shape[1:]) if converted else None
        else:
            wr_t = w_router[i].T
            wr_hi = wr_t.astype(_BF16)
            wr = jnp.concatenate([wr_hi, (wr_t - wr_hi.astype(_F32)).astype(_BF16)], axis=0)
            x2, h_rows, route_i, route_f, counts = _out_proj(
                a2, p, x2, wa, wp, row(attn_post_g[l]), row(ffn_pre_g[l]), wr)
            tile_expert, n_valid, src, pos1, pos2 = _routing_plan(route_i, counts, t)
            y_rows = _expert_ffn(tile_expert, n_valid, src, h_rows, next_gate,
                                 converted[0].reshape(w_up_e.shape[1:]),
                                 converted[1].reshape(w_down_e.shape[1:]))
            x2 = _combine(pos1, pos2, y_rows, x2, route_f[0:2].T, row(ffn_post_g[l]))
    return x2.reshape(b, seq, d)
```

```python
import functools

import jax
import jax.numpy as jnp
from jax import lax
from jax.experimental import pallas as pl
from jax.experimental.pallas import tpu as pltpu

D_MODEL = 1024
CHUNK = 64
EPS = 1e-6
N_HEADS = 6
NOPE_DIM = 128
ROPE_DIM = 64
V_DIM = 128
Q_LORA = 384
KV_LORA = 256
ROPE_THETA = 10000.0
POOL_WINDOWS = (2, 4, 8, 16)
POOL_GROUP_DIM = 64
POOL_WIDTH = len(POOL_WINDOWS) * POOL_GROUP_DIM
N_EXPERTS = 8
MAX_WINDOW = max(POOL_WINDOWS)

LANES = 128
SUBLANES = 8
HEAD_W = 2 * LANES
ROW_SPLIT = D_MODEL // LANES

_C_Q = 0
_C_KV = Q_LORA
_C_KR = Q_LORA + KV_LORA
_C_KRS = _C_KR + LANES
_C_POOL = _C_KRS + LANES
IN_AUG = _C_POOL + POOL_WIDTH

TOKEN_TILE = 512
IN_SUB_ROWS = 256
ATTN_TILE = 256
EXPERT_TILE = 512
EXPERT_FF_STEPS = 2
EXPERT_FF_SUB = 256
CAST_HEADS_WIDE = (0, 4)
CAST_HEADS_REST = (4, 6)

_F32 = jnp.float32
_BF16 = jnp.bfloat16
_MIB = 1024 * 1024


def _rms(x, g):
    return x * lax.rsqrt(jnp.mean(x * x, axis=-1, keepdims=True) + EPS) * g


def _silu_mul(g, u):
    hg = 0.5 * g
    return (hg + hg * jnp.tanh(hg)) * u


def _params(sem, vmem_mib):
    return pltpu.CompilerParams(dimension_semantics=sem, vmem_limit_bytes=vmem_mib * _MIB)


def _in_kernel(x_ref, g_ref, win_ref, qg_ref, kvg_ref, wq_ref, wkv_ref, tab_ref, pw_ref, ps_ref,
               q_ref, k_ref, v_ref, p_ref, ext_ref, *, tm, tiles_per_seq):
    i = pl.program_id(0)

    @pl.when(i % tiles_per_seq == 0)
    def _():
        ext_ref[0:MAX_WINDOW, :] = jnp.zeros((MAX_WINDOW, POOL_WIDTH), _F32)

    for r0 in range(0, tm, IN_SUB_ROWS):
        rows = slice(r0, r0 + IN_SUB_ROWS)
        h = _rms(x_ref[rows, :], g_ref[...]).astype(_BF16)
        z = jnp.dot(h, win_ref[...], preferred_element_type=_F32)
        cq = _rms(z[:, _C_Q:_C_Q + Q_LORA], qg_ref[...]).astype(_BF16)
        ckv = _rms(z[:, _C_KV:_C_KV + KV_LORA], kvg_ref[...]).astype(_BF16)
        q = jnp.dot(cq, wq_ref[...], preferred_element_type=_F32)
        kv = jnp.dot(ckv, wkv_ref[...], preferred_element_type=_F32)
        tab = tab_ref[rows, :]
        qmul = tab[:, 0:HEAD_W]
        cc = tab[:, HEAD_W:HEAD_W + LANES]
        ss = tab[:, HEAD_W + LANES:HEAD_W + 2 * LANES]
        krr = (z[:, _C_KR:_C_KR + LANES] * cc + z[:, _C_KRS:_C_KRS + LANES] * ss).astype(_BF16)
        for hd in range(N_HEADS):
            q_ref[rows, hd * HEAD_W:(hd + 1) * HEAD_W] = (q[:, hd * HEAD_W:(hd + 1) * HEAD_W] * qmul).astype(_BF16)
            k_ref[rows, hd * HEAD_W:hd * HEAD_W + NOPE_DIM] = kv[:, hd * NOPE_DIM:(hd + 1) * NOPE_DIM].astype(_BF16)
            k_ref[rows, hd * HEAD_W + NOPE_DIM:(hd + 1) * HEAD_W] = krr
        v_ref[rows, :] = kv[:, N_HEADS * NOPE_DIM:].astype(_BF16)
        ext_ref[MAX_WINDOW + r0:MAX_WINDOW + r0 + IN_SUB_ROWS, :] = z[:, _C_POOL:_C_POOL + POOL_WIDTH]

    u = ext_ref[MAX_WINDOW:MAX_WINDOW + tm, :]

    def window_sum(k0, k1, lo):
        acc = ext_ref[MAX_WINDOW - k0:MAX_WINDOW - k0 + tm, lo:lo + LANES]
        for k in range(k0 + 1, k1):
            acc = acc + ext_ref[MAX_WINDOW - k:MAX_WINDOW - k + tm, lo:lo + LANES]
        return acc

    t_pos = (i % tiles_per_seq) * tm + lax.broadcasted_iota(jnp.int32, (tm, LANES), 0)
    cnt = (t_pos + 1).astype(_F32)
    first_group = lax.broadcasted_iota(jnp.int32, (tm, LANES), 1) < POOL_GROUP_DIM
    w0, w1, w2, w3 = POOL_WINDOWS
    s_a = window_sum(0, w0, 0)
    s_b = s_a + window_sum(w0, w1, 0)
    pooled_lo = jnp.where(first_group, s_a / jnp.minimum(cnt, float(w0)), s_b / jnp.minimum(cnt, float(w1)))
    s_c = window_sum(0, w2, LANES)
    s_d = s_c + window_sum(w2, w3, LANES)
    pooled_hi = jnp.where(first_group, s_c / jnp.minimum(cnt, float(w2)), s_d / jnp.minimum(cnt, float(w3)))
    d = jnp.concatenate([pooled_lo - u[:, :LANES], pooled_hi - u[:, LANES:]], axis=1).astype(_BF16)
    y = jnp.dot(d, pw_ref[...], preferred_element_type=_F32) * ps_ref[...]
    p_ref[...] = y.astype(_BF16)
    ext_ref[0:MAX_WINDOW, :] = ext_ref[tm:tm + MAX_WINDOW, :]


def _in_proj(x2, g, win, qg, kvg, wq, wkv, tab, pw, ps, *, seq):
    t = x2.shape[0]
    tm = TOKEN_TILE
    tps = seq // tm
    full = lambda a: pl.BlockSpec(a.shape, lambda i: (0,) * a.ndim)
    return pl.pallas_call(
        functools.partial(_in_kernel, tm=tm, tiles_per_seq=tps),
        grid=(t // tm,),
        in_specs=[pl.BlockSpec((tm, D_MODEL), lambda i: (i, 0)),
                  full(g), full(win), full(qg), full(kvg), full(wq), full(wkv),
                  pl.BlockSpec((tm, tab.shape[1]), lambda i: (i % tps, 0)),
                  full(pw), full(ps)],
        out_specs=[pl.BlockSpec((tm, N_HEADS * HEAD_W), lambda i: (i, 0)),
                   pl.BlockSpec((tm, N_HEADS * HEAD_W), lambda i: (i, 0)),
                   pl.BlockSpec((tm, N_HEADS * V_DIM), lambda i: (i, 0)),
                   pl.BlockSpec((tm, POOL_WIDTH), lambda i: (i, 0))],
        out_shape=[jax.ShapeDtypeStruct((t, N_HEADS * HEAD_W), _BF16),
                   jax.ShapeDtypeStruct((t, N_HEADS * HEAD_W), _BF16),
                   jax.ShapeDtypeStruct((t, N_HEADS * V_DIM), _BF16),
                   jax.ShapeDtypeStruct((t, POOL_WIDTH), _BF16)],
        scratch_shapes=[pltpu.VMEM((tm + MAX_WINDOW, POOL_WIDTH), _F32)],
        compiler_params=_params(("arbitrary",), 48),
        name="in_proj",
    )(x2, g, win, qg, kvg, wq, wkv, tab, pw, ps)


def _attn_kernel(*refs, tq, nq, cast_heads):
    n_cast = len(cast_heads)
    q_ref, k_ref, v_ref = refs[0:3]
    o_ref = refs[3 + n_cast]
    vaug_ref = refs[-1]
    head = pl.program_id(1)
    for (lo, hi), src_ref, dst_ref in zip(cast_heads, refs[3:3 + n_cast], refs[4 + n_cast:4 + 2 * n_cast]):
        @pl.when(jnp.logical_and(head >= lo, head < hi))
        def _():
            dst_ref[...] = src_ref[...].astype(_BF16)

    seq = v_ref.shape[0]
    vaug_ref[:, 0:V_DIM] = v_ref[...]
    ones_col = lax.broadcasted_iota(jnp.int32, (seq, LANES), 1) == 0
    vaug_ref[:, V_DIM:V_DIM + LANES] = jnp.where(ones_col, 1.0, 0.0).astype(_BF16)
    row_chunk = lax.broadcasted_iota(jnp.int32, (tq, tq), 0) // CHUNK
    col_chunk = lax.broadcasted_iota(jnp.int32, (tq, tq), 1) // CHUNK
    visible = col_chunk <= row_chunk
    nt = (((1,), (1,)), ((), ()))
    def scores(i):
        d0 = i * tq
        s = lax.dot_general(q_ref[d0:d0 + tq, :], k_ref[0:d0 + tq, :], nt, preferred_element_type=_F32)
        s_diag = jnp.where(visible, s[:, d0:], -jnp.inf)
        return s_diag if i == 0 else jnp.concatenate([s[:, :d0], s_diag], axis=1)

    s_next = scores(0)
    for i in range(nq):
        s = s_next
        if i + 1 < nq:
            s_next = scores(i + 1)
        d0 = i * tq
        p = jnp.exp(s - jnp.max(s, axis=1, keepdims=True)).astype(_BF16)
        o = jnp.dot(p, vaug_ref[0:d0 + tq, :], preferred_element_type=_F32)
        o_ref[d0:d0 + tq, :] = (o[:, 0:V_DIM] / o[:, V_DIM:V_DIM + 1]).astype(_BF16)


def _attention(q, k, v, casts=()):
    b, seq, _ = q.shape
    tq = ATTN_TILE
    head = lambda w: pl.BlockSpec((None, seq, w), lambda bb, hh: (bb, 0, hh))
    cast_specs, cast_shapes = [], []
    for w, lo, hi in casts:
        n_h = hi - lo
        rows, cols = w.shape
        assert rows % (b * n_h * 2 * SUBLANES) == 0
        block_of = lambda bb, hh, lo=lo, n_h=n_h: (bb * n_h + jnp.clip(hh - lo, 0, n_h - 1), 0)
        cast_specs.append(pl.BlockSpec((rows // (b * n_h), cols), block_of))
        cast_shapes.append(jax.ShapeDtypeStruct(w.shape, _BF16))
    outs = pl.pallas_call(
        functools.partial(_attn_kernel, tq=tq, nq=seq // tq, cast_heads=tuple((lo, hi) for _, lo, hi in casts)),
        grid=(b, N_HEADS),
        in_specs=[head(HEAD_W), head(HEAD_W), head(V_DIM)] + cast_specs,
        out_specs=[head(V_DIM)] + cast_specs,
        out_shape=[jax.ShapeDtypeStruct((b, seq, N_HEADS * V_DIM), _BF16)] + cast_shapes,
        scratch_shapes=[pltpu.VMEM((seq, V_DIM + LANES), _BF16)],
        compiler_params=_params(("arbitrary", "arbitrary"), 56),
        name="attention",
    )(q, k, v, *(w for w, _, _ in casts))
    return outs[0], outs[1:]


def _mix_residual(a_ref, p_ref, x_ref, wa_ref, wp_ref, g1_ref):
    mix = (jnp.dot(a_ref[...], wa_ref[...], preferred_element_type=_F32)
           + jnp.dot(p_ref[...], wp_ref[...], preferred_element_type=_F32))
    return x_ref[...] + _rms(mix, g1_ref[...])


def _out_dense_kernel(a_ref, p_ref, x_ref, wa_ref, wp_ref, g1_ref, g2_ref,
                      wg_ref, wu_ref, wd_ref, g3_ref, o_ref):
    xn = _mix_residual(a_ref, p_ref, x_ref, wa_ref, wp_ref, g1_ref)
    h = _rms(xn, g2_ref[...]).astype(_BF16)
    gate = jnp.dot(h, wg_ref[...], preferred_element_type=_F32)
    up = jnp.dot(h, wu_ref[...], preferred_element_type=_F32)
    hm = _silu_mul(gate, up).astype(_BF16)
    f = jnp.dot(hm, wd_ref[...], preferred_element_type=_F32)
    o_ref[...] = xn + _rms(f, g3_ref[...])


def _out_moe_kernel(a_ref, p_ref, x_ref, wa_ref, wp_ref, g1_ref, g2_ref, wr_ref,
                    xo_ref, ho_ref, ri_ref, rf_ref, cnt_ref, carry_ref, *, tm):
    i = pl.program_id(0)
    xn = _mix_residual(a_ref, p_ref, x_ref, wa_ref, wp_ref, g1_ref)
    xo_ref[...] = xn
    h = _rms(xn, g2_ref[...])
    for s in range(ROW_SPLIT):
        ho_ref[pl.ds(s, tm, stride=ROW_SPLIT), :] = h[:, s * LANES:(s + 1) * LANES]

    h_hi = h.astype(_BF16)
    h_lo = (h - h_hi.astype(_F32)).astype(_BF16)
    wr = wr_ref[...]
    nt = (((1,), (1,)), ((), ()))
    by_hi = lax.dot_general(wr, h_hi, nt, preferred_element_type=_F32)
    by_lo = lax.dot_general(wr, h_lo, nt, preferred_element_type=_F32)
    logits = by_hi[0:N_EXPERTS] + by_hi[N_EXPERTS:] + by_lo[0:N_EXPERTS]
    sub = lax.broadcasted_iota(jnp.int32, (N_EXPERTS, tm), 0)
    v1 = jnp.max(logits, axis=0, keepdims=True)
    e1 = jnp.min(jnp.where(logits == v1, sub, N_EXPERTS), axis=0, keepdims=True)
    rest = jnp.where(sub == e1, -jnp.inf, logits)
    v2 = jnp.max(rest, axis=0, keepdims=True)
    e2 = jnp.min(jnp.where(rest == v2, sub, N_EXPERTS), axis=0, keepdims=True)
    ex = jnp.exp(v2 - v1)
    gate1 = 1.0 / (1.0 + ex)
    gate2 = ex * gate1

    @pl.when(i == 0)
    def _():
        carry_ref[...] = jnp.zeros_like(carry_ref)

    sel1 = sub == e1
    sel2 = sub == e2
    member = jnp.logical_or(sel1, sel2).astype(_F32)
    earlier = (lax.broadcasted_iota(jnp.int32, (tm, tm), 0)
               < lax.broadcasted_iota(jnp.int32, (tm, tm), 1)).astype(_BF16)
    member_lhs = jnp.concatenate([member, jnp.zeros_like(member)], axis=0).astype(_BF16)
    carry = carry_ref[:, 0:1]
    before = jnp.dot(member_lhs, earlier, preferred_element_type=_F32)[0:N_EXPERTS] + carry
    r1 = jnp.sum(jnp.where(sel1, before, 0.0), axis=0, keepdims=True).astype(jnp.int32)
    r2 = jnp.sum(jnp.where(sel2, before, 0.0), axis=0, keepdims=True).astype(jnp.int32)
    total = carry + jnp.sum(member, axis=1, keepdims=True)
    carry_ref[...] = jnp.broadcast_to(total, carry_ref.shape)
    cnt_ref[...] = jnp.broadcast_to(total, cnt_ref.shape)

    ri_ref[...] = jnp.where(sub == 0, e1, jnp.where(sub == 1, e2, jnp.where(sub == 2, r1, r2)))
    rf_ref[...] = jnp.where(sub == 0, gate1, gate2)


def _out_proj(a2, p2, x2, wa, wp, g1, g2, w_router=None, dense=None):
    t = x2.shape[0]
    tm = TOKEN_TILE
    full = lambda a: pl.BlockSpec(a.shape, lambda i: (0,) * a.ndim)
    resident = lambda a: pl.BlockSpec(a.shape, lambda i: (0,) * a.ndim, pipeline_mode=pl.Buffered(1))
    row = lambda w: pl.BlockSpec((tm, w), lambda i: (i, 0))
    in_specs = [row(a2.shape[1]), row(p2.shape[1]), row(D_MODEL), full(wa), full(wp), full(g1), full(g2)]
    if w_router is None:
        wg, wu, wd, g3 = dense
        return pl.pallas_call(
            _out_dense_kernel,
            grid=(t // tm,),
            in_specs=in_specs + [resident(wg), resident(wu), resident(wd), full(g3)],
            out_specs=row(D_MODEL),
            out_shape=jax.ShapeDtypeStruct((t, D_MODEL), _F32),
            compiler_params=_params(("arbitrary",), 56),
            name="out_proj_dense_ffn",
        )(a2, p2, x2, wa, wp, g1, g2, wg, wu, wd, g3)
    return pl.pallas_call(
        functools.partial(_out_moe_kernel, tm=tm),
        grid=(t // tm,),
        in_specs=in_specs + [full(w_router)],
        out_specs=[row(D_MODEL),
                   pl.BlockSpec((tm * ROW_SPLIT, LANES), lambda i: (i, 0)),
                   pl.BlockSpec((N_EXPERTS, tm), lambda i: (0, i)),
                   pl.BlockSpec((N_EXPERTS, tm), lambda i: (0, i)),
                   pl.BlockSpec((N_EXPERTS, LANES), lambda i: (0, 0))],
        out_shape=[jax.ShapeDtypeStruct((t, D_MODEL), _F32),
                   jax.ShapeDtypeStruct((t * ROW_SPLIT, LANES), _F32),
                   jax.ShapeDtypeStruct((N_EXPERTS, t), jnp.int32),
                   jax.ShapeDtypeStruct((N_EXPERTS, t), _F32),
                   jax.ShapeDtypeStruct((N_EXPERTS, LANES), _F32)],
        scratch_shapes=[pltpu.VMEM((N_EXPERTS, LANES), _F32)],
        compiler_params=_params(("arbitrary",), 48),
        name="out_proj_router",
    )(a2, p2, x2, wa, wp, g1, g2, w_router)


def _expert_kernel(te_ref, ch_ref, nv_ref, src_ref, h_hbm, wg_ref, wu_ref, wd_ref, y_ref,
                   buf_ref, xb_ref, acc_ref, sem, *, tm, n_tiles, n_steps):
    i = pl.program_id(0)
    c = pl.program_id(1)
    last_c = n_steps - 1
    nv = nv_ref[0]
    rows = tm * ROW_SPLIT

    def start_row(tile_base, slot, r):
        tok = src_ref[tile_base + r]
        pltpu.make_async_copy(
            h_hbm.at[pl.ds(pl.multiple_of(tok * ROW_SPLIT, ROW_SPLIT), ROW_SPLIT), :],
            buf_ref.at[slot, pl.ds(pl.multiple_of(r * ROW_SPLIT, ROW_SPLIT), ROW_SPLIT), :],
            sem.at[slot]).start()

    def wait_slot(slot):
        pltpu.make_async_copy(h_hbm.at[pl.ds(0, rows), :], buf_ref.at[slot], sem.at[slot]).wait()

    @pl.when(jnp.logical_and(i == 0, c == 0))
    def _():
        def body(r, carry):
            start_row(0, 0, r)
            return carry

        lax.fori_loop(0, tm, body, 0)

    for slot in range(2):
        @pl.when(jnp.logical_and(jnp.logical_and(c == 0, i <= nv), i % 2 == slot))
        def _():
            wait_slot(slot)
            for s in range(ROW_SPLIT):
                xb_ref[:, s * LANES:(s + 1) * LANES] = (
                    buf_ref[slot, pl.ds(s, tm, stride=ROW_SPLIT), :].astype(_BF16))

    def swiglu_part():
        x = xb_ref[...]
        part = None
        for k0 in range(0, wg_ref.shape[1], EXPERT_FF_SUB):
            k1 = k0 + EXPERT_FF_SUB
            gate = jnp.dot(x, wg_ref[:, k0:k1], preferred_element_type=_F32)
            up = jnp.dot(x, wu_ref[:, k0:k1], preferred_element_type=_F32)
            hm = _silu_mul(gate, up).astype(_BF16)
            down = jnp.dot(hm, wd_ref[k0:k1, :], preferred_element_type=_F32)
            part = down if part is None else part + down
        return part

    @pl.when(jnp.logical_and(i < nv, c == 0))
    def _():
        next_base = jnp.minimum(i + 1, n_tiles - 1) * tm
        next_slot = (i + 1) % 2
        for r in range(tm):
            start_row(next_base, next_slot, r)
        acc_ref[...] = swiglu_part()

    if n_steps > 2:
        @pl.when(jnp.logical_and(i < nv, jnp.logical_and(c > 0, c < last_c)))
        def _():
            acc_ref[...] += swiglu_part()

    @pl.when(jnp.logical_and(i < nv, c == last_c))
    def _():
        total = acc_ref[...] + swiglu_part()
        for s in range(ROW_SPLIT):
            y_ref[pl.ds(s, tm, stride=ROW_SPLIT), :] = total[:, s * LANES:(s + 1) * LANES]

    @pl.when(jnp.logical_and(c == last_c, i >= nv))
    def _():
        y_ref[...] = jnp.zeros_like(y_ref)

    @pl.when(jnp.logical_and(jnp.logical_and(i == n_tiles - 1, c == last_c), nv == n_tiles))
    def _():
        wait_slot(n_tiles % 2)


def _expert_ffn(tile_expert, step_chunk, n_valid, src_tok, h_rows, wg, wu, wd):
    tm = EXPERT_TILE
    n_tiles = tile_expert.shape[0]
    steps = EXPERT_FF_STEPS
    assert steps >= 2
    fc = wg.shape[2] // steps

    col_map = lambda i, c, te, ch, nv, src: (te[i], 0, ch[i * steps + c])
    row_map = lambda i, c, te, ch, nv, src: (te[i], ch[i * steps + c], 0)
    grid_spec = pltpu.PrefetchScalarGridSpec(
        num_scalar_prefetch=4,
        grid=(n_tiles, steps),
        in_specs=[pl.BlockSpec(memory_space=pl.ANY),
                  pl.BlockSpec((None, D_MODEL, fc), col_map),
                  pl.BlockSpec((None, D_MODEL, fc), col_map),
                  pl.BlockSpec((None, fc, D_MODEL), row_map)],
        out_specs=pl.BlockSpec((tm * ROW_SPLIT, LANES), lambda i, c, te, ch, nv, src: (i, 0)),
        scratch_shapes=[pltpu.VMEM((2, tm * ROW_SPLIT, LANES), _F32),
                        pltpu.VMEM((tm, D_MODEL), _BF16),
                        pltpu.VMEM((tm, D_MODEL), _F32),
                        pltpu.SemaphoreType.DMA((2,))],
    )
    return pl.pallas_call(
        functools.partial(_expert_kernel, tm=tm, n_tiles=n_tiles, n_steps=steps),
        grid_spec=grid_spec,
        out_shape=jax.ShapeDtypeStruct((n_tiles * tm * ROW_SPLIT, LANES), _F32),
        compiler_params=_params(("arbitrary", "arbitrary"), 56),
        name="expert_ffn",
    )(tile_expert, step_chunk, n_valid, src_tok, h_rows, wg, wu, wd)


def _combine_kernel(p1_ref, p2_ref, y_hbm, x_ref, rf_ref, g_ref, o_ref, buf_ref, sem, *, tm, n_tiles):
    i = pl.program_id(0)
    rows = tm * ROW_SPLIT

    def start_pair(tile_base, slot, r):
        if isinstance(r, int):
            dst = pl.ds(r * ROW_SPLIT, ROW_SPLIT)
        else:
            dst = pl.ds(pl.multiple_of(r * ROW_SPLIT, ROW_SPLIT), ROW_SPLIT)
        for k, pos_ref in enumerate((p1_ref, p2_ref)):
            pos = pos_ref[tile_base + r]
            pltpu.make_async_copy(
                y_hbm.at[pl.ds(pl.multiple_of(pos * ROW_SPLIT, ROW_SPLIT), ROW_SPLIT), :],
                buf_ref.at[slot, k, dst, :], sem.at[slot]).start()

    def wait_slot(slot):
        for k in range(2):
            pltpu.make_async_copy(y_hbm.at[pl.ds(0, rows), :], buf_ref.at[slot, k], sem.at[slot]).wait()

    @pl.when(i == 0)
    def _():
        def body(r, carry):
            start_pair(0, 0, r)
            return carry

        lax.fori_loop(0, tm, body, 0)

    for slot in range(2):
        @pl.when(i % 2 == slot)
        def _():
            next_base = jnp.minimum(i + 1, n_tiles - 1) * tm
            for r in range(tm):
                start_pair(next_base, 1 - slot, r)
            wait_slot(slot)
            rf = rf_ref[...]
            g1 = rf[:, 0:1]
            g2 = rf[:, 1:2]
            parts = []
            for s in range(ROW_SPLIT):
                y1 = buf_ref[slot, 0, pl.ds(s, tm, stride=ROW_SPLIT), :]
                y2 = buf_ref[slot, 1, pl.ds(s, tm, stride=ROW_SPLIT), :]
                parts.append(g1 * y1 + g2 * y2)
            f = jnp.concatenate(parts, axis=1)
            o_ref[...] = x_ref[...] + _rms(f, g_ref[...])

    @pl.when(i == n_tiles - 1)
    def _():
        wait_slot(n_tiles % 2)


def _combine(pos1, pos2, y_rows, x2, route_f, g):
    t = x2.shape[0]
    tm = TOKEN_TILE
    grid_spec = pltpu.PrefetchScalarGridSpec(
        num_scalar_prefetch=2,
        grid=(t // tm,),
        in_specs=[pl.BlockSpec(memory_space=pl.ANY),
                  pl.BlockSpec((tm, D_MODEL), lambda i, p1, p2: (i, 0)),
                  pl.BlockSpec((tm, route_f.shape[1]), lambda i, p1, p2: (i, 0)),
                  pl.BlockSpec(g.shape, lambda i, p1, p2: (0, 0))],
        out_specs=pl.BlockSpec((tm, D_MODEL), lambda i, p1, p2: (i, 0)),
        scratch_shapes=[pltpu.VMEM((2, 2, tm * ROW_SPLIT, LANES), _F32),
                        pltpu.SemaphoreType.DMA((2,))],
    )
    return pl.pallas_call(
        functools.partial(_combine_kernel, tm=tm, n_tiles=t // tm),
        grid_spec=grid_spec,
        out_shape=jax.ShapeDtypeStruct((t, D_MODEL), _F32),
        compiler_params=_params(("arbitrary",), 40),
        name="combine",
    )(pos1, pos2, y_rows, x2, route_f, g)


def _swap_halves(w):
    half = w.shape[-1] // 2
    return jnp.concatenate([w[..., half:], w[..., :half]], axis=-1)


def _layer_weights(w_in, w_uq, w_ukv, pool_w, w_out):
    kr = w_in[:, _C_KR:_C_KR + ROPE_DIM]
    krs = _swap_halves(kr)
    win = jnp.concatenate([w_in[:, :_C_KR], kr, kr, krs, krs, w_in[:, _C_KR + ROPE_DIM:]], axis=1)
    q_rope = w_uq[:, :, NOPE_DIM:]
    wq = jnp.concatenate([w_uq[:, :, :NOPE_DIM], q_rope, _swap_halves(q_rope)], axis=2)
    wq = wq.reshape(Q_LORA, N_HEADS * HEAD_W)
    wkv = jnp.concatenate([w_ukv[:, :, :NOPE_DIM].reshape(KV_LORA, N_HEADS * NOPE_DIM),
                           w_ukv[:, :, NOPE_DIM:].reshape(KV_LORA, N_HEADS * V_DIM)], axis=1)
    pw = jnp.zeros((POOL_WIDTH, POOL_WIDTH), _F32)
    for gidx in range(len(POOL_WINDOWS)):
        lo = gidx * POOL_GROUP_DIM
        pw = pw.at[lo:lo + POOL_GROUP_DIM, lo:lo + POOL_GROUP_DIM].set(pool_w[gidx])
    wa = w_out[:N_HEADS * V_DIM]
    wp = w_out[N_HEADS * V_DIM:]
    return tuple(a.astype(_BF16) for a in (win, wq, wkv, pw, wa, wp))


def _rope_table(seq):
    inv = 1.0 / (ROPE_THETA ** (jnp.arange(0, ROPE_DIM, 2, dtype=_F32) / ROPE_DIM))
    ang = jnp.arange(seq, dtype=_F32)[:, None] * inv[None, :]
    cos, sin = jnp.cos(ang), jnp.sin(ang)
    cos2 = jnp.concatenate([cos, cos], axis=1)
    sin2 = jnp.concatenate([-sin, sin], axis=1)
    scale = (NOPE_DIM + ROPE_DIM) ** -0.5
    qmul = jnp.concatenate([jnp.full((seq, NOPE_DIM), scale, _F32), scale * cos2, scale * sin2], axis=1)
    return jnp.concatenate([qmul, cos2, cos2, sin2, sin2], axis=1)


def _routing_plan(route_i, counts, t):
    tm = EXPERT_TILE
    n_tiles = (2 * t) // tm + N_EXPERTS
    cnt = counts[:, 0].astype(jnp.int32)
    padded = ((cnt + tm - 1) // tm) * tm
    ends = jnp.cumsum(padded)
    base = ends - padded
    n_valid = (ends[-1] // tm).astype(jnp.int32)
    tile = jnp.arange(n_tiles, dtype=jnp.int32)
    tile_expert = jnp.minimum(jnp.sum((tile * tm)[:, None] >= ends[None, :], axis=1), N_EXPERTS - 1)
    first_tile = jnp.sum(jnp.where(tile_expert[:, None] == jnp.arange(N_EXPERTS)[None, :],
                                   (base // tm)[None, :], 0), axis=1)
    backward = ((tile - first_tile) % 2 == 1)[:, None]
    steps = jnp.arange(EXPERT_FF_STEPS, dtype=jnp.int32)[None, :]
    step_chunk = jnp.where(backward, EXPERT_FF_STEPS - 1 - steps, steps)
    used = (tile < n_valid)[:, None]
    last = n_valid - 1
    tile_expert = jnp.where(used[:, 0], tile_expert, tile_expert[last])
    step_chunk = jnp.where(used, step_chunk, step_chunk[last, EXPERT_FF_STEPS - 1])
    e1, e2, pos1, pos2 = (route_i[k] for k in range(4))
    for k in range(N_EXPERTS):
        pos1 = pos1 + jnp.where(e1 == k, base[k], 0)
        pos2 = pos2 + jnp.where(e2 == k, base[k], 0)
    src = _invert_positions(pos1, pos2, n_tiles * tm)
    return (tile_expert.astype(jnp.int32), step_chunk.reshape(-1).astype(jnp.int32), n_valid.reshape(1),
            src, pos1, pos2)


def _invert_kernel(p1_ref, p2_ref, zeros_hbm, src_ref, *, unroll):
    pltpu.sync_copy(zeros_hbm, src_ref)

    def fill(j, carry):
        for u in range(unroll):
            tok = j * unroll + u
            src_ref[p1_ref[tok]] = tok
            src_ref[p2_ref[tok]] = tok
        return carry

    lax.fori_loop(0, p1_ref.shape[0] // unroll, fill, 0)


def _invert_positions(pos1, pos2, n_slots):
    unroll = 8
    assert n_slots % unroll == 0 and pos1.shape[0] % unroll == 0
    smem = pl.BlockSpec(memory_space=pltpu.SMEM)
    return pl.pallas_call(
        functools.partial(_invert_kernel, unroll=unroll),
        in_specs=[smem, smem, pl.BlockSpec(memory_space=pl.ANY)],
        out_specs=smem,
        out_shape=jax.ShapeDtypeStruct((n_slots,), jnp.int32),
        name="invert_positions",
    )(pos1, pos2, jnp.zeros((n_slots,), jnp.int32))


def kernel(x, attn_pre_g, attn_post_g, ffn_pre_g, ffn_post_g, w_in, q_norm_g, kv_norm_g, w_uq, w_ukv,
           pool_w, pool_scale, w_out, w_gate_d, w_up_d, w_down_d, w_router, w_gate_e, w_up_e, w_down_e):
    b, seq, d = x.shape
    t = b * seq
    depth = w_in.shape[0]
    assert d == D_MODEL and seq % TOKEN_TILE == 0 and seq % ATTN_TILE == 0
    tab = _rope_table(seq)
    row = lambda a: a.reshape(1, -1)
    x2 = x.reshape(t, d)
    for l in range(depth):
        win, wq, wkv, pw, wa, wp = _layer_weights(w_in[l], w_uq[l], w_ukv[l], pool_w[l], w_out[l])
        q, k, v, p = _in_proj(x2, row(attn_pre_g[l]), win, row(q_norm_g[l]), row(kv_norm_g[l]),
                              wq, wkv, tab, pw, row(pool_scale[l]), seq=seq)
        i = l // 2
        if l % 2 == 1:
            casts = ((w_up_e[i].reshape(-1, w_up_e.shape[-1]),) + CAST_HEADS_WIDE,
                     (w_down_e[i].reshape(-1, d),) + CAST_HEADS_REST)
        elif l + 1 < depth:
            casts = ((w_gate_e[(l + 1) // 2].reshape(-1, w_gate_e.shape[-1]),) + CAST_HEADS_WIDE,)
        else:
            casts = ()
        a, converted = _attention(q.reshape(b, seq, -1), k.reshape(b, seq, -1), v.reshape(b, seq, -1), casts)
        a2 = a.reshape(t, -1)
        if l % 2 == 0:
            x2 = _out_proj(a2, p, x2, wa, wp, row(attn_post_g[l]), row(ffn_pre_g[l]),
                           dense=(w_gate_d[i].astype(_BF16), w_up_d[i].astype(_BF16),
                                  w_down_d[i].astype(_BF16), row(ffn_post_g[l])))
            next_gate = converted[0].reshape(w_gate_e.shape[1:]) if converted else None
        else:
            wr_t = w_router[i].T
            wr_hi = wr_t.astype(_BF16)
            wr = jnp.concatenate([wr_hi, (wr_t - wr_hi.astype(_F32)).astype(_BF16)], axis=0)
            x2, h_rows, route_i, route_f, counts = _out_proj(
                a2, p, x2, wa, wp, row(attn_post_g[l]), row(ffn_pre_g[l]), wr)
            tile_expert, step_chunk, n_valid, src, pos1, pos2 = _routing_plan(route_i, counts, t)
            y_rows = _expert_ffn(tile_expert, step_chunk, n_valid, src, h_rows, next_gate,
                                 converted[0].reshape(w_up_e.shape[1:]),
                                 converted[1].reshape(w_down_e.shape[1:]))
            x2 = _combine(pos1, pos2, y_rows, x2, route_f[0:2].T, row(ffn_post_g[l]))
    return x2.reshape(b, seq, d)
```

```python
import functools

import jax
import jax.numpy as jnp
from jax import lax
from jax.experimental import pallas as pl
from jax.experimental.pallas import tpu as pltpu

D_MODEL = 1024
CHUNK = 64
EPS = 1e-6
N_HEADS = 6
NOPE_DIM = 128
ROPE_DIM = 64
V_DIM = 128
Q_LORA = 384
KV_LORA = 256
ROPE_THETA = 10000.0
POOL_WINDOWS = (2, 4, 8, 16)
POOL_GROUP_DIM = 64
POOL_WIDTH = len(POOL_WINDOWS) * POOL_GROUP_DIM
N_EXPERTS = 8
MAX_WINDOW = max(POOL_WINDOWS)

LANES = 128
SUBLANES = 8
HEAD_W = 2 * LANES
ROW_SPLIT = D_MODEL // LANES

_C_Q = 0
_C_KV = Q_LORA
_C_KR = Q_LORA + KV_LORA
_C_KRS = _C_KR + LANES
_C_POOL = _C_KRS + LANES
IN_AUG = _C_POOL + POOL_WIDTH

TOKEN_TILE = 512
IN_SUB_ROWS = 256
ROUTER_SUB_ROWS = 512
ATTN_TILE = 256
EXPERT_TILE = 512
EXPERT_FF_STEPS = 2
EXPERT_FF_SUB = 256
CAST_HEADS_WIDE = (0, 4)
CAST_HEADS_REST = (4, 6)

_F32 = jnp.float32
_BF16 = jnp.bfloat16
_MIB = 1024 * 1024


def _rms(x, g):
    return x * lax.rsqrt(jnp.mean(x * x, axis=-1, keepdims=True) + EPS) * g


def _silu_mul(g, u):
    hg = 0.5 * g
    return (hg + hg * jnp.tanh(hg)) * u


def _params(sem, vmem_mib):
    return pltpu.CompilerParams(dimension_semantics=sem, vmem_limit_bytes=vmem_mib * _MIB)


def _in_kernel(x_ref, g_ref, win_ref, qg_ref, kvg_ref, wq_ref, wkv_ref, tab_ref, pw_ref, ps_ref,
               q_ref, k_ref, v_ref, p_ref, ext_ref, *, tm, tiles_per_seq):
    i = pl.program_id(0)

    @pl.when(i % tiles_per_seq == 0)
    def _():
        ext_ref[0:MAX_WINDOW, :] = jnp.zeros((MAX_WINDOW, POOL_WIDTH), _F32)

    for r0 in range(0, tm, IN_SUB_ROWS):
        rows = slice(r0, r0 + IN_SUB_ROWS)
        h = _rms(x_ref[rows, :], g_ref[...]).astype(_BF16)
        z = jnp.dot(h, win_ref[...], preferred_element_type=_F32)
        cq = _rms(z[:, _C_Q:_C_Q + Q_LORA], qg_ref[...]).astype(_BF16)
        ckv = _rms(z[:, _C_KV:_C_KV + KV_LORA], kvg_ref[...]).astype(_BF16)
        q = jnp.dot(cq, wq_ref[...], preferred_element_type=_F32)
        kv = jnp.dot(ckv, wkv_ref[...], preferred_element_type=_F32)
        tab = tab_ref[rows, :]
        qmul = tab[:, 0:HEAD_W]
        cc = tab[:, HEAD_W:HEAD_W + LANES]
        ss = tab[:, HEAD_W + LANES:HEAD_W + 2 * LANES]
        krr = (z[:, _C_KR:_C_KR + LANES] * cc + z[:, _C_KRS:_C_KRS + LANES] * ss).astype(_BF16)
        for hd in range(N_HEADS):
            q_ref[rows, hd * HEAD_W:(hd + 1) * HEAD_W] = (q[:, hd * HEAD_W:(hd + 1) * HEAD_W] * qmul).astype(_BF16)
            k_ref[rows, hd * HEAD_W:hd * HEAD_W + NOPE_DIM] = kv[:, hd * NOPE_DIM:(hd + 1) * NOPE_DIM].astype(_BF16)
            k_ref[rows, hd * HEAD_W + NOPE_DIM:(hd + 1) * HEAD_W] = krr
        v_ref[rows, :] = kv[:, N_HEADS * NOPE_DIM:].astype(_BF16)
        ext_ref[MAX_WINDOW + r0:MAX_WINDOW + r0 + IN_SUB_ROWS, :] = z[:, _C_POOL:_C_POOL + POOL_WIDTH]

    u = ext_ref[MAX_WINDOW:MAX_WINDOW + tm, :]

    def window_sum(k0, k1, lo):
        acc = ext_ref[MAX_WINDOW - k0:MAX_WINDOW - k0 + tm, lo:lo + LANES]
        for k in range(k0 + 1, k1):
            acc = acc + ext_ref[MAX_WINDOW - k:MAX_WINDOW - k + tm, lo:lo + LANES]
        return acc

    t_pos = (i % tiles_per_seq) * tm + lax.broadcasted_iota(jnp.int32, (tm, LANES), 0)
    cnt = (t_pos + 1).astype(_F32)
    first_group = lax.broadcasted_iota(jnp.int32, (tm, LANES), 1) < POOL_GROUP_DIM
    w0, w1, w2, w3 = POOL_WINDOWS
    s_a = window_sum(0, w0, 0)
    s_b = s_a + window_sum(w0, w1, 0)
    pooled_lo = jnp.where(first_group, s_a / jnp.minimum(cnt, float(w0)), s_b / jnp.minimum(cnt, float(w1)))
    s_c = window_sum(0, w2, LANES)
    s_d = s_c + window_sum(w2, w3, LANES)
    pooled_hi = jnp.where(first_group, s_c / jnp.minimum(cnt, float(w2)), s_d / jnp.minimum(cnt, float(w3)))
    d = jnp.concatenate([pooled_lo - u[:, :LANES], pooled_hi - u[:, LANES:]], axis=1).astype(_BF16)
    y = jnp.dot(d, pw_ref[...], preferred_element_type=_F32) * ps_ref[...]
    p_ref[...] = y.astype(_BF16)
    ext_ref[0:MAX_WINDOW, :] = ext_ref[tm:tm + MAX_WINDOW, :]


def _in_proj(x2, g, win, qg, kvg, wq, wkv, tab, pw, ps, *, seq):
    t = x2.shape[0]
    tm = TOKEN_TILE
    tps = seq // tm
    full = lambda a: pl.BlockSpec(a.shape, lambda i: (0,) * a.ndim)
    return pl.pallas_call(
        functools.partial(_in_kernel, tm=tm, tiles_per_seq=tps),
        grid=(t // tm,),
        in_specs=[pl.BlockSpec((tm, D_MODEL), lambda i: (i, 0)),
                  full(g), full(win), full(qg), full(kvg), full(wq), full(wkv),
                  pl.BlockSpec((tm, tab.shape[1]), lambda i: (i % tps, 0)),
                  full(pw), full(ps)],
        out_specs=[pl.BlockSpec((tm, N_HEADS * HEAD_W), lambda i: (i, 0)),
                   pl.BlockSpec((tm, N_HEADS * HEAD_W), lambda i: (i, 0)),
                   pl.BlockSpec((tm, N_HEADS * V_DIM), lambda i: (i, 0)),
                   pl.BlockSpec((tm, POOL_WIDTH), lambda i: (i, 0))],
        out_shape=[jax.ShapeDtypeStruct((t, N_HEADS * HEAD_W), _BF16),
                   jax.ShapeDtypeStruct((t, N_HEADS * HEAD_W), _BF16),
                   jax.ShapeDtypeStruct((t, N_HEADS * V_DIM), _BF16),
                   jax.ShapeDtypeStruct((t, POOL_WIDTH), _BF16)],
        scratch_shapes=[pltpu.VMEM((tm + MAX_WINDOW, POOL_WIDTH), _F32)],
        compiler_params=_params(("arbitrary",), 48),
        name="in_proj",
    )(x2, g, win, qg, kvg, wq, wkv, tab, pw, ps)


def _attn_kernel(*refs, tq, nq, cast_heads):
    n_cast = len(cast_heads)
    q_ref, k_ref, v_ref = refs[0:3]
    o_ref = refs[3 + n_cast]
    vaug_ref = refs[-1]
    head = pl.program_id(1)
    for (lo, hi), src_ref, dst_ref in zip(cast_heads, refs[3:3 + n_cast], refs[4 + n_cast:4 + 2 * n_cast]):
        @pl.when(jnp.logical_and(head >= lo, head < hi))
        def _():
            dst_ref[...] = src_ref[...].astype(_BF16)

    seq = v_ref.shape[0]
    vaug_ref[:, 0:V_DIM] = v_ref[...]
    ones_col = lax.broadcasted_iota(jnp.int32, (seq, LANES), 1) == 0
    vaug_ref[:, V_DIM:V_DIM + LANES] = jnp.where(ones_col, 1.0, 0.0).astype(_BF16)
    row_chunk = lax.broadcasted_iota(jnp.int32, (tq, tq), 0) // CHUNK
    col_chunk = lax.broadcasted_iota(jnp.int32, (tq, tq), 1) // CHUNK
    visible = col_chunk <= row_chunk
    nt = (((1,), (1,)), ((), ()))
    def scores(i):
        d0 = i * tq
        s = lax.dot_general(q_ref[d0:d0 + tq, :], k_ref[0:d0 + tq, :], nt, preferred_element_type=_F32)
        s_diag = jnp.where(visible, s[:, d0:], -jnp.inf)
        return s_diag if i == 0 else jnp.concatenate([s[:, :d0], s_diag], axis=1)

    s_next = scores(0)
    for i in range(nq):
        s = s_next
        if i + 1 < nq:
            s_next = scores(i + 1)
        d0 = i * tq
        p = jnp.exp(s - jnp.max(s, axis=1, keepdims=True)).astype(_BF16)
        o = jnp.dot(p, vaug_ref[0:d0 + tq, :], preferred_element_type=_F32)
        o_ref[d0:d0 + tq, :] = (o[:, 0:V_DIM] / o[:, V_DIM:V_DIM + 1]).astype(_BF16)


def _attention(q, k, v, casts=()):
    b, seq, _ = q.shape
    tq = ATTN_TILE
    head = lambda w: pl.BlockSpec((None, seq, w), lambda bb, hh: (bb, 0, hh))
    cast_specs, cast_shapes = [], []
    for w, lo, hi in casts:
        n_h = hi - lo
        rows, cols = w.shape
        assert rows % (b * n_h * 2 * SUBLANES) == 0
        block_of = lambda bb, hh, lo=lo, n_h=n_h: (bb * n_h + jnp.clip(hh - lo, 0, n_h - 1), 0)
        cast_specs.append(pl.BlockSpec((rows // (b * n_h), cols), block_of))
        cast_shapes.append(jax.ShapeDtypeStruct(w.shape, _BF16))
    outs = pl.pallas_call(
        functools.partial(_attn_kernel, tq=tq, nq=seq // tq, cast_heads=tuple((lo, hi) for _, lo, hi in casts)),
        grid=(b, N_HEADS),
        in_specs=[head(HEAD_W), head(HEAD_W), head(V_DIM)] + cast_specs,
        out_specs=[head(V_DIM)] + cast_specs,
        out_shape=[jax.ShapeDtypeStruct((b, seq, N_HEADS * V_DIM), _BF16)] + cast_shapes,
        scratch_shapes=[pltpu.VMEM((seq, V_DIM + LANES), _BF16)],
        compiler_params=_params(("arbitrary", "arbitrary"), 56),
        name="attention",
    )(q, k, v, *(w for w, _, _ in casts))
    return outs[0], outs[1:]


def _mix_residual(a_ref, p_ref, x_ref, wa_ref, wp_ref, g1_ref, rows=slice(None)):
    mix = (jnp.dot(a_ref[rows, :], wa_ref[...], preferred_element_type=_F32)
           + jnp.dot(p_ref[rows, :], wp_ref[...], preferred_element_type=_F32))
    return x_ref[rows, :] + _rms(mix, g1_ref[...])


def _out_dense_kernel(a_ref, p_ref, x_ref, wa_ref, wp_ref, g1_ref, g2_ref,
                      wg_ref, wu_ref, wd_ref, g3_ref, o_ref):
    xn = _mix_residual(a_ref, p_ref, x_ref, wa_ref, wp_ref, g1_ref)
    h = _rms(xn, g2_ref[...]).astype(_BF16)
    gate = jnp.dot(h, wg_ref[...], preferred_element_type=_F32)
    up = jnp.dot(h, wu_ref[...], preferred_element_type=_F32)
    hm = _silu_mul(gate, up).astype(_BF16)
    f = jnp.dot(hm, wd_ref[...], preferred_element_type=_F32)
    o_ref[...] = xn + _rms(f, g3_ref[...])


def _out_moe_kernel(a_ref, p_ref, x_ref, wa_ref, wp_ref, g1_ref, g2_ref, wr_ref,
                    xo_ref, ho_ref, ri_ref, rf_ref, cnt_ref, carry_ref, *, tm):
    @pl.when(pl.program_id(0) == 0)
    def _():
        carry_ref[...] = jnp.zeros_like(carry_ref)

    nt = (((1,), (1,)), ((), ()))
    nsub = ROUTER_SUB_ROWS
    sub = lax.broadcasted_iota(jnp.int32, (N_EXPERTS, nsub), 0)
    earlier = (lax.broadcasted_iota(jnp.int32, (nsub, nsub), 0)
               < lax.broadcasted_iota(jnp.int32, (nsub, nsub), 1)).astype(_BF16)
    wr = wr_ref[...]
    carry = carry_ref[:, 0:1]
    for r0 in range(0, tm, nsub):
        rows = slice(r0, r0 + nsub)
        xn = _mix_residual(a_ref, p_ref, x_ref, wa_ref, wp_ref, g1_ref, rows)
        xo_ref[rows, :] = xn
        h = _rms(xn, g2_ref[...])
        for s in range(ROW_SPLIT):
            ho_ref[pl.ds(r0 * ROW_SPLIT + s, nsub, stride=ROW_SPLIT), :] = h[:, s * LANES:(s + 1) * LANES]

        h_hi = h.astype(_BF16)
        h_lo = (h - h_hi.astype(_F32)).astype(_BF16)
        by_hi = lax.dot_general(wr, h_hi, nt, preferred_element_type=_F32)
        by_lo = lax.dot_general(wr, h_lo, nt, preferred_element_type=_F32)
        logits = by_hi[0:N_EXPERTS] + by_hi[N_EXPERTS:] + by_lo[0:N_EXPERTS]
        v1 = jnp.max(logits, axis=0, keepdims=True)
        e1 = jnp.min(jnp.where(logits == v1, sub, N_EXPERTS), axis=0, keepdims=True)
        rest = jnp.where(sub == e1, -jnp.inf, logits)
        v2 = jnp.max(rest, axis=0, keepdims=True)
        e2 = jnp.min(jnp.where(rest == v2, sub, N_EXPERTS), axis=0, keepdims=True)
        ex = jnp.exp(v2 - v1)
        gate1 = 1.0 / (1.0 + ex)
        gate2 = ex * gate1

        sel1 = sub == e1
        sel2 = sub == e2
        member = jnp.logical_or(sel1, sel2).astype(_F32)
        member_lhs = jnp.concatenate([member, jnp.zeros_like(member)], axis=0).astype(_BF16)
        before = jnp.dot(member_lhs, earlier, preferred_element_type=_F32)[0:N_EXPERTS] + carry
        r1 = jnp.sum(jnp.where(sel1, before, 0.0), axis=0, keepdims=True).astype(jnp.int32)
        r2 = jnp.sum(jnp.where(sel2, before, 0.0), axis=0, keepdims=True).astype(jnp.int32)
        carry = carry + jnp.sum(member, axis=1, keepdims=True)

        ri_ref[:, rows] = jnp.where(sub == 0, e1, jnp.where(sub == 1, e2, jnp.where(sub == 2, r1, r2)))
        rf_ref[:, rows] = jnp.where(sub == 0, gate1, gate2)

    carry_ref[...] = jnp.broadcast_to(carry, carry_ref.shape)
    cnt_ref[...] = jnp.broadcast_to(carry, cnt_ref.shape)


def _out_proj(a2, p2, x2, wa, wp, g1, g2, w_router=None, dense=None):
    t = x2.shape[0]
    tm = TOKEN_TILE
    full = lambda a: pl.BlockSpec(a.shape, lambda i: (0,) * a.ndim)
    resident = lambda a: pl.BlockSpec(a.shape, lambda i: (0,) * a.ndim, pipeline_mode=pl.Buffered(1))
    row = lambda w: pl.BlockSpec((tm, w), lambda i: (i, 0))
    in_specs = [row(a2.shape[1]), row(p2.shape[1]), row(D_MODEL), full(wa), full(wp), full(g1), full(g2)]
    if w_router is None:
        wg, wu, wd, g3 = dense
        return pl.pallas_call(
            _out_dense_kernel,
            grid=(t // tm,),
            in_specs=in_specs + [resident(wg), resident(wu), resident(wd), full(g3)],
            out_specs=row(D_MODEL),
            out_shape=jax.ShapeDtypeStruct((t, D_MODEL), _F32),
            compiler_params=_params(("arbitrary",), 56),
            name="out_proj_dense_ffn",
        )(a2, p2, x2, wa, wp, g1, g2, wg, wu, wd, g3)
    return pl.pallas_call(
        functools.partial(_out_moe_kernel, tm=tm),
        grid=(t // tm,),
        in_specs=in_specs + [full(w_router)],
        out_specs=[row(D_MODEL),
                   pl.BlockSpec((tm * ROW_SPLIT, LANES), lambda i: (i, 0)),
                   pl.BlockSpec((N_EXPERTS, tm), lambda i: (0, i)),
                   pl.BlockSpec((N_EXPERTS, tm), lambda i: (0, i)),
                   pl.BlockSpec((N_EXPERTS, LANES), lambda i: (0, 0))],
        out_shape=[jax.ShapeDtypeStruct((t, D_MODEL), _F32),
                   jax.ShapeDtypeStruct((t * ROW_SPLIT, LANES), _F32),
                   jax.ShapeDtypeStruct((N_EXPERTS, t), jnp.int32),
                   jax.ShapeDtypeStruct((N_EXPERTS, t), _F32),
                   jax.ShapeDtypeStruct((N_EXPERTS, LANES), _F32)],
        scratch_shapes=[pltpu.VMEM((N_EXPERTS, LANES), _F32)],
        compiler_params=_params(("arbitrary",), 48),
        name="out_proj_router",
    )(a2, p2, x2, wa, wp, g1, g2, w_router)


def _expert_kernel(te_ref, ch_ref, nv_ref, src_ref, h_hbm, wg_ref, wu_ref, wd_ref, y_ref,
                   buf_ref, xb_ref, acc_ref, sem, *, tm, n_tiles, n_steps):
    i = pl.program_id(0)
    c = pl.program_id(1)
    last_c = n_steps - 1
    nv = nv_ref[0]
    rows = tm * ROW_SPLIT

    def start_row(tile_base, slot, r):
        tok = src_ref[tile_base + r]
        pltpu.make_async_copy(
            h_hbm.at[pl.ds(pl.multiple_of(tok * ROW_SPLIT, ROW_SPLIT), ROW_SPLIT), :],
            buf_ref.at[slot, pl.ds(pl.multiple_of(r * ROW_SPLIT, ROW_SPLIT), ROW_SPLIT), :],
            sem.at[slot]).start()

    def wait_slot(slot):
        pltpu.make_async_copy(h_hbm.at[pl.ds(0, rows), :], buf_ref.at[slot], sem.at[slot]).wait()

    @pl.when(jnp.logical_and(i == 0, c == 0))
    def _():
        def body(r, carry):
            start_row(0, 0, r)
            return carry

        lax.fori_loop(0, tm, body, 0)

    for slot in range(2):
        @pl.when(jnp.logical_and(jnp.logical_and(c == 0, i <= nv), i % 2 == slot))
        def _():
            wait_slot(slot)
            for s in range(ROW_SPLIT):
                xb_ref[:, s * LANES:(s + 1) * LANES] = (
                    buf_ref[slot, pl.ds(s, tm, stride=ROW_SPLIT), :].astype(_BF16))

    def swiglu_part():
        x = xb_ref[...]
        part = None
        for k0 in range(0, wg_ref.shape[1], EXPERT_FF_SUB):
            k1 = k0 + EXPERT_FF_SUB
            gate = jnp.dot(x, wg_ref[:, k0:k1], preferred_element_type=_F32)
            up = jnp.dot(x, wu_ref[:, k0:k1], preferred_element_type=_F32)
            hm = _silu_mul(gate, up).astype(_BF16)
            down = jnp.dot(hm, wd_ref[k0:k1, :], preferred_element_type=_F32)
            part = down if part is None else part + down
        return part

    @pl.when(jnp.logical_and(i < nv, c == 0))
    def _():
        next_base = jnp.minimum(i + 1, n_tiles - 1) * tm
        next_slot = (i + 1) % 2
        for r in range(tm):
            start_row(next_base, next_slot, r)
        acc_ref[...] = swiglu_part()

    if n_steps > 2:
        @pl.when(jnp.logical_and(i < nv, jnp.logical_and(c > 0, c < last_c)))
        def _():
            acc_ref[...] += swiglu_part()

    @pl.when(jnp.logical_and(i < nv, c == last_c))
    def _():
        total = acc_ref[...] + swiglu_part()
        for s in range(ROW_SPLIT):
            y_ref[pl.ds(s, tm, stride=ROW_SPLIT), :] = total[:, s * LANES:(s + 1) * LANES]

    @pl.when(jnp.logical_and(c == last_c, i >= nv))
    def _():
        y_ref[...] = jnp.zeros_like(y_ref)

    @pl.when(jnp.logical_and(jnp.logical_and(i == n_tiles - 1, c == last_c), nv == n_tiles))
    def _():
        wait_slot(n_tiles % 2)


def _expert_ffn(tile_expert, step_chunk, n_valid, src_tok, h_rows, wg, wu, wd):
    tm = EXPERT_TILE
    n_tiles = tile_expert.shape[0]
    steps = EXPERT_FF_STEPS
    assert steps >= 2
    fc = wg.shape[2] // steps

    col_map = lambda i, c, te, ch, nv, src: (te[i], 0, ch[i * steps + c])
    row_map = lambda i, c, te, ch, nv, src: (te[i], ch[i * steps + c], 0)
    grid_spec = pltpu.PrefetchScalarGridSpec(
        num_scalar_prefetch=4,
        grid=(n_tiles, steps),
        in_specs=[pl.BlockSpec(memory_space=pl.ANY),
                  pl.BlockSpec((None, D_MODEL, fc), col_map),
                  pl.BlockSpec((None, D_MODEL, fc), col_map),
                  pl.BlockSpec((None, fc, D_MODEL), row_map)],
        out_specs=pl.BlockSpec((tm * ROW_SPLIT, LANES), lambda i, c, te, ch, nv, src: (i, 0)),
        scratch_shapes=[pltpu.VMEM((2, tm * ROW_SPLIT, LANES), _F32),
                        pltpu.VMEM((tm, D_MODEL), _BF16),
                        pltpu.VMEM((tm, D_MODEL), _F32),
                        pltpu.SemaphoreType.DMA((2,))],
    )
    return pl.pallas_call(
        functools.partial(_expert_kernel, tm=tm, n_tiles=n_tiles, n_steps=steps),
        grid_spec=grid_spec,
        out_shape=jax.ShapeDtypeStruct((n_tiles * tm * ROW_SPLIT, LANES), _F32),
        compiler_params=_params(("arbitrary", "arbitrary"), 56),
        name="expert_ffn",
    )(tile_expert, step_chunk, n_valid, src_tok, h_rows, wg, wu, wd)


def _combine_kernel(p1_ref, p2_ref, y_hbm, x_ref, rf_ref, g_ref, o_ref, buf_ref, sem, *, tm, n_tiles):
    i = pl.program_id(0)
    rows = tm * ROW_SPLIT

    def start_pair(tile_base, slot, r):
        if isinstance(r, int):
            dst = pl.ds(r * ROW_SPLIT, ROW_SPLIT)
        else:
            dst = pl.ds(pl.multiple_of(r * ROW_SPLIT, ROW_SPLIT), ROW_SPLIT)
        for k, pos_ref in enumerate((p1_ref, p2_ref)):
            pos = pos_ref[tile_base + r]
            pltpu.make_async_copy(
                y_hbm.at[pl.ds(pl.multiple_of(pos * ROW_SPLIT, ROW_SPLIT), ROW_SPLIT), :],
                buf_ref.at[slot, k, dst, :], sem.at[slot]).start(priority=k)

    def wait_slot(slot):
        for k in range(2):
            pltpu.make_async_copy(y_hbm.at[pl.ds(0, rows), :], buf_ref.at[slot, k], sem.at[slot]).wait()

    @pl.when(i == 0)
    def _():
        def body(r, carry):
            start_pair(0, 0, r)
            return carry

        lax.fori_loop(0, tm, body, 0)

    for slot in range(2):
        @pl.when(i % 2 == slot)
        def _():
            next_base = jnp.minimum(i + 1, n_tiles - 1) * tm
            for r in range(tm):
                start_pair(next_base, 1 - slot, r)
            wait_slot(slot)
            rf = rf_ref[...]
            g1 = rf[:, 0:1]
            g2 = rf[:, 1:2]
            parts = []
            for s in range(ROW_SPLIT):
                y1 = buf_ref[slot, 0, pl.ds(s, tm, stride=ROW_SPLIT), :]
                y2 = buf_ref[slot, 1, pl.ds(s, tm, stride=ROW_SPLIT), :]
                parts.append(g1 * y1 + g2 * y2)
            f = jnp.concatenate(parts, axis=1)
            o_ref[...] = x_ref[...] + _rms(f, g_ref[...])

    @pl.when(i == n_tiles - 1)
    def _():
        wait_slot(n_tiles % 2)


def _combine(pos1, pos2, y_rows, x2, route_f, g):
    t = x2.shape[0]
    tm = TOKEN_TILE
    grid_spec = pltpu.PrefetchScalarGridSpec(
        num_scalar_prefetch=2,
        grid=(t // tm,),
        in_specs=[pl.BlockSpec(memory_space=pl.ANY),
                  pl.BlockSpec((tm, D_MODEL), lambda i, p1, p2: (i, 0)),
                  pl.BlockSpec((tm, route_f.shape[1]), lambda i, p1, p2: (i, 0)),
                  pl.BlockSpec(g.shape, lambda i, p1, p2: (0, 0))],
        out_specs=pl.BlockSpec((tm, D_MODEL), lambda i, p1, p2: (i, 0)),
        scratch_shapes=[pltpu.VMEM((2, 2, tm * ROW_SPLIT, LANES), _F32),
                        pltpu.SemaphoreType.DMA((2,))],
    )
    return pl.pallas_call(
        functools.partial(_combine_kernel, tm=tm, n_tiles=t // tm),
        grid_spec=grid_spec,
        out_shape=jax.ShapeDtypeStruct((t, D_MODEL), _F32),
        compiler_params=_params(("arbitrary",), 40),
        name="combine",
    )(pos1, pos2, y_rows, x2, route_f, g)


def _swap_halves(w):
    half = w.shape[-1] // 2
    return jnp.concatenate([w[..., half:], w[..., :half]], axis=-1)


def _layer_weights(w_in, w_uq, w_ukv, pool_w, w_out):
    kr = w_in[:, _C_KR:_C_KR + ROPE_DIM]
    krs = _swap_halves(kr)
    win = jnp.concatenate([w_in[:, :_C_KR], kr, kr, krs, krs, w_in[:, _C_KR + ROPE_DIM:]], axis=1)
    q_rope = w_uq[:, :, NOPE_DIM:]
    wq = jnp.concatenate([w_uq[:, :, :NOPE_DIM], q_rope, _swap_halves(q_rope)], axis=2)
    wq = wq.reshape(Q_LORA, N_HEADS * HEAD_W)
    wkv = jnp.concatenate([w_ukv[:, :, :NOPE_DIM].reshape(KV_LORA, N_HEADS * NOPE_DIM),
                           w_ukv[:, :, NOPE_DIM:].reshape(KV_LORA, N_HEADS * V_DIM)], axis=1)
    pw = jnp.zeros((POOL_WIDTH, POOL_WIDTH), _F32)
    for gidx in range(len(POOL_WINDOWS)):
        lo = gidx * POOL_GROUP_DIM
        pw = pw.at[lo:lo + POOL_GROUP_DIM, lo:lo + POOL_GROUP_DIM].set(pool_w[gidx])
    wa = w_out[:N_HEADS * V_DIM]
    wp = w_out[N_HEADS * V_DIM:]
    return tuple(a.astype(_BF16) for a in (win, wq, wkv, pw, wa, wp))


def _rope_table(seq):
    inv = 1.0 / (ROPE_THETA ** (jnp.arange(0, ROPE_DIM, 2, dtype=_F32) / ROPE_DIM))
    ang = jnp.arange(seq, dtype=_F32)[:, None] * inv[None, :]
    cos, sin = jnp.cos(ang), jnp.sin(ang)
    cos2 = jnp.concatenate([cos, cos], axis=1)
    sin2 = jnp.concatenate([-sin, sin], axis=1)
    scale = (NOPE_DIM + ROPE_DIM) ** -0.5
    qmul = jnp.concatenate([jnp.full((seq, NOPE_DIM), scale, _F32), scale * cos2, scale * sin2], axis=1)
    return jnp.concatenate([qmul, cos2, cos2, sin2, sin2], axis=1)


def _routing_plan(route_i, counts, t):
    tm = EXPERT_TILE
    n_tiles = (2 * t) // tm + N_EXPERTS
    cnt = counts[:, 0].astype(jnp.int32)
    padded = ((cnt + tm - 1) // tm) * tm
    ends = jnp.cumsum(padded)
    base = ends - padded
    n_valid = (ends[-1] // tm).astype(jnp.int32)
    tile = jnp.arange(n_tiles, dtype=jnp.int32)
    tile_expert = jnp.minimum(jnp.sum((tile * tm)[:, None] >= ends[None, :], axis=1), N_EXPERTS - 1)
    first_tile = jnp.sum(jnp.where(tile_expert[:, None] == jnp.arange(N_EXPERTS)[None, :],
                                   (base // tm)[None, :], 0), axis=1)
    backward = ((tile - first_tile) % 2 == 1)[:, None]
    steps = jnp.arange(EXPERT_FF_STEPS, dtype=jnp.int32)[None, :]
    step_chunk = jnp.where(backward, EXPERT_FF_STEPS - 1 - steps, steps)
    used = (tile < n_valid)[:, None]
    last = n_valid - 1
    tile_expert = jnp.where(used[:, 0], tile_expert, tile_expert[last])
    step_chunk = jnp.where(used, step_chunk, step_chunk[last, EXPERT_FF_STEPS - 1])
    e1, e2, pos1, pos2 = (route_i[k] for k in range(4))
    for k in range(N_EXPERTS):
        pos1 = pos1 + jnp.where(e1 == k, base[k], 0)
        pos2 = pos2 + jnp.where(e2 == k, base[k], 0)
    src = _invert_positions(pos1, pos2, n_tiles * tm)
    return (tile_expert.astype(jnp.int32), step_chunk.reshape(-1).astype(jnp.int32), n_valid.reshape(1),
            src, pos1, pos2)


def _invert_kernel(p1_ref, p2_ref, zeros_hbm, src_ref, *, unroll):
    pltpu.sync_copy(zeros_hbm, src_ref)

    def fill(j, carry):
        for u in range(unroll):
            tok = j * unroll + u
            src_ref[p1_ref[tok]] = tok
            src_ref[p2_ref[tok]] = tok
        return carry

    lax.fori_loop(0, p1_ref.shape[0] // unroll, fill, 0)


def _invert_positions(pos1, pos2, n_slots):
    unroll = 8
    assert n_slots % unroll == 0 and pos1.shape[0] % unroll == 0
    smem = pl.BlockSpec(memory_space=pltpu.SMEM)
    return pl.pallas_call(
        functools.partial(_invert_kernel, unroll=unroll),
        in_specs=[smem, smem, pl.BlockSpec(memory_space=pl.ANY)],
        out_specs=smem,
        out_shape=jax.ShapeDtypeStruct((n_slots,), jnp.int32),
        name="invert_positions",
    )(pos1, pos2, jnp.zeros((n_slots,), jnp.int32))


def kernel(x, attn_pre_g, attn_post_g, ffn_pre_g, ffn_post_g, w_in, q_norm_g, kv_norm_g, w_uq, w_ukv,
           pool_w, pool_scale, w_out, w_gate_d, w_up_d, w_down_d, w_router, w_gate_e, w_up_e, w_down_e):
    b, seq, d = x.shape
    t = b * seq
    depth = w_in.shape[0]
    assert d == D_MODEL and seq % TOKEN_TILE == 0 and seq % ATTN_TILE == 0
    tab = _rope_table(seq)
    row = lambda a: a.reshape(1, -1)
    x2 = x.reshape(t, d)
    for l in range(depth):
        win, wq, wkv, pw, wa, wp = _layer_weights(w_in[l], w_uq[l], w_ukv[l], pool_w[l], w_out[l])
        q, k, v, p = _in_proj(x2, row(attn_pre_g[l]), win, row(q_norm_g[l]), row(kv_norm_g[l]),
                              wq, wkv, tab, pw, row(pool_scale[l]), seq=seq)
        i = l // 2
        if l % 2 == 1:
            casts = ((w_up_e[i].reshape(-1, w_up_e.shape[-1]),) + CAST_HEADS_WIDE,
                     (w_down_e[i].reshape(-1, d),) + CAST_HEADS_REST)
        elif l + 1 < depth:
            casts = ((w_gate_e[(l + 1) // 2].reshape(-1, w_gate_e.shape[-1]),) + CAST_HEADS_WIDE,)
        else:
            casts = ()
        a, converted = _attention(q.reshape(b, seq, -1), k.reshape(b, seq, -1), v.reshape(b, seq, -1), casts)
        a2 = a.reshape(t, -1)
        if l % 2 == 0:
            x2 = _out_proj(a2, p, x2, wa, wp, row(attn_post_g[l]), row(ffn_pre_g[l]),
                           dense=(w_gate_d[i].astype(_BF16), w_up_d[i].astype(_BF16),
                                  w_down_d[i].astype(_BF16), row(ffn_post_g[l])))
            next_gate = converted[0].reshape(w_gate_e.shape[1:]) if converted else None
        else:
            wr_t = w_router[i].T
            wr_hi = wr_t.astype(_BF16)
            wr = jnp.concatenate([wr_hi, (wr_t - wr_hi.astype(_F32)).astype(_BF16)], axis=0)
            x2, h_rows, route_i, route_f, counts = _out_proj(
                a2, p, x2, wa, wp, row(attn_post_g[l]), row(ffn_pre_g[l]), wr)
            tile_expert, step_chunk, n_valid, src, pos1, pos2 = _routing_plan(route_i, counts, t)
            y_rows = _expert_ffn(tile_expert, step_chunk, n_valid, src, h_rows, next_gate,
                                 converted[0].reshape(w_up_e.shape[1:]),
                                 converted[1].reshape(w_down_e.shape[1:]))
            x2 = _combine(pos1, pos2, y_rows, x2, route_f[0:2].T, row(ffn_post_g[l]))
    return x2.reshape(b, seq, d)
```

```python
import functools

import jax
import jax.numpy as jnp
from jax import lax
from jax.experimental import pallas as pl
from jax.experimental.pallas import tpu as pltpu

D_MODEL = 1024
CHUNK = 64
EPS = 1e-6
N_HEADS = 6
NOPE_DIM = 128
ROPE_DIM = 64
V_DIM = 128
Q_LORA = 384
KV_LORA = 256
ROPE_THETA = 10000.0
POOL_WINDOWS = (2, 4, 8, 16)
POOL_GROUP_DIM = 64
POOL_WIDTH = len(POOL_WINDOWS) * POOL_GROUP_DIM
N_EXPERTS = 8
MAX_WINDOW = max(POOL_WINDOWS)

LANES = 128
SUBLANES = 8
HEAD_W = 2 * LANES
ROW_SPLIT = D_MODEL // LANES

_C_Q = 0
_C_KV = Q_LORA
_C_KR = Q_LORA + KV_LORA
_C_KRS = _C_KR + LANES
_C_POOL = _C_KRS + LANES
IN_AUG = _C_POOL + POOL_WIDTH

TOKEN_TILE = 512
IN_SUB_ROWS = 256
ATTN_TILE = 256
EXPERT_TILE = 512
EXPERT_FF_STEPS = 2
EXPERT_FF_SUB = 256
CAST_HEADS_WIDE = (0, 4)
CAST_HEADS_REST = (4, 6)

_F32 = jnp.float32
_BF16 = jnp.bfloat16
_MIB = 1024 * 1024


def _rms(x, g):
    return x * lax.rsqrt(jnp.mean(x * x, axis=-1, keepdims=True) + EPS) * g


def _silu_mul(g, u):
    hg = 0.5 * g
    return (hg + hg * jnp.tanh(hg)) * u


def _params(sem, vmem_mib):
    return pltpu.CompilerParams(dimension_semantics=sem, vmem_limit_bytes=vmem_mib * _MIB)


def _in_kernel(x_ref, g_ref, win_ref, qg_ref, kvg_ref, wq_ref, wkv_ref, tab_ref, pw_ref, ps_ref,
               q_ref, k_ref, v_ref, p_ref, ext_ref, *, tm, tiles_per_seq):
    i = pl.program_id(0)

    @pl.when(i % tiles_per_seq == 0)
    def _():
        ext_ref[0:MAX_WINDOW, :] = jnp.zeros((MAX_WINDOW, POOL_WIDTH), _F32)

    for r0 in range(0, tm, IN_SUB_ROWS):
        rows = slice(r0, r0 + IN_SUB_ROWS)
        h = _rms(x_ref[rows, :], g_ref[...]).astype(_BF16)
        z = jnp.dot(h, win_ref[...], preferred_element_type=_F32)
        cq = _rms(z[:, _C_Q:_C_Q + Q_LORA], qg_ref[...]).astype(_BF16)
        ckv = _rms(z[:, _C_KV:_C_KV + KV_LORA], kvg_ref[...]).astype(_BF16)
        q = jnp.dot(cq, wq_ref[...], preferred_element_type=_F32)
        kv = jnp.dot(ckv, wkv_ref[...], preferred_element_type=_F32)
        tab = tab_ref[rows, :]
        qmul = tab[:, 0:HEAD_W]
        cc = tab[:, HEAD_W:HEAD_W + LANES]
        ss = tab[:, HEAD_W + LANES:HEAD_W + 2 * LANES]
        krr = (z[:, _C_KR:_C_KR + LANES] * cc + z[:, _C_KRS:_C_KRS + LANES] * ss).astype(_BF16)
        for hd in range(N_HEADS):
            q_ref[rows, hd * HEAD_W:(hd + 1) * HEAD_W] = (q[:, hd * HEAD_W:(hd + 1) * HEAD_W] * qmul).astype(_BF16)
            k_ref[rows, hd * HEAD_W:hd * HEAD_W + NOPE_DIM] = kv[:, hd * NOPE_DIM:(hd + 1) * NOPE_DIM].astype(_BF16)
            k_ref[rows, hd * HEAD_W + NOPE_DIM:(hd + 1) * HEAD_W] = krr
        v_ref[rows, :] = kv[:, N_HEADS * NOPE_DIM:].astype(_BF16)
        ext_ref[MAX_WINDOW + r0:MAX_WINDOW + r0 + IN_SUB_ROWS, :] = z[:, _C_POOL:_C_POOL + POOL_WIDTH]

    u = ext_ref[MAX_WINDOW:MAX_WINDOW + tm, :]

    def window_sum(k0, k1, lo):
        acc = ext_ref[MAX_WINDOW - k0:MAX_WINDOW - k0 + tm, lo:lo + LANES]
        for k in range(k0 + 1, k1):
            acc = acc + ext_ref[MAX_WINDOW - k:MAX_WINDOW - k + tm, lo:lo + LANES]
        return acc

    t_pos = (i % tiles_per_seq) * tm + lax.broadcasted_iota(jnp.int32, (tm, LANES), 0)
    cnt = (t_pos + 1).astype(_F32)
    first_group = lax.broadcasted_iota(jnp.int32, (tm, LANES), 1) < POOL_GROUP_DIM
    w0, w1, w2, w3 = POOL_WINDOWS
    s_a = window_sum(0, w0, 0)
    s_b = s_a + window_sum(w0, w1, 0)
    pooled_lo = jnp.where(first_group, s_a / jnp.minimum(cnt, float(w0)), s_b / jnp.minimum(cnt, float(w1)))
    s_c = window_sum(0, w2, LANES)
    s_d = s_c + window_sum(w2, w3, LANES)
    pooled_hi = jnp.where(first_group, s_c / jnp.minimum(cnt, float(w2)), s_d / jnp.minimum(cnt, float(w3)))
    d = jnp.concatenate([pooled_lo - u[:, :LANES], pooled_hi - u[:, LANES:]], axis=1).astype(_BF16)
    y = jnp.dot(d, pw_ref[...], preferred_element_type=_F32) * ps_ref[...]
    p_ref[...] = y.astype(_BF16)
    ext_ref[0:MAX_WINDOW, :] = ext_ref[tm:tm + MAX_WINDOW, :]


def _in_proj(x2, g, win, qg, kvg, wq, wkv, tab, pw, ps, *, seq):
    t = x2.shape[0]
    tm = TOKEN_TILE
    tps = seq // tm
    full = lambda a: pl.BlockSpec(a.shape, lambda i: (0,) * a.ndim)
    return pl.pallas_call(
        functools.partial(_in_kernel, tm=tm, tiles_per_seq=tps),
        grid=(t // tm,),
        in_specs=[pl.BlockSpec((tm, D_MODEL), lambda i: (i, 0)),
                  full(g), full(win), full(qg), full(kvg), full(wq), full(wkv),
                  pl.BlockSpec((tm, tab.shape[1]), lambda i: (i % tps, 0)),
                  full(pw), full(ps)],
        out_specs=[pl.BlockSpec((tm, N_HEADS * HEAD_W), lambda i: (i, 0)),
                   pl.BlockSpec((tm, N_HEADS * HEAD_W), lambda i: (i, 0)),
                   pl.BlockSpec((tm, N_HEADS * V_DIM), lambda i: (i, 0)),
                   pl.BlockSpec((tm, POOL_WIDTH), lambda i: (i, 0))],
        out_shape=[jax.ShapeDtypeStruct((t, N_HEADS * HEAD_W), _BF16),
                   jax.ShapeDtypeStruct((t, N_HEADS * HEAD_W), _BF16),
                   jax.ShapeDtypeStruct((t, N_HEADS * V_DIM), _BF16),
                   jax.ShapeDtypeStruct((t, POOL_WIDTH), _BF16)],
        scratch_shapes=[pltpu.VMEM((tm + MAX_WINDOW, POOL_WIDTH), _F32)],
        compiler_params=_params(("arbitrary",), 48),
        name="in_proj",
    )(x2, g, win, qg, kvg, wq, wkv, tab, pw, ps)


def _attn_kernel(*refs, tq, nq, cast_heads):
    n_cast = len(cast_heads)
    q_ref, k_ref, v_ref = refs[0:3]
    o_ref = refs[3 + n_cast]
    vaug_ref = refs[-1]
    head = pl.program_id(1)
    for (lo, hi), src_ref, dst_ref in zip(cast_heads, refs[3:3 + n_cast], refs[4 + n_cast:4 + 2 * n_cast]):
        @pl.when(jnp.logical_and(head >= lo, head < hi))
        def _():
            dst_ref[...] = src_ref[...].astype(_BF16)

    seq = v_ref.shape[0]
    vaug_ref[:, 0:V_DIM] = v_ref[...]
    ones_col = lax.broadcasted_iota(jnp.int32, (seq, LANES), 1) == 0
    vaug_ref[:, V_DIM:V_DIM + LANES] = jnp.where(ones_col, 1.0, 0.0).astype(_BF16)
    row_chunk = lax.broadcasted_iota(jnp.int32, (tq, tq), 0) // CHUNK
    col_chunk = lax.broadcasted_iota(jnp.int32, (tq, tq), 1) // CHUNK
    visible = col_chunk <= row_chunk
    nt = (((1,), (1,)), ((), ()))
    def scores(i):
        d0 = i * tq
        s = lax.dot_general(q_ref[d0:d0 + tq, :], k_ref[0:d0 + tq, :], nt, preferred_element_type=_F32)
        s_diag = jnp.where(visible, s[:, d0:], -jnp.inf)
        return s_diag if i == 0 else jnp.concatenate([s[:, :d0], s_diag], axis=1)

    order = list(range(nq - 1, -1, -1))
    s_next = scores(order[0])
    for n, i in enumerate(order):
        s = s_next
        d0 = i * tq
        p = jnp.exp(s - jnp.max(s, axis=1, keepdims=True)).astype(_BF16)
        if n + 1 < nq:
            s_next = scores(order[n + 1])
        o = jnp.dot(p, vaug_ref[0:d0 + tq, :], preferred_element_type=_F32)
        o_ref[d0:d0 + tq, :] = (o[:, 0:V_DIM] / o[:, V_DIM:V_DIM + 1]).astype(_BF16)


def _attention(q, k, v, casts=()):
    b, seq, _ = q.shape
    tq = ATTN_TILE
    head = lambda w: pl.BlockSpec((None, seq, w), lambda bb, hh: (bb, 0, hh))
    cast_specs, cast_shapes = [], []
    for w, lo, hi in casts:
        n_h = hi - lo
        rows, cols = w.shape
        assert rows % (b * n_h * 2 * SUBLANES) == 0
        block_of = lambda bb, hh, lo=lo, n_h=n_h: (bb * n_h + jnp.clip(hh - lo, 0, n_h - 1), 0)
        cast_specs.append(pl.BlockSpec((rows // (b * n_h), cols), block_of))
        cast_shapes.append(jax.ShapeDtypeStruct(w.shape, _BF16))
    outs = pl.pallas_call(
        functools.partial(_attn_kernel, tq=tq, nq=seq // tq, cast_heads=tuple((lo, hi) for _, lo, hi in casts)),
        grid=(b, N_HEADS),
        in_specs=[head(HEAD_W), head(HEAD_W), head(V_DIM)] + cast_specs,
        out_specs=[head(V_DIM)] + cast_specs,
        out_shape=[jax.ShapeDtypeStruct((b, seq, N_HEADS * V_DIM), _BF16)] + cast_shapes,
        scratch_shapes=[pltpu.VMEM((seq, V_DIM + LANES), _BF16)],
        compiler_params=_params(("arbitrary", "arbitrary"), 56),
        name="attention",
    )(q, k, v, *(w for w, _, _ in casts))
    return outs[0], outs[1:]


def _mix_residual(a_ref, p_ref, x_ref, wa_ref, wp_ref, g1_ref):
    mix = (jnp.dot(a_ref[...], wa_ref[...], preferred_element_type=_F32)
           + jnp.dot(p_ref[...], wp_ref[...], preferred_element_type=_F32))
    return x_ref[...] + _rms(mix, g1_ref[...])


def _out_dense_kernel(a_ref, p_ref, x_ref, wa_ref, wp_ref, g1_ref, g2_ref,
                      wg_ref, wu_ref, wd_ref, g3_ref, o_ref):
    xn = _mix_residual(a_ref, p_ref, x_ref, wa_ref, wp_ref, g1_ref)
    h = _rms(xn, g2_ref[...]).astype(_BF16)
    gate = jnp.dot(h, wg_ref[...], preferred_element_type=_F32)
    up = jnp.dot(h, wu_ref[...], preferred_element_type=_F32)
    hm = _silu_mul(gate, up).astype(_BF16)
    f = jnp.dot(hm, wd_ref[...], preferred_element_type=_F32)
    o_ref[...] = xn + _rms(f, g3_ref[...])


def _out_moe_kernel(a_ref, p_ref, x_ref, wa_ref, wp_ref, g1_ref, g2_ref, wr_ref,
                    xo_ref, ho_ref, ri_ref, rf_ref, cnt_ref, carry_ref, *, tm):
    @pl.when(pl.program_id(0) == 0)
    def _():
        carry_ref[...] = jnp.zeros_like(carry_ref)

    xn = _mix_residual(a_ref, p_ref, x_ref, wa_ref, wp_ref, g1_ref)
    xo_ref[...] = xn
    h = _rms(xn, g2_ref[...])
    for s in range(ROW_SPLIT):
        ho_ref[pl.ds(s, tm, stride=ROW_SPLIT), :] = h[:, s * LANES:(s + 1) * LANES]

    h_hi = h.astype(_BF16)
    h_lo = (h - h_hi.astype(_F32)).astype(_BF16)
    wr = wr_ref[...]
    nt = (((1,), (1,)), ((), ()))
    by_hi = lax.dot_general(wr, h_hi, nt, preferred_element_type=_F32)
    by_lo = lax.dot_general(wr, h_lo, nt, preferred_element_type=_F32)
    logits = by_hi[0:N_EXPERTS] + by_hi[N_EXPERTS:] + by_lo[0:N_EXPERTS]
    sub = lax.broadcasted_iota(jnp.int32, (N_EXPERTS, tm), 0)
    v1 = jnp.max(logits, axis=0, keepdims=True)
    e1 = jnp.min(jnp.where(logits == v1, sub, N_EXPERTS), axis=0, keepdims=True)
    rest = jnp.where(sub == e1, -jnp.inf, logits)
    v2 = jnp.max(rest, axis=0, keepdims=True)
    e2 = jnp.min(jnp.where(rest == v2, sub, N_EXPERTS), axis=0, keepdims=True)
    ex = jnp.exp(v2 - v1)
    gate1 = 1.0 / (1.0 + ex)
    gate2 = ex * gate1

    sel1 = sub == e1
    sel2 = sub == e2
    member = jnp.logical_or(sel1, sel2).astype(_F32)
    earlier = (lax.broadcasted_iota(jnp.int32, (tm, tm), 0)
               < lax.broadcasted_iota(jnp.int32, (tm, tm), 1)).astype(_BF16)
    member_lhs = jnp.concatenate([member, jnp.zeros_like(member)], axis=0).astype(_BF16)
    carry = carry_ref[:, 0:1]
    before = jnp.dot(member_lhs, earlier, preferred_element_type=_F32)[0:N_EXPERTS] + carry
    r1 = jnp.sum(jnp.where(sel1, before, 0.0), axis=0, keepdims=True).astype(jnp.int32)
    r2 = jnp.sum(jnp.where(sel2, before, 0.0), axis=0, keepdims=True).astype(jnp.int32)
    total = carry + jnp.sum(member, axis=1, keepdims=True)
    carry_ref[...] = jnp.broadcast_to(total, carry_ref.shape)
    cnt_ref[...] = jnp.broadcast_to(total, cnt_ref.shape)

    ri_ref[...] = jnp.where(sub == 0, e1, jnp.where(sub == 1, e2, jnp.where(sub == 2, r1, r2)))
    rf_ref[...] = jnp.where(sub == 0, gate1, gate2)


def _out_proj(a2, p2, x2, wa, wp, g1, g2, w_router=None, dense=None):
    t = x2.shape[0]
    tm = TOKEN_TILE
    full = lambda a: pl.BlockSpec(a.shape, lambda i: (0,) * a.ndim)
    resident = lambda a: pl.BlockSpec(a.shape, lambda i: (0,) * a.ndim, pipeline_mode=pl.Buffered(1))
    row = lambda w: pl.BlockSpec((tm, w), lambda i: (i, 0))
    in_specs = [row(a2.shape[1]), row(p2.shape[1]), row(D_MODEL), full(wa), full(wp), full(g1), full(g2)]
    if w_router is None:
        wg, wu, wd, g3 = dense
        return pl.pallas_call(
            _out_dense_kernel,
            grid=(t // tm,),
            in_specs=in_specs + [resident(wg), resident(wu), resident(wd), full(g3)],
            out_specs=row(D_MODEL),
            out_shape=jax.ShapeDtypeStruct((t, D_MODEL), _F32),
            compiler_params=_params(("arbitrary",), 56),
            name="out_proj_dense_ffn",
        )(a2, p2, x2, wa, wp, g1, g2, wg, wu, wd, g3)
    return pl.pallas_call(
        functools.partial(_out_moe_kernel, tm=tm),
        grid=(t // tm,),
        in_specs=in_specs + [full(w_router)],
        out_specs=[row(D_MODEL),
                   pl.BlockSpec((tm * ROW_SPLIT, LANES), lambda i: (i, 0)),
                   pl.BlockSpec((N_EXPERTS, tm), lambda i: (0, i)),
                   pl.BlockSpec((N_EXPERTS, tm), lambda i: (0, i)),
                   pl.BlockSpec((N_EXPERTS, LANES), lambda i: (0, 0))],
        out_shape=[jax.ShapeDtypeStruct((t, D_MODEL), _F32),
                   jax.ShapeDtypeStruct((t * ROW_SPLIT, LANES), _F32),
                   jax.ShapeDtypeStruct((N_EXPERTS, t), jnp.int32),
                   jax.ShapeDtypeStruct((N_EXPERTS, t), _F32),
                   jax.ShapeDtypeStruct((N_EXPERTS, LANES), _F32)],
        scratch_shapes=[pltpu.VMEM((N_EXPERTS, LANES), _F32)],
        compiler_params=_params(("arbitrary",), 48),
        name="out_proj_router",
    )(a2, p2, x2, wa, wp, g1, g2, w_router)


def _expert_kernel(te_ref, ch_ref, nv_ref, src_ref, h_hbm, wg_ref, wu_ref, wd_ref, y_ref,
                   buf_ref, xb_ref, acc_ref, sem, *, tm, n_tiles, n_steps):
    i = pl.program_id(0)
    c = pl.program_id(1)
    last_c = n_steps - 1
    nv = nv_ref[0]
    rows = tm * ROW_SPLIT

    def start_row(tile_base, slot, r):
        tok = src_ref[tile_base + r]
        pltpu.make_async_copy(
            h_hbm.at[pl.ds(pl.multiple_of(tok * ROW_SPLIT, ROW_SPLIT), ROW_SPLIT), :],
            buf_ref.at[slot, pl.ds(pl.multiple_of(r * ROW_SPLIT, ROW_SPLIT), ROW_SPLIT), :],
            sem.at[slot]).start()

    def wait_slot(slot):
        pltpu.make_async_copy(h_hbm.at[pl.ds(0, rows), :], buf_ref.at[slot], sem.at[slot]).wait()

    @pl.when(jnp.logical_and(i == 0, c == 0))
    def _():
        def body(r, carry):
            start_row(0, 0, r)
            return carry

        lax.fori_loop(0, tm, body, 0)

    for slot in range(2):
        @pl.when(jnp.logical_and(jnp.logical_and(c == 0, i <= nv), i % 2 == slot))
        def _():
            wait_slot(slot)
            for s in range(ROW_SPLIT):
                xb_ref[:, s * LANES:(s + 1) * LANES] = (
                    buf_ref[slot, pl.ds(s, tm, stride=ROW_SPLIT), :].astype(_BF16))

    def swiglu_part():
        x = xb_ref[...]
        part = None
        for k0 in range(0, wg_ref.shape[1], EXPERT_FF_SUB):
            k1 = k0 + EXPERT_FF_SUB
            gate = jnp.dot(x, wg_ref[:, k0:k1], preferred_element_type=_F32)
            up = jnp.dot(x, wu_ref[:, k0:k1], preferred_element_type=_F32)
            hm = _silu_mul(gate, up).astype(_BF16)
            down = jnp.dot(hm, wd_ref[k0:k1, :], preferred_element_type=_F32)
            part = down if part is None else part + down
        return part

    @pl.when(jnp.logical_and(i < nv, c == 0))
    def _():
        next_base = jnp.minimum(i + 1, n_tiles - 1) * tm
        next_slot = (i + 1) % 2
        for r in range(tm):
            start_row(next_base, next_slot, r)
        acc_ref[...] = swiglu_part()

    if n_steps > 2:
        @pl.when(jnp.logical_and(i < nv, jnp.logical_and(c > 0, c < last_c)))
        def _():
            acc_ref[...] += swiglu_part()

    @pl.when(jnp.logical_and(i < nv, c == last_c))
    def _():
        total = acc_ref[...] + swiglu_part()
        for s in range(ROW_SPLIT):
            y_ref[pl.ds(s, tm, stride=ROW_SPLIT), :] = total[:, s * LANES:(s + 1) * LANES]

    @pl.when(jnp.logical_and(c == last_c, i >= nv))
    def _():
        y_ref[...] = jnp.zeros_like(y_ref)

    @pl.when(jnp.logical_and(jnp.logical_and(i == n_tiles - 1, c == last_c), nv == n_tiles))
    def _():
        wait_slot(n_tiles % 2)


def _expert_ffn(tile_expert, step_chunk, n_valid, src_tok, h_rows, wg, wu, wd):
    tm = EXPERT_TILE
    n_tiles = tile_expert.shape[0]
    steps = EXPERT_FF_STEPS
    assert steps >= 2
    fc = wg.shape[2] // steps

    col_map = lambda i, c, te, ch, nv, src: (te[i], 0, ch[i * steps + c])
    row_map = lambda i, c, te, ch, nv, src: (te[i], ch[i * steps + c], 0)
    grid_spec = pltpu.PrefetchScalarGridSpec(
        num_scalar_prefetch=4,
        grid=(n_tiles, steps),
        in_specs=[pl.BlockSpec(memory_space=pl.ANY),
                  pl.BlockSpec((None, D_MODEL, fc), col_map),
                  pl.BlockSpec((None, D_MODEL, fc), col_map),
                  pl.BlockSpec((None, fc, D_MODEL), row_map)],
        out_specs=pl.BlockSpec((tm * ROW_SPLIT, LANES), lambda i, c, te, ch, nv, src: (i, 0)),
        scratch_shapes=[pltpu.VMEM((2, tm * ROW_SPLIT, LANES), _F32),
                        pltpu.VMEM((tm, D_MODEL), _BF16),
                        pltpu.VMEM((tm, D_MODEL), _F32),
                        pltpu.SemaphoreType.DMA((2,))],
    )
    return pl.pallas_call(
        functools.partial(_expert_kernel, tm=tm, n_tiles=n_tiles, n_steps=steps),
        grid_spec=grid_spec,
        out_shape=jax.ShapeDtypeStruct((n_tiles * tm * ROW_SPLIT, LANES), _F32),
        compiler_params=_params(("arbitrary", "arbitrary"), 56),
        name="expert_ffn",
    )(tile_expert, step_chunk, n_valid, src_tok, h_rows, wg, wu, wd)


def _combine_kernel(p1_ref, p2_ref, y_hbm, x_ref, rf_ref, g_ref, o_ref, buf_ref, sem, *, tm, n_tiles):
    i = pl.program_id(0)
    rows = tm * ROW_SPLIT

    def start_pair(tile_base, slot, r):
        if isinstance(r, int):
            dst = pl.ds(r * ROW_SPLIT, ROW_SPLIT)
        else:
            dst = pl.ds(pl.multiple_of(r * ROW_SPLIT, ROW_SPLIT), ROW_SPLIT)
        for k, pos_ref in enumerate((p1_ref, p2_ref)):
            pos = pos_ref[tile_base + r]
            pltpu.make_async_copy(
                y_hbm.at[pl.ds(pl.multiple_of(pos * ROW_SPLIT, ROW_SPLIT), ROW_SPLIT), :],
                buf_ref.at[slot, k, dst, :], sem.at[slot]).start(priority=k)

    def wait_slot(slot):
        for k in range(2):
            pltpu.make_async_copy(y_hbm.at[pl.ds(0, rows), :], buf_ref.at[slot, k], sem.at[slot]).wait()

    @pl.when(i == 0)
    def _():
        def body(r, carry):
            start_pair(0, 0, r)
            return carry

        lax.fori_loop(0, tm, body, 0)

    for slot in range(2):
        @pl.when(i % 2 == slot)
        def _():
            next_base = jnp.minimum(i + 1, n_tiles - 1) * tm
            for r in range(tm):
                start_pair(next_base, 1 - slot, r)
            wait_slot(slot)
            rf = rf_ref[...]
            g1 = rf[:, 0:1]
            g2 = rf[:, 1:2]
            parts = []
            for s in range(ROW_SPLIT):
                y1 = buf_ref[slot, 0, pl.ds(s, tm, stride=ROW_SPLIT), :]
                y2 = buf_ref[slot, 1, pl.ds(s, tm, stride=ROW_SPLIT), :]
                parts.append(g1 * y1 + g2 * y2)
            f = jnp.concatenate(parts, axis=1)
            o_ref[...] = x_ref[...] + _rms(f, g_ref[...])

    @pl.when(i == n_tiles - 1)
    def _():
        wait_slot(n_tiles % 2)


def _combine(pos1, pos2, y_rows, x2, route_f, g):
    t = x2.shape[0]
    tm = TOKEN_TILE
    grid_spec = pltpu.PrefetchScalarGridSpec(
        num_scalar_prefetch=2,
        grid=(t // tm,),
        in_specs=[pl.BlockSpec(memory_space=pl.ANY),
                  pl.BlockSpec((tm, D_MODEL), lambda i, p1, p2: (i, 0)),
                  pl.BlockSpec((tm, route_f.shape[1]), lambda i, p1, p2: (i, 0)),
                  pl.BlockSpec(g.shape, lambda i, p1, p2: (0, 0))],
        out_specs=pl.BlockSpec((tm, D_MODEL), lambda i, p1, p2: (i, 0)),
        scratch_shapes=[pltpu.VMEM((2, 2, tm * ROW_SPLIT, LANES), _F32),
                        pltpu.SemaphoreType.DMA((2,))],
    )
    return pl.pallas_call(
        functools.partial(_combine_kernel, tm=tm, n_tiles=t // tm),
        grid_spec=grid_spec,
        out_shape=jax.ShapeDtypeStruct((t, D_MODEL), _F32),
        compiler_params=_params(("arbitrary",), 40),
        name="combine",
    )(pos1, pos2, y_rows, x2, route_f, g)


def _swap_halves(w):
    half = w.shape[-1] // 2
    return jnp.concatenate([w[..., half:], w[..., :half]], axis=-1)


def _layer_weights(w_in, w_uq, w_ukv, pool_w, w_out):
    kr = w_in[:, _C_KR:_C_KR + ROPE_DIM]
    krs = _swap_halves(kr)
    win = jnp.concatenate([w_in[:, :_C_KR], kr, kr, krs, krs, w_in[:, _C_KR + ROPE_DIM:]], axis=1)
    q_rope = w_uq[:, :, NOPE_DIM:]
    wq = jnp.concatenate([w_uq[:, :, :NOPE_DIM], q_rope, _swap_halves(q_rope)], axis=2)
    wq = wq.reshape(Q_LORA, N_HEADS * HEAD_W)
    wkv = jnp.concatenate([w_ukv[:, :, :NOPE_DIM].reshape(KV_LORA, N_HEADS * NOPE_DIM),
                           w_ukv[:, :, NOPE_DIM:].reshape(KV_LORA, N_HEADS * V_DIM)], axis=1)
    pw = jnp.zeros((POOL_WIDTH, POOL_WIDTH), _F32)
    for gidx in range(len(POOL_WINDOWS)):
        lo = gidx * POOL_GROUP_DIM
        pw = pw.at[lo:lo + POOL_GROUP_DIM, lo:lo + POOL_GROUP_DIM].set(pool_w[gidx])
    wa = w_out[:N_HEADS * V_DIM]
    wp = w_out[N_HEADS * V_DIM:]
    return tuple(a.astype(_BF16) for a in (win, wq, wkv, pw, wa, wp))


def _rope_table(seq):
    inv = 1.0 / (ROPE_THETA ** (jnp.arange(0, ROPE_DIM, 2, dtype=_F32) / ROPE_DIM))
    ang = jnp.arange(seq, dtype=_F32)[:, None] * inv[None, :]
    cos, sin = jnp.cos(ang), jnp.sin(ang)
    cos2 = jnp.concatenate([cos, cos], axis=1)
    sin2 = jnp.concatenate([-sin, sin], axis=1)
    scale = (NOPE_DIM + ROPE_DIM) ** -0.5
    qmul = jnp.concatenate([jnp.full((seq, NOPE_DIM), scale, _F32), scale * cos2, scale * sin2], axis=1)
    return jnp.concatenate([qmul, cos2, cos2, sin2, sin2], axis=1)


def _routing_plan(route_i, counts, t):
    tm = EXPERT_TILE
    n_tiles = (2 * t) // tm + N_EXPERTS
    cnt = counts[:, 0].astype(jnp.int32)
    padded = ((cnt + tm - 1) // tm) * tm
    ends = jnp.cumsum(padded)
    base = ends - padded
    n_valid = (ends[-1] // tm).astype(jnp.int32)
    tile = jnp.arange(n_tiles, dtype=jnp.int32)
    tile_expert = jnp.minimum(jnp.sum((tile * tm)[:, None] >= ends[None, :], axis=1), N_EXPERTS - 1)
    first_tile = jnp.sum(jnp.where(tile_expert[:, None] == jnp.arange(N_EXPERTS)[None, :],
                                   (base // tm)[None, :], 0), axis=1)
    backward = ((tile - first_tile) % 2 == 1)[:, None]
    steps = jnp.arange(EXPERT_FF_STEPS, dtype=jnp.int32)[None, :]
    step_chunk = jnp.where(backward, EXPERT_FF_STEPS - 1 - steps, steps)
    used = (tile < n_valid)[:, None]
    last = n_valid - 1
    tile_expert = jnp.where(used[:, 0], tile_expert, tile_expert[last])
    step_chunk = jnp.where(used, step_chunk, step_chunk[last, EXPERT_FF_STEPS - 1])
    e1, e2, pos1, pos2 = (route_i[k] for k in range(4))
    for k in range(N_EXPERTS):
        pos1 = pos1 + jnp.where(e1 == k, base[k], 0)
        pos2 = pos2 + jnp.where(e2 == k, base[k], 0)
    src = _invert_positions(pos1, pos2, n_tiles * tm)
    return (tile_expert.astype(jnp.int32), step_chunk.reshape(-1).astype(jnp.int32), n_valid.reshape(1),
            src, pos1, pos2)


def _invert_kernel(p1_ref, p2_ref, zeros_hbm, src_ref, *, unroll):
    pltpu.sync_copy(zeros_hbm, src_ref)

    def fill(j, carry):
        for u in range(unroll):
            tok = j * unroll + u
            src_ref[p1_ref[tok]] = tok
            src_ref[p2_ref[tok]] = tok
        return carry

    lax.fori_loop(0, p1_ref.shape[0] // unroll, fill, 0)


def _invert_positions(pos1, pos2, n_slots):
    unroll = 8
    assert n_slots % unroll == 0 and pos1.shape[0] % unroll == 0
    smem = pl.BlockSpec(memory_space=pltpu.SMEM)
    return pl.pallas_call(
        functools.partial(_invert_kernel, unroll=unroll),
        in_specs=[smem, smem, pl.BlockSpec(memory_space=pl.ANY)],
        out_specs=smem,
        out_shape=jax.ShapeDtypeStruct((n_slots,), jnp.int32),
        name="invert_positions",
    )(pos1, pos2, jnp.zeros((n_slots,), jnp.int32))


def kernel(x, attn_pre_g, attn_post_g, ffn_pre_g, ffn_post_g, w_in, q_norm_g, kv_norm_g, w_uq, w_ukv,
           pool_w, pool_scale, w_out, w_gate_d, w_up_d, w_down_d, w_router, w_gate_e, w_up_e, w_down_e):
    b, seq, d = x.shape
    t = b * seq
    depth = w_in.shape[0]
    assert d == D_MODEL and seq % TOKEN_TILE == 0 and seq % ATTN_TILE == 0
    tab = _rope_table(seq)
    row = lambda a: a.reshape(1, -1)
    x2 = x.reshape(t, d)
    for l in range(depth):
        win, wq, wkv, pw, wa, wp = _layer_weights(w_in[l], w_uq[l], w_ukv[l], pool_w[l], w_out[l])
        q, k, v, p = _in_proj(x2, row(attn_pre_g[l]), win, row(q_norm_g[l]), row(kv_norm_g[l]),
                              wq, wkv, tab, pw, row(pool_scale[l]), seq=seq)
        i = l // 2
        if l % 2 == 1:
            casts = ((w_up_e[i].reshape(-1, w_up_e.shape[-1]),) + CAST_HEADS_WIDE,
                     (w_down_e[i].reshape(-1, d),) + CAST_HEADS_REST)
        else:
            casts = tuple((w,) + CAST_HEADS_REST for w in (w_gate_d[i], w_up_d[i], w_down_d[i]))
            if l + 1 < depth:
                casts += ((w_gate_e[(l + 1) // 2].reshape(-1, w_gate_e.shape[-1]),) + CAST_HEADS_WIDE,)
        a, converted = _attention(q.reshape(b, seq, -1), k.reshape(b, seq, -1), v.reshape(b, seq, -1), casts)
        a2 = a.reshape(t, -1)
        if l % 2 == 0:
            x2 = _out_proj(a2, p, x2, wa, wp, row(attn_post_g[l]), row(ffn_pre_g[l]),
                           dense=tuple(converted[0:3]) + (row(ffn_post_g[l]),))
            next_gate = converted[3].reshape(w_gate_e.shape[1:]) if len(converted) > 3 else None
        else:
            wr_t = w_router[i].T
            wr_hi = wr_t.astype(_BF16)
            wr = jnp.concatenate([wr_hi, (wr_t - wr_hi.astype(_F32)).astype(_BF16)], axis=0)
            x2, h_rows, route_i, route_f, counts = _out_proj(
                a2, p, x2, wa, wp, row(attn_post_g[l]), row(ffn_pre_g[l]), wr)
            tile_expert, step_chunk, n_valid, src, pos1, pos2 = _routing_plan(route_i, counts, t)
            y_rows = _expert_ffn(tile_expert, step_chunk, n_valid, src, h_rows, next_gate,
                                 converted[0].reshape(w_up_e.shape[1:]),
                                 converted[1].reshape(w_down_e.shape[1:]))
            x2 = _combine(pos1, pos2, y_rows, x2, route_f[0:2].T, row(ffn_post_g[l]))
    return x2.reshape(b, seq, d)
```

```python
import functools

import jax
import jax.numpy as jnp
from jax import lax
from jax.experimental import pallas as pl
from jax.experimental.pallas import tpu as pltpu

D_MODEL = 1024
CHUNK = 64
EPS = 1e-6
N_HEADS = 6
NOPE_DIM = 128
ROPE_DIM = 64
V_DIM = 128
Q_LORA = 384
KV_LORA = 256
ROPE_THETA = 10000.0
POOL_WINDOWS = (2, 4, 8, 16)
POOL_GROUP_DIM = 64
POOL_WIDTH = len(POOL_WINDOWS) * POOL_GROUP_DIM
N_EXPERTS = 8
MAX_WINDOW = max(POOL_WINDOWS)

LANES = 128
SUBLANES = 8
HEAD_W = 2 * LANES
ROW_SPLIT = D_MODEL // LANES

_C_Q = 0
_C_KV = Q_LORA
_C_KR = Q_LORA + KV_LORA
_C_KRS = _C_KR + LANES
_C_POOL = _C_KRS + LANES
IN_AUG = _C_POOL + POOL_WIDTH

TOKEN_TILE = 512
IN_SUB_ROWS = 256
ATTN_TILE = 256
EXPERT_TILE = 512
EXPERT_FF_STEPS = 2
EXPERT_FF_SUB = 256
CAST_HEADS_WIDE = (0, 4)
CAST_HEADS_REST = (4, 6)

_F32 = jnp.float32
_BF16 = jnp.bfloat16
_MIB = 1024 * 1024


def _rms(x, g):
    return x * lax.rsqrt(jnp.mean(x * x, axis=-1, keepdims=True) + EPS) * g


def _silu_mul(g, u):
    hg = 0.5 * g
    return (hg + hg * jnp.tanh(hg)) * u


def _params(sem, vmem_mib):
    return pltpu.CompilerParams(dimension_semantics=sem, vmem_limit_bytes=vmem_mib * _MIB)


def _in_kernel(x_ref, g_ref, win_ref, qg_ref, kvg_ref, wq_ref, wkv_ref, tab_ref, pw_ref, ps_ref,
               q_ref, k_ref, v_ref, p_ref, ext_ref, *, tm, tiles_per_seq):
    i = pl.program_id(0)

    @pl.when(i % tiles_per_seq == 0)
    def _():
        ext_ref[0:MAX_WINDOW, :] = jnp.zeros((MAX_WINDOW, POOL_WIDTH), _F32)

    for r0 in range(0, tm, IN_SUB_ROWS):
        rows = slice(r0, r0 + IN_SUB_ROWS)
        h = _rms(x_ref[rows, :], g_ref[...]).astype(_BF16)
        z = jnp.dot(h, win_ref[...], preferred_element_type=_F32)
        cq = _rms(z[:, _C_Q:_C_Q + Q_LORA], qg_ref[...]).astype(_BF16)
        ckv = _rms(z[:, _C_KV:_C_KV + KV_LORA], kvg_ref[...]).astype(_BF16)
        q = jnp.dot(cq, wq_ref[...], preferred_element_type=_F32)
        kv = jnp.dot(ckv, wkv_ref[...], preferred_element_type=_F32)
        tab = tab_ref[rows, :]
        qmul = tab[:, 0:HEAD_W]
        cc = tab[:, HEAD_W:HEAD_W + LANES]
        ss = tab[:, HEAD_W + LANES:HEAD_W + 2 * LANES]
        krr = (z[:, _C_KR:_C_KR + LANES] * cc + z[:, _C_KRS:_C_KRS + LANES] * ss).astype(_BF16)
        for hd in range(N_HEADS):
            q_ref[rows, hd * HEAD_W:(hd + 1) * HEAD_W] = (q[:, hd * HEAD_W:(hd + 1) * HEAD_W] * qmul).astype(_BF16)
            k_ref[rows, hd * HEAD_W:hd * HEAD_W + NOPE_DIM] = kv[:, hd * NOPE_DIM:(hd + 1) * NOPE_DIM].astype(_BF16)
            k_ref[rows, hd * HEAD_W + NOPE_DIM:(hd + 1) * HEAD_W] = krr
        v_ref[rows, :] = kv[:, N_HEADS * NOPE_DIM:].astype(_BF16)
        ext_ref[MAX_WINDOW + r0:MAX_WINDOW + r0 + IN_SUB_ROWS, :] = z[:, _C_POOL:_C_POOL + POOL_WIDTH]

    u = ext_ref[MAX_WINDOW:MAX_WINDOW + tm, :]

    def window_sum(k0, k1, lo):
        acc = ext_ref[MAX_WINDOW - k0:MAX_WINDOW - k0 + tm, lo:lo + LANES]
        for k in range(k0 + 1, k1):
            acc = acc + ext_ref[MAX_WINDOW - k:MAX_WINDOW - k + tm, lo:lo + LANES]
        return acc

    t_pos = (i % tiles_per_seq) * tm + lax.broadcasted_iota(jnp.int32, (tm, LANES), 0)
    cnt = (t_pos + 1).astype(_F32)
    first_group = lax.broadcasted_iota(jnp.int32, (tm, LANES), 1) < POOL_GROUP_DIM
    w0, w1, w2, w3 = POOL_WINDOWS
    s_a = window_sum(0, w0, 0)
    s_b = s_a + window_sum(w0, w1, 0)
    pooled_lo = jnp.where(first_group, s_a / jnp.minimum(cnt, float(w0)), s_b / jnp.minimum(cnt, float(w1)))
    s_c = window_sum(0, w2, LANES)
    s_d = s_c + window_sum(w2, w3, LANES)
    pooled_hi = jnp.where(first_group, s_c / jnp.minimum(cnt, float(w2)), s_d / jnp.minimum(cnt, float(w3)))
    d = jnp.concatenate([pooled_lo - u[:, :LANES], pooled_hi - u[:, LANES:]], axis=1).astype(_BF16)
    y = jnp.dot(d, pw_ref[...], preferred_element_type=_F32) * ps_ref[...]
    p_ref[...] = y.astype(_BF16)
    ext_ref[0:MAX_WINDOW, :] = ext_ref[tm:tm + MAX_WINDOW, :]


def _in_proj(x2, g, win, qg, kvg, wq, wkv, tab, pw, ps, *, seq):
    t = x2.shape[0]
    tm = TOKEN_TILE
    tps = seq // tm
    full = lambda a: pl.BlockSpec(a.shape, lambda i: (0,) * a.ndim)
    return pl.pallas_call(
        functools.partial(_in_kernel, tm=tm, tiles_per_seq=tps),
        grid=(t // tm,),
        in_specs=[pl.BlockSpec((tm, D_MODEL), lambda i: (i, 0)),
                  full(g), full(win), full(qg), full(kvg), full(wq), full(wkv),
                  pl.BlockSpec((tm, tab.shape[1]), lambda i: (i % tps, 0)),
                  full(pw), full(ps)],
        out_specs=[pl.BlockSpec((tm, N_HEADS * HEAD_W), lambda i: (i, 0)),
                   pl.BlockSpec((tm, N_HEADS * HEAD_W), lambda i: (i, 0)),
                   pl.BlockSpec((tm, N_HEADS * V_DIM), lambda i: (i, 0)),
                   pl.BlockSpec((tm, POOL_WIDTH), lambda i: (i, 0))],
        out_shape=[jax.ShapeDtypeStruct((t, N_HEADS * HEAD_W), _BF16),
                   jax.ShapeDtypeStruct((t, N_HEADS * HEAD_W), _BF16),
                   jax.ShapeDtypeStruct((t, N_HEADS * V_DIM), _BF16),
                   jax.ShapeDtypeStruct((t, POOL_WIDTH), _BF16)],
        scratch_shapes=[pltpu.VMEM((tm + MAX_WINDOW, POOL_WIDTH), _F32)],
        compiler_params=_params(("arbitrary",), 48),
        name="in_proj",
    )(x2, g, win, qg, kvg, wq, wkv, tab, pw, ps)


def _attn_kernel(*refs, tq, nq, cast_heads):
    n_cast = len(cast_heads)
    q_ref, k_ref, v_ref = refs[0:3]
    o_ref = refs[3 + n_cast]
    vaug_ref = refs[-1]
    head = pl.program_id(1)
    for (lo, hi), src_ref, dst_ref in zip(cast_heads, refs[3:3 + n_cast], refs[4 + n_cast:4 + 2 * n_cast]):
        @pl.when(jnp.logical_and(head >= lo, head < hi))
        def _():
            dst_ref[...] = src_ref[...].astype(_BF16)

    seq = v_ref.shape[0]
    vaug_ref[:, 0:V_DIM] = v_ref[...]
    ones_col = lax.broadcasted_iota(jnp.int32, (seq, LANES), 1) == 0
    vaug_ref[:, V_DIM:V_DIM + LANES] = jnp.where(ones_col, 1.0, 0.0).astype(_BF16)
    row_chunk = lax.broadcasted_iota(jnp.int32, (tq, tq), 0) // CHUNK
    col_chunk = lax.broadcasted_iota(jnp.int32, (tq, tq), 1) // CHUNK
    visible = col_chunk <= row_chunk
    nt = (((1,), (1,)), ((), ()))
    def scores(i):
        d0 = i * tq
        s = lax.dot_general(q_ref[d0:d0 + tq, :], k_ref[0:d0 + tq, :], nt, preferred_element_type=_F32)
        s_diag = jnp.where(visible, s[:, d0:], -jnp.inf)
        return s_diag if i == 0 else jnp.concatenate([s[:, :d0], s_diag], axis=1)

    order = list(range(nq - 1, -1, -1))
    s_next = scores(order[0])
    for n, i in enumerate(order):
        s = s_next
        d0 = i * tq
        p = jnp.exp(s - jnp.max(s, axis=1, keepdims=True)).astype(_BF16)
        if n + 1 < nq:
            s_next = scores(order[n + 1])
        o = jnp.dot(p, vaug_ref[0:d0 + tq, :], preferred_element_type=_F32)
        o_ref[d0:d0 + tq, :] = (o[:, 0:V_DIM] / o[:, V_DIM:V_DIM + 1]).astype(_BF16)


def _attention(q, k, v, casts=()):
    b, seq, _ = q.shape
    tq = ATTN_TILE
    head = lambda w: pl.BlockSpec((None, seq, w), lambda bb, hh: (bb, 0, hh))
    cast_specs, cast_shapes = [], []
    for w, lo, hi in casts:
        n_h = hi - lo
        rows, cols = w.shape
        assert rows % (b * n_h * 2 * SUBLANES) == 0
        block_of = lambda bb, hh, lo=lo, n_h=n_h: (bb * n_h + jnp.clip(hh - lo, 0, n_h - 1), 0)
        cast_specs.append(pl.BlockSpec((rows // (b * n_h), cols), block_of))
        cast_shapes.append(jax.ShapeDtypeStruct(w.shape, _BF16))
    outs = pl.pallas_call(
        functools.partial(_attn_kernel, tq=tq, nq=seq // tq, cast_heads=tuple((lo, hi) for _, lo, hi in casts)),
        grid=(b, N_HEADS),
        in_specs=[head(HEAD_W), head(HEAD_W), head(V_DIM)] + cast_specs,
        out_specs=[head(V_DIM)] + cast_specs,
        out_shape=[jax.ShapeDtypeStruct((b, seq, N_HEADS * V_DIM), _BF16)] + cast_shapes,
        scratch_shapes=[pltpu.VMEM((seq, V_DIM + LANES), _BF16)],
        compiler_params=_params(("arbitrary", "arbitrary"), 56),
        name="attention",
    )(q, k, v, *(w for w, _, _ in casts))
    return outs[0], outs[1:]


def _mix_residual(a_ref, p_ref, x_ref, wa_ref, wp_ref, g1_ref):
    mix = (jnp.dot(a_ref[...], wa_ref[...], preferred_element_type=_F32)
           + jnp.dot(p_ref[...], wp_ref[...], preferred_element_type=_F32))
    return x_ref[...] + _rms(mix, g1_ref[...])


def _out_dense_kernel(a_ref, p_ref, x_ref, wa_ref, wp_ref, g1_ref, g2_ref,
                      wg_ref, wu_ref, wd_ref, g3_ref, o_ref):
    xn = _mix_residual(a_ref, p_ref, x_ref, wa_ref, wp_ref, g1_ref)
    h = _rms(xn, g2_ref[...]).astype(_BF16)
    gate = jnp.dot(h, wg_ref[...], preferred_element_type=_F32)
    up = jnp.dot(h, wu_ref[...], preferred_element_type=_F32)
    hm = _silu_mul(gate, up).astype(_BF16)
    f = jnp.dot(hm, wd_ref[...], preferred_element_type=_F32)
    o_ref[...] = xn + _rms(f, g3_ref[...])


def _out_moe_kernel(a_ref, p_ref, x_ref, wa_ref, wp_ref, g1_ref, g2_ref, wr_ref,
                    xo_ref, ho_ref, ri_ref, rf_ref, cnt_ref, carry_ref, *, tm):
    @pl.when(pl.program_id(0) == 0)
    def _():
        carry_ref[...] = jnp.zeros_like(carry_ref)

    xn = _mix_residual(a_ref, p_ref, x_ref, wa_ref, wp_ref, g1_ref)
    xo_ref[...] = xn
    h = _rms(xn, g2_ref[...])
    for s in range(ROW_SPLIT):
        ho_ref[pl.ds(s, tm, stride=ROW_SPLIT), :] = h[:, s * LANES:(s + 1) * LANES]

    h_hi = h.astype(_BF16)
    h_lo = (h - h_hi.astype(_F32)).astype(_BF16)
    wr = wr_ref[...]
    nt = (((1,), (1,)), ((), ()))
    by_hi = lax.dot_general(wr, h_hi, nt, preferred_element_type=_F32)
    by_lo = lax.dot_general(wr, h_lo, nt, preferred_element_type=_F32)
    logits = by_hi[0:N_EXPERTS] + by_hi[N_EXPERTS:] + by_lo[0:N_EXPERTS]
    sub = lax.broadcasted_iota(jnp.int32, (N_EXPERTS, tm), 0)
    v1 = jnp.max(logits, axis=0, keepdims=True)
    e1 = jnp.min(jnp.where(logits == v1, sub, N_EXPERTS), axis=0, keepdims=True)
    rest = jnp.where(sub == e1, -jnp.inf, logits)
    v2 = jnp.max(rest, axis=0, keepdims=True)
    e2 = jnp.min(jnp.where(rest == v2, sub, N_EXPERTS), axis=0, keepdims=True)
    ex = jnp.exp(v2 - v1)
    gate1 = 1.0 / (1.0 + ex)
    gate2 = ex * gate1

    sel1 = sub == e1
    sel2 = sub == e2
    member = jnp.logical_or(sel1, sel2).astype(_F32)
    earlier = (lax.broadcasted_iota(jnp.int32, (tm, tm), 0)
               < lax.broadcasted_iota(jnp.int32, (tm, tm), 1)).astype(_BF16)
    member_lhs = jnp.concatenate([member, jnp.zeros_like(member)], axis=0).astype(_BF16)
    carry = carry_ref[:, 0:1]
    before = jnp.dot(member_lhs, earlier, preferred_element_type=_F32)[0:N_EXPERTS] + carry
    r1 = jnp.sum(jnp.where(sel1, before, 0.0), axis=0, keepdims=True).astype(jnp.int32)
    r2 = jnp.sum(jnp.where(sel2, before, 0.0), axis=0, keepdims=True).astype(jnp.int32)
    total = carry + jnp.sum(member, axis=1, keepdims=True)
    carry_ref[...] = jnp.broadcast_to(total, carry_ref.shape)
    cnt_ref[...] = jnp.broadcast_to(total, cnt_ref.shape)

    ri_ref[...] = jnp.where(sub == 0, e1, jnp.where(sub == 1, e2, jnp.where(sub == 2, r1, r2)))
    rf_ref[...] = jnp.where(sub == 0, gate1, gate2)


def _out_proj(a2, p2, x2, wa, wp, g1, g2, w_router=None, dense=None):
    t = x2.shape[0]
    tm = TOKEN_TILE
    full = lambda a: pl.BlockSpec(a.shape, lambda i: (0,) * a.ndim)
    resident = lambda a: pl.BlockSpec(a.shape, lambda i: (0,) * a.ndim, pipeline_mode=pl.Buffered(1))
    row = lambda w: pl.BlockSpec((tm, w), lambda i: (i, 0))
    in_specs = [row(a2.shape[1]), row(p2.shape[1]), row(D_MODEL), full(wa), full(wp), full(g1), full(g2)]
    if w_router is None:
        wg, wu, wd, g3 = dense
        return pl.pallas_call(
            _out_dense_kernel,
            grid=(t // tm,),
            in_specs=in_specs + [resident(wg), resident(wu), resident(wd), full(g3)],
            out_specs=row(D_MODEL),
            out_shape=jax.ShapeDtypeStruct((t, D_MODEL), _F32),
            compiler_params=_params(("arbitrary",), 56),
            name="out_proj_dense_ffn",
        )(a2, p2, x2, wa, wp, g1, g2, wg, wu, wd, g3)
    return pl.pallas_call(
        functools.partial(_out_moe_kernel, tm=tm),
        grid=(t // tm,),
        in_specs=in_specs + [full(w_router)],
        out_specs=[row(D_MODEL),
                   pl.BlockSpec((tm * ROW_SPLIT, LANES), lambda i: (i, 0)),
                   pl.BlockSpec((N_EXPERTS, tm), lambda i: (0, i)),
                   pl.BlockSpec((N_EXPERTS, tm), lambda i: (0, i)),
                   pl.BlockSpec((N_EXPERTS, LANES), lambda i: (0, 0))],
        out_shape=[jax.ShapeDtypeStruct((t, D_MODEL), _F32),
                   jax.ShapeDtypeStruct((t * ROW_SPLIT, LANES), _F32),
                   jax.ShapeDtypeStruct((N_EXPERTS, t), jnp.int32),
                   jax.ShapeDtypeStruct((N_EXPERTS, t), _F32),
                   jax.ShapeDtypeStruct((N_EXPERTS, LANES), _F32)],
        scratch_shapes=[pltpu.VMEM((N_EXPERTS, LANES), _F32)],
        compiler_params=_params(("arbitrary",), 48),
        name="out_proj_router",
    )(a2, p2, x2, wa, wp, g1, g2, w_router)


def _expert_kernel(te_ref, ch_ref, nv_ref, src_ref, h_hbm, wg_ref, wu_ref, wd_ref, y_ref,
                   buf_ref, xb_ref, acc_ref, sem, *, tm, n_tiles, n_steps):
    i = pl.program_id(0)
    c = pl.program_id(1)
    last_c = n_steps - 1
    nv = nv_ref[0]
    rows = tm * ROW_SPLIT

    def start_row(tile_base, slot, r):
        tok = src_ref[tile_base + r]
        pltpu.make_async_copy(
            h_hbm.at[pl.ds(pl.multiple_of(tok * ROW_SPLIT, ROW_SPLIT), ROW_SPLIT), :],
            buf_ref.at[slot, pl.ds(pl.multiple_of(r * ROW_SPLIT, ROW_SPLIT), ROW_SPLIT), :],
            sem.at[slot]).start()

    def wait_slot(slot):
        pltpu.make_async_copy(h_hbm.at[pl.ds(0, rows), :], buf_ref.at[slot], sem.at[slot]).wait()

    @pl.when(jnp.logical_and(i == 0, c == 0))
    def _():
        def body(r, carry):
            start_row(0, 0, r)
            return carry

        lax.fori_loop(0, tm, body, 0)

    @pl.when(jnp.logical_and(c == 0, i == nv))
    def _():
        wait_slot(i % 2)

    def swiglu_part():
        x = xb_ref[...]
        part = None
        for k0 in range(0, wg_ref.shape[1], EXPERT_FF_SUB):
            k1 = k0 + EXPERT_FF_SUB
            gate = jnp.dot(x, wg_ref[:, k0:k1], preferred_element_type=_F32)
            up = jnp.dot(x, wu_ref[:, k0:k1], preferred_element_type=_F32)
            hm = _silu_mul(gate, up).astype(_BF16)
            down = jnp.dot(hm, wd_ref[k0:k1, :], preferred_element_type=_F32)
            part = down if part is None else part + down
        return part

    @pl.when(jnp.logical_and(i < nv, c == 0))
    def _():
        slot = i % 2
        wait_slot(slot)
        for s in range(ROW_SPLIT):
            xb_ref[:, s * LANES:(s + 1) * LANES] = (
                buf_ref[slot, pl.ds(s, tm, stride=ROW_SPLIT), :].astype(_BF16))
        next_base = jnp.minimum(i + 1, n_tiles - 1) * tm
        for r in range(tm):
            start_row(next_base, 1 - slot, r)
        acc_ref[...] = swiglu_part()

    if n_steps > 2:
        @pl.when(jnp.logical_and(i < nv, jnp.logical_and(c > 0, c < last_c)))
        def _():
            acc_ref[...] += swiglu_part()

    @pl.when(jnp.logical_and(i < nv, c == last_c))
    def _():
        total = acc_ref[...] + swiglu_part()
        for s in range(ROW_SPLIT):
            y_ref[pl.ds(s, tm, stride=ROW_SPLIT), :] = total[:, s * LANES:(s + 1) * LANES]

    @pl.when(jnp.logical_and(c == last_c, i >= nv))
    def _():
        y_ref[...] = jnp.zeros_like(y_ref)

    @pl.when(jnp.logical_and(jnp.logical_and(i == n_tiles - 1, c == last_c), nv == n_tiles))
    def _():
        wait_slot(n_tiles % 2)


def _expert_ffn(tile_expert, step_chunk, n_valid, src_tok, h_rows, wg, wu, wd):
    tm = EXPERT_TILE
    n_tiles = tile_expert.shape[0]
    steps = EXPERT_FF_STEPS
    assert steps >= 2
    fc = wg.shape[2] // steps

    col_map = lambda i, c, te, ch, nv, src: (te[i], 0, ch[i * steps + c])
    row_map = lambda i, c, te, ch, nv, src: (te[i], ch[i * steps + c], 0)
    grid_spec = pltpu.PrefetchScalarGridSpec(
        num_scalar_prefetch=4,
        grid=(n_tiles, steps),
        in_specs=[pl.BlockSpec(memory_space=pl.ANY),
                  pl.BlockSpec((None, D_MODEL, fc), col_map),
                  pl.BlockSpec((None, D_MODEL, fc), col_map),
                  pl.BlockSpec((None, fc, D_MODEL), row_map)],
        out_specs=pl.BlockSpec((tm * ROW_SPLIT, LANES), lambda i, c, te, ch, nv, src: (i, 0)),
        scratch_shapes=[pltpu.VMEM((2, tm * ROW_SPLIT, LANES), _F32),
                        pltpu.VMEM((tm, D_MODEL), _BF16),
                        pltpu.VMEM((tm, D_MODEL), _F32),
                        pltpu.SemaphoreType.DMA((2,))],
    )
    return pl.pallas_call(
        functools.partial(_expert_kernel, tm=tm, n_tiles=n_tiles, n_steps=steps),
        grid_spec=grid_spec,
        out_shape=jax.ShapeDtypeStruct((n_tiles * tm * ROW_SPLIT, LANES), _F32),
        compiler_params=_params(("arbitrary", "arbitrary"), 56),
        name="expert_ffn",
    )(tile_expert, step_chunk, n_valid, src_tok, h_rows, wg, wu, wd)


def _combine_kernel(p1_ref, p2_ref, y_hbm, x_ref, rf_ref, g_ref, o_ref, buf_ref, sem, *, tm, n_tiles):
    i = pl.program_id(0)
    rows = tm * ROW_SPLIT

    def start_pair(tile_base, slot, r):
        if isinstance(r, int):
            dst = pl.ds(r * ROW_SPLIT, ROW_SPLIT)
        else:
            dst = pl.ds(pl.multiple_of(r * ROW_SPLIT, ROW_SPLIT), ROW_SPLIT)
        for k, pos_ref in enumerate((p1_ref, p2_ref)):
            pos = pos_ref[tile_base + r]
            pltpu.make_async_copy(
                y_hbm.at[pl.ds(pl.multiple_of(pos * ROW_SPLIT, ROW_SPLIT), ROW_SPLIT), :],
                buf_ref.at[slot, k, dst, :], sem.at[slot]).start(priority=k)

    def wait_slot(slot):
        for k in range(2):
            pltpu.make_async_copy(y_hbm.at[pl.ds(0, rows), :], buf_ref.at[slot, k], sem.at[slot]).wait()

    @pl.when(i == 0)
    def _():
        def body(r, carry):
            start_pair(0, 0, r)
            return carry

        lax.fori_loop(0, tm, body, 0)

    for slot in range(2):
        @pl.when(i % 2 == slot)
        def _():
            wait_slot(slot)
            next_base = jnp.minimum(i + 1, n_tiles - 1) * tm
            for r in range(tm):
                start_pair(next_base, 1 - slot, r)
            rf = rf_ref[...]
            g1 = rf[:, 0:1]
            g2 = rf[:, 1:2]
            parts = []
            for s in range(ROW_SPLIT):
                y1 = buf_ref[slot, 0, pl.ds(s, tm, stride=ROW_SPLIT), :]
                y2 = buf_ref[slot, 1, pl.ds(s, tm, stride=ROW_SPLIT), :]
                parts.append(g1 * y1 + g2 * y2)
            f = jnp.concatenate(parts, axis=1)
            o_ref[...] = x_ref[...] + _rms(f, g_ref[...])

    @pl.when(i == n_tiles - 1)
    def _():
        wait_slot(n_tiles % 2)


def _combine(pos1, pos2, y_rows, x2, route_f, g):
    t = x2.shape[0]
    tm = TOKEN_TILE
    grid_spec = pltpu.PrefetchScalarGridSpec(
        num_scalar_prefetch=2,
        grid=(t // tm,),
        in_specs=[pl.BlockSpec(memory_space=pl.ANY),
                  pl.BlockSpec((tm, D_MODEL), lambda i, p1, p2: (i, 0)),
                  pl.BlockSpec((tm, route_f.shape[1]), lambda i, p1, p2: (i, 0)),
                  pl.BlockSpec(g.shape, lambda i, p1, p2: (0, 0))],
        out_specs=pl.BlockSpec((tm, D_MODEL), lambda i, p1, p2: (i, 0)),
        scratch_shapes=[pltpu.VMEM((2, 2, tm * ROW_SPLIT, LANES), _F32),
                        pltpu.SemaphoreType.DMA((2,))],
    )
    return pl.pallas_call(
        functools.partial(_combine_kernel, tm=tm, n_tiles=t // tm),
        grid_spec=grid_spec,
        out_shape=jax.ShapeDtypeStruct((t, D_MODEL), _F32),
        compiler_params=_params(("arbitrary",), 40),
        name="combine",
    )(pos1, pos2, y_rows, x2, route_f, g)


def _swap_halves(w):
    half = w.shape[-1] // 2
    return jnp.concatenate([w[..., half:], w[..., :half]], axis=-1)


def _layer_weights(w_in, w_uq, w_ukv, pool_w, w_out):
    kr = w_in[:, _C_KR:_C_KR + ROPE_DIM]
    krs = _swap_halves(kr)
    win = jnp.concatenate([w_in[:, :_C_KR], kr, kr, krs, krs, w_in[:, _C_KR + ROPE_DIM:]], axis=1)
    q_rope = w_uq[:, :, NOPE_DIM:]
    wq = jnp.concatenate([w_uq[:, :, :NOPE_DIM], q_rope, _swap_halves(q_rope)], axis=2)
    wq = wq.reshape(Q_LORA, N_HEADS * HEAD_W)
    wkv = jnp.concatenate([w_ukv[:, :, :NOPE_DIM].reshape(KV_LORA, N_HEADS * NOPE_DIM),
                           w_ukv[:, :, NOPE_DIM:].reshape(KV_LORA, N_HEADS * V_DIM)], axis=1)
    pw = jnp.zeros((POOL_WIDTH, POOL_WIDTH), _F32)
    for gidx in range(len(POOL_WINDOWS)):
        lo = gidx * POOL_GROUP_DIM
        pw = pw.at[lo:lo + POOL_GROUP_DIM, lo:lo + POOL_GROUP_DIM].set(pool_w[gidx])
    wa = w_out[:N_HEADS * V_DIM]
    wp = w_out[N_HEADS * V_DIM:]
    return tuple(a.astype(_BF16) for a in (win, wq, wkv, pw, wa, wp))


def _rope_table(seq):
    inv = 1.0 / (ROPE_THETA ** (jnp.arange(0, ROPE_DIM, 2, dtype=_F32) / ROPE_DIM))
    ang = jnp.arange(seq, dtype=_F32)[:, None] * inv[None, :]
    cos, sin = jnp.cos(ang), jnp.sin(ang)
    cos2 = jnp.concatenate([cos, cos], axis=1)
    sin2 = jnp.concatenate([-sin, sin], axis=1)
    scale = (NOPE_DIM + ROPE_DIM) ** -0.5
    qmul = jnp.concatenate([jnp.full((seq, NOPE_DIM), scale, _F32), scale * cos2, scale * sin2], axis=1)
    return jnp.concatenate([qmul, cos2, cos2, sin2, sin2], axis=1)


def _routing_plan(route_i, counts, t):
    tm = EXPERT_TILE
    n_tiles = (2 * t) // tm + N_EXPERTS
    cnt = counts[:, 0].astype(jnp.int32)
    padded = ((cnt + tm - 1) // tm) * tm
    ends = jnp.cumsum(padded)
    base = ends - padded
    n_valid = (ends[-1] // tm).astype(jnp.int32)
    tile = jnp.arange(n_tiles, dtype=jnp.int32)
    tile_expert = jnp.minimum(jnp.sum((tile * tm)[:, None] >= ends[None, :], axis=1), N_EXPERTS - 1)
    first_tile = jnp.sum(jnp.where(tile_expert[:, None] == jnp.arange(N_EXPERTS)[None, :],
                                   (base // tm)[None, :], 0), axis=1)
    backward = ((tile - first_tile) % 2 == 1)[:, None]
    steps = jnp.arange(EXPERT_FF_STEPS, dtype=jnp.int32)[None, :]
    step_chunk = jnp.where(backward, EXPERT_FF_STEPS - 1 - steps, steps)
    used = (tile < n_valid)[:, None]
    last = n_valid - 1
    tile_expert = jnp.where(used[:, 0], tile_expert, tile_expert[last])
    step_chunk = jnp.where(used, step_chunk, step_chunk[last, EXPERT_FF_STEPS - 1])
    e1, e2, pos1, pos2 = (route_i[k] for k in range(4))
    for k in range(N_EXPERTS):
        pos1 = pos1 + jnp.where(e1 == k, base[k], 0)
        pos2 = pos2 + jnp.where(e2 == k, base[k], 0)
    src = _invert_positions(pos1, pos2, n_tiles * tm)
    return (tile_expert.astype(jnp.int32), step_chunk.reshape(-1).astype(jnp.int32), n_valid.reshape(1),
            src, pos1, pos2)


def _invert_kernel(p1_ref, p2_ref, zeros_hbm, src_ref, *, unroll):
    pltpu.sync_copy(zeros_hbm, src_ref)

    def fill(j, carry):
        for u in range(unroll):
            tok = j * unroll + u
            src_ref[p1_ref[tok]] = tok
            src_ref[p2_ref[tok]] = tok
        return carry

    lax.fori_loop(0, p1_ref.shape[0] // unroll, fill, 0)


def _invert_positions(pos1, pos2, n_slots):
    unroll = 8
    assert n_slots % unroll == 0 and pos1.shape[0] % unroll == 0
    smem = pl.BlockSpec(memory_space=pltpu.SMEM)
    return pl.pallas_call(
        functools.partial(_invert_kernel, unroll=unroll),
        in_specs=[smem, smem, pl.BlockSpec(memory_space=pl.ANY)],
        out_specs=smem,
        out_shape=jax.ShapeDtypeStruct((n_slots,), jnp.int32),
        name="invert_positions",
    )(pos1, pos2, jnp.zeros((n_slots,), jnp.int32))


def kernel(x, attn_pre_g, attn_post_g, ffn_pre_g, ffn_post_g, w_in, q_norm_g, kv_norm_g, w_uq, w_ukv,
           pool_w, pool_scale, w_out, w_gate_d, w_up_d, w_down_d, w_router, w_gate_e, w_up_e, w_down_e):
    b, seq, d = x.shape
    t = b * seq
    depth = w_in.shape[0]
    assert d == D_MODEL and seq % TOKEN_TILE == 0 and seq % ATTN_TILE == 0
    tab = _rope_table(seq)
    row = lambda a: a.reshape(1, -1)
    x2 = x.reshape(t, d)
    for l in range(depth):
        win, wq, wkv, pw, wa, wp = _layer_weights(w_in[l], w_uq[l], w_ukv[l], pool_w[l], w_out[l])
        q, k, v, p = _in_proj(x2, row(attn_pre_g[l]), win, row(q_norm_g[l]), row(kv_norm_g[l]),
                              wq, wkv, tab, pw, row(pool_scale[l]), seq=seq)
        i = l // 2
        if l % 2 == 1:
            casts = ((w_up_e[i].reshape(-1, w_up_e.shape[-1]),) + CAST_HEADS_WIDE,
                     (w_down_e[i].reshape(-1, d),) + CAST_HEADS_REST)
        else:
            casts = tuple((w,) + CAST_HEADS_REST for w in (w_gate_d[i], w_up_d[i], w_down_d[i]))
            if l + 1 < depth:
                casts += ((w_gate_e[(l + 1) // 2].reshape(-1, w_gate_e.shape[-1]),) + CAST_HEADS_WIDE,)
        a, converted = _attention(q.reshape(b, seq, -1), k.reshape(b, seq, -1), v.reshape(b, seq, -1), casts)
        a2 = a.reshape(t, -1)
        if l % 2 == 0:
            x2 = _out_proj(a2, p, x2, wa, wp, row(attn_post_g[l]), row(ffn_pre_g[l]),
                           dense=tuple(converted[0:3]) + (row(ffn_post_g[l]),))
            next_gate = converted[3].reshape(w_gate_e.shape[1:]) if len(converted) > 3 else None
        else:
            wr_t = w_router[i].T
            wr_hi = wr_t.astype(_BF16)
            wr = jnp.concatenate([wr_hi, (wr_t - wr_hi.astype(_F32)).astype(_BF16)], axis=0)
            x2, h_rows, route_i, route_f, counts = _out_proj(
                a2, p, x2, wa, wp, row(attn_post_g[l]), row(ffn_pre_g[l]), wr)
            tile_expert, step_chunk, n_valid, src, pos1, pos2 = _routing_plan(route_i, counts, t)
            y_rows = _expert_ffn(tile_expert, step_chunk, n_valid, src, h_rows, next_gate,
                                 converted[0].reshape(w_up_e.shape[1:]),
                                 converted[1].reshape(w_down_e.shape[1:]))
            x2 = _combine(pos1, pos2, y_rows, x2, route_f[0:2].T, row(ffn_post_g[l]))
    return x2.reshape(b, seq, d)
```

```python
import functools

import jax
import jax.numpy as jnp
from jax import lax
from jax.experimental import pallas as pl
from jax.experimental.pallas import tpu as pltpu

D_MODEL = 1024
CHUNK = 64
EPS = 1e-6
N_HEADS = 6
NOPE_DIM = 128
ROPE_DIM = 64
V_DIM = 128
Q_LORA = 384
KV_LORA = 256
ROPE_THETA = 10000.0
POOL_WINDOWS = (2, 4, 8, 16)
POOL_GROUP_DIM = 64
POOL_WIDTH = len(POOL_WINDOWS) * POOL_GROUP_DIM
N_EXPERTS = 8
MAX_WINDOW = max(POOL_WINDOWS)

LANES = 128
SUBLANES = 8
HEAD_W = 2 * LANES
ROW_SPLIT = D_MODEL // LANES

_C_Q = 0
_C_KV = Q_LORA
_C_KR = Q_LORA + KV_LORA
_C_KRS = _C_KR + LANES
_C_POOL = _C_KRS + LANES
IN_AUG = _C_POOL + POOL_WIDTH

TOKEN_TILE = 512
IN_SUB_ROWS = 256
ATTN_TILE = 256
EXPERT_TILE = 512
EXPERT_FF_STEPS = 1
EXPERT_FF_SUB = 256
CAST_HEADS_WIDE = (0, 4)
CAST_HEADS_REST = (4, 6)

_F32 = jnp.float32
_BF16 = jnp.bfloat16
_MIB = 1024 * 1024


def _rms(x, g):
    return x * lax.rsqrt(jnp.mean(x * x, axis=-1, keepdims=True) + EPS) * g


def _silu_mul(g, u):
    hg = 0.5 * g
    return (hg + hg * jnp.tanh(hg)) * u


def _params(sem, vmem_mib):
    return pltpu.CompilerParams(dimension_semantics=sem, vmem_limit_bytes=vmem_mib * _MIB)


def _in_kernel(x_ref, g_ref, win_ref, qg_ref, kvg_ref, wq_ref, wkv_ref, tab_ref, pw_ref, ps_ref,
               q_ref, k_ref, v_ref, p_ref, ext_ref, *, tm, tiles_per_seq):
    i = pl.program_id(0)

    @pl.when(i % tiles_per_seq == 0)
    def _():
        ext_ref[0:MAX_WINDOW, :] = jnp.zeros((MAX_WINDOW, POOL_WIDTH), _F32)

    for r0 in range(0, tm, IN_SUB_ROWS):
        rows = slice(r0, r0 + IN_SUB_ROWS)
        h = _rms(x_ref[rows, :], g_ref[...]).astype(_BF16)
        z = jnp.dot(h, win_ref[...], preferred_element_type=_F32)
        cq = _rms(z[:, _C_Q:_C_Q + Q_LORA], qg_ref[...]).astype(_BF16)
        ckv = _rms(z[:, _C_KV:_C_KV + KV_LORA], kvg_ref[...]).astype(_BF16)
        q = jnp.dot(cq, wq_ref[...], preferred_element_type=_F32)
        kv = jnp.dot(ckv, wkv_ref[...], preferred_element_type=_F32)
        tab = tab_ref[rows, :]
        qmul = tab[:, 0:HEAD_W]
        cc = tab[:, HEAD_W:HEAD_W + LANES]
        ss = tab[:, HEAD_W + LANES:HEAD_W + 2 * LANES]
        krr = (z[:, _C_KR:_C_KR + LANES] * cc + z[:, _C_KRS:_C_KRS + LANES] * ss).astype(_BF16)
        for hd in range(N_HEADS):
            q_ref[rows, hd * HEAD_W:(hd + 1) * HEAD_W] = (q[:, hd * HEAD_W:(hd + 1) * HEAD_W] * qmul).astype(_BF16)
            k_ref[rows, hd * HEAD_W:hd * HEAD_W + NOPE_DIM] = kv[:, hd * NOPE_DIM:(hd + 1) * NOPE_DIM].astype(_BF16)
            k_ref[rows, hd * HEAD_W + NOPE_DIM:(hd + 1) * HEAD_W] = krr
        v_ref[rows, :] = kv[:, N_HEADS * NOPE_DIM:].astype(_BF16)
        ext_ref[MAX_WINDOW + r0:MAX_WINDOW + r0 + IN_SUB_ROWS, :] = z[:, _C_POOL:_C_POOL + POOL_WIDTH]

    u = ext_ref[MAX_WINDOW:MAX_WINDOW + tm, :]

    def window_sum(k0, k1, lo):
        acc = ext_ref[MAX_WINDOW - k0:MAX_WINDOW - k0 + tm, lo:lo + LANES]
        for k in range(k0 + 1, k1):
            acc = acc + ext_ref[MAX_WINDOW - k:MAX_WINDOW - k + tm, lo:lo + LANES]
        return acc

    t_pos = (i % tiles_per_seq) * tm + lax.broadcasted_iota(jnp.int32, (tm, LANES), 0)
    cnt = (t_pos + 1).astype(_F32)
    first_group = lax.broadcasted_iota(jnp.int32, (tm, LANES), 1) < POOL_GROUP_DIM
    w0, w1, w2, w3 = POOL_WINDOWS
    s_a = window_sum(0, w0, 0)
    s_b = s_a + window_sum(w0, w1, 0)
    pooled_lo = jnp.where(first_group, s_a / jnp.minimum(cnt, float(w0)), s_b / jnp.minimum(cnt, float(w1)))
    s_c = window_sum(0, w2, LANES)
    s_d = s_c + window_sum(w2, w3, LANES)
    pooled_hi = jnp.where(first_group, s_c / jnp.minimum(cnt, float(w2)), s_d / jnp.minimum(cnt, float(w3)))
    d = jnp.concatenate([pooled_lo - u[:, :LANES], pooled_hi - u[:, LANES:]], axis=1).astype(_BF16)
    y = jnp.dot(d, pw_ref[...], preferred_element_type=_F32) * ps_ref[...]
    p_ref[...] = y.astype(_BF16)
    ext_ref[0:MAX_WINDOW, :] = ext_ref[tm:tm + MAX_WINDOW, :]


def _in_proj(x2, g, win, qg, kvg, wq, wkv, tab, pw, ps, *, seq):
    t = x2.shape[0]
    tm = TOKEN_TILE
    tps = seq // tm
    full = lambda a: pl.BlockSpec(a.shape, lambda i: (0,) * a.ndim)
    return pl.pallas_call(
        functools.partial(_in_kernel, tm=tm, tiles_per_seq=tps),
        grid=(t // tm,),
        in_specs=[pl.BlockSpec((tm, D_MODEL), lambda i: (i, 0)),
                  full(g), full(win), full(qg), full(kvg), full(wq), full(wkv),
                  pl.BlockSpec((tm, tab.shape[1]), lambda i: (i % tps, 0)),
                  full(pw), full(ps)],
        out_specs=[pl.BlockSpec((tm, N_HEADS * HEAD_W), lambda i: (i, 0)),
                   pl.BlockSpec((tm, N_HEADS * HEAD_W), lambda i: (i, 0)),
                   pl.BlockSpec((tm, N_HEADS * V_DIM), lambda i: (i, 0)),
                   pl.BlockSpec((tm, POOL_WIDTH), lambda i: (i, 0))],
        out_shape=[jax.ShapeDtypeStruct((t, N_HEADS * HEAD_W), _BF16),
                   jax.ShapeDtypeStruct((t, N_HEADS * HEAD_W), _BF16),
                   jax.ShapeDtypeStruct((t, N_HEADS * V_DIM), _BF16),
                   jax.ShapeDtypeStruct((t, POOL_WIDTH), _BF16)],
        scratch_shapes=[pltpu.VMEM((tm + MAX_WINDOW, POOL_WIDTH), _F32)],
        compiler_params=_params(("arbitrary",), 48),
        name="in_proj",
    )(x2, g, win, qg, kvg, wq, wkv, tab, pw, ps)


def _attn_kernel(*refs, tq, nq, cast_heads):
    n_cast = len(cast_heads)
    q_ref, k_ref, v_ref = refs[0:3]
    o_ref = refs[3 + n_cast]
    vaug_ref = refs[-1]
    head = pl.program_id(1)
    for (lo, hi), src_ref, dst_ref in zip(cast_heads, refs[3:3 + n_cast], refs[4 + n_cast:4 + 2 * n_cast]):
        @pl.when(jnp.logical_and(head >= lo, head < hi))
        def _():
            dst_ref[...] = src_ref[...].astype(_BF16)

    seq = v_ref.shape[0]
    vaug_ref[:, 0:V_DIM] = v_ref[...]
    ones_col = lax.broadcasted_iota(jnp.int32, (seq, LANES), 1) == 0
    vaug_ref[:, V_DIM:V_DIM + LANES] = jnp.where(ones_col, 1.0, 0.0).astype(_BF16)
    row_chunk = lax.broadcasted_iota(jnp.int32, (tq, tq), 0) // CHUNK
    col_chunk = lax.broadcasted_iota(jnp.int32, (tq, tq), 1) // CHUNK
    visible = col_chunk <= row_chunk
    nt = (((1,), (1,)), ((), ()))
    def scores(i):
        d0 = i * tq
        s = lax.dot_general(q_ref[d0:d0 + tq, :], k_ref[0:d0 + tq, :], nt, preferred_element_type=_F32)
        s_diag = jnp.where(visible, s[:, d0:], -jnp.inf)
        return s_diag if i == 0 else jnp.concatenate([s[:, :d0], s_diag], axis=1)

    order = list(range(nq - 1, -1, -1))
    s_next = scores(order[0])
    for n, i in enumerate(order):
        s = s_next
        d0 = i * tq
        p = jnp.exp(s - jnp.max(s, axis=1, keepdims=True)).astype(_BF16)
        if n + 1 < nq:
            s_next = scores(order[n + 1])
        o = jnp.dot(p, vaug_ref[0:d0 + tq, :], preferred_element_type=_F32)
        o_ref[d0:d0 + tq, :] = (o[:, 0:V_DIM] / o[:, V_DIM:V_DIM + 1]).astype(_BF16)


def _attention(q, k, v, casts=()):
    b, seq, _ = q.shape
    tq = ATTN_TILE
    head = lambda w: pl.BlockSpec((None, seq, w), lambda bb, hh: (bb, 0, hh))
    cast_specs, cast_shapes = [], []
    for w, lo, hi in casts:
        n_h = hi - lo
        rows, cols = w.shape
        assert rows % (b * n_h * 2 * SUBLANES) == 0
        block_of = lambda bb, hh, lo=lo, n_h=n_h: (bb * n_h + jnp.clip(hh - lo, 0, n_h - 1), 0)
        cast_specs.append(pl.BlockSpec((rows // (b * n_h), cols), block_of))
        cast_shapes.append(jax.ShapeDtypeStruct(w.shape, _BF16))
    outs = pl.pallas_call(
        functools.partial(_attn_kernel, tq=tq, nq=seq // tq, cast_heads=tuple((lo, hi) for _, lo, hi in casts)),
        grid=(b, N_HEADS),
        in_specs=[head(HEAD_W), head(HEAD_W), head(V_DIM)] + cast_specs,
        out_specs=[head(V_DIM)] + cast_specs,
        out_shape=[jax.ShapeDtypeStruct((b, seq, N_HEADS * V_DIM), _BF16)] + cast_shapes,
        scratch_shapes=[pltpu.VMEM((seq, V_DIM + LANES), _BF16)],
        compiler_params=_params(("arbitrary", "arbitrary"), 56),
        name="attention",
    )(q, k, v, *(w for w, _, _ in casts))
    return outs[0], outs[1:]


def _mix_residual(a_ref, p_ref, x_ref, wa_ref, wp_ref, g1_ref):
    mix = (jnp.dot(a_ref[...], wa_ref[...], preferred_element_type=_F32)
           + jnp.dot(p_ref[...], wp_ref[...], preferred_element_type=_F32))
    return x_ref[...] + _rms(mix, g1_ref[...])


def _out_dense_kernel(a_ref, p_ref, x_ref, wa_ref, wp_ref, g1_ref, g2_ref,
                      wg_ref, wu_ref, wd_ref, g3_ref, o_ref):
    xn = _mix_residual(a_ref, p_ref, x_ref, wa_ref, wp_ref, g1_ref)
    h = _rms(xn, g2_ref[...]).astype(_BF16)
    gate = jnp.dot(h, wg_ref[...], preferred_element_type=_F32)
    up = jnp.dot(h, wu_ref[...], preferred_element_type=_F32)
    hm = _silu_mul(gate, up).astype(_BF16)
    f = jnp.dot(hm, wd_ref[...], preferred_element_type=_F32)
    o_ref[...] = xn + _rms(f, g3_ref[...])


def _out_moe_kernel(a_ref, p_ref, x_ref, wa_ref, wp_ref, g1_ref, g2_ref, wr_ref,
                    xo_ref, ho_ref, ri_ref, rf_ref, cnt_ref, carry_ref, *, tm):
    @pl.when(pl.program_id(0) == 0)
    def _():
        carry_ref[...] = jnp.zeros_like(carry_ref)

    xn = _mix_residual(a_ref, p_ref, x_ref, wa_ref, wp_ref, g1_ref)
    xo_ref[...] = xn
    h = _rms(xn, g2_ref[...])
    for s in range(ROW_SPLIT):
        ho_ref[pl.ds(s, tm, stride=ROW_SPLIT), :] = h[:, s * LANES:(s + 1) * LANES]

    h_hi = h.astype(_BF16)
    h_lo = (h - h_hi.astype(_F32)).astype(_BF16)
    wr = wr_ref[...]
    nt = (((1,), (1,)), ((), ()))
    by_hi = lax.dot_general(wr, h_hi, nt, preferred_element_type=_F32)
    by_lo = lax.dot_general(wr, h_lo, nt, preferred_element_type=_F32)
    logits = by_hi[0:N_EXPERTS] + by_hi[N_EXPERTS:] + by_lo[0:N_EXPERTS]
    sub = lax.broadcasted_iota(jnp.int32, (N_EXPERTS, tm), 0)
    v1 = jnp.max(logits, axis=0, keepdims=True)
    e1 = jnp.min(jnp.where(logits == v1, sub, N_EXPERTS), axis=0, keepdims=True)
    rest = jnp.where(sub == e1, -jnp.inf, logits)
    v2 = jnp.max(rest, axis=0, keepdims=True)
    e2 = jnp.min(jnp.where(rest == v2, sub, N_EXPERTS), axis=0, keepdims=True)
    ex = jnp.exp(v2 - v1)
    gate1 = 1.0 / (1.0 + ex)
    gate2 = ex * gate1

    sel1 = sub == e1
    sel2 = sub == e2
    member = jnp.logical_or(sel1, sel2).astype(_F32)
    earlier = (lax.broadcasted_iota(jnp.int32, (tm, tm), 0)
               < lax.broadcasted_iota(jnp.int32, (tm, tm), 1)).astype(_BF16)
    member_lhs = jnp.concatenate([member, jnp.zeros_like(member)], axis=0).astype(_BF16)
    carry = carry_ref[:, 0:1]
    before = jnp.dot(member_lhs, earlier, preferred_element_type=_F32)[0:N_EXPERTS] + carry
    r1 = jnp.sum(jnp.where(sel1, before, 0.0), axis=0, keepdims=True).astype(jnp.int32)
    r2 = jnp.sum(jnp.where(sel2, before, 0.0), axis=0, keepdims=True).astype(jnp.int32)
    total = carry + jnp.sum(member, axis=1, keepdims=True)
    carry_ref[...] = jnp.broadcast_to(total, carry_ref.shape)
    cnt_ref[...] = jnp.broadcast_to(total, cnt_ref.shape)

    ri_ref[...] = jnp.where(sub == 0, e1, jnp.where(sub == 1, e2, jnp.where(sub == 2, r1, r2)))
    rf_ref[...] = jnp.where(sub == 0, gate1, gate2)


def _out_proj(a2, p2, x2, wa, wp, g1, g2, w_router=None, dense=None):
    t = x2.shape[0]
    tm = TOKEN_TILE
    full = lambda a: pl.BlockSpec(a.shape, lambda i: (0,) * a.ndim)
    resident = lambda a: pl.BlockSpec(a.shape, lambda i: (0,) * a.ndim, pipeline_mode=pl.Buffered(1))
    row = lambda w: pl.BlockSpec((tm, w), lambda i: (i, 0))
    in_specs = [row(a2.shape[1]), row(p2.shape[1]), row(D_MODEL), full(wa), full(wp), full(g1), full(g2)]
    if w_router is None:
        wg, wu, wd, g3 = dense
        return pl.pallas_call(
            _out_dense_kernel,
            grid=(t // tm,),
            in_specs=in_specs + [resident(wg), resident(wu), resident(wd), full(g3)],
            out_specs=row(D_MODEL),
            out_shape=jax.ShapeDtypeStruct((t, D_MODEL), _F32),
            compiler_params=_params(("arbitrary",), 56),
            name="out_proj_dense_ffn",
        )(a2, p2, x2, wa, wp, g1, g2, wg, wu, wd, g3)
    return pl.pallas_call(
        functools.partial(_out_moe_kernel, tm=tm),
        grid=(t // tm,),
        in_specs=in_specs + [full(w_router)],
        out_specs=[row(D_MODEL),
                   pl.BlockSpec((tm * ROW_SPLIT, LANES), lambda i: (i, 0)),
                   pl.BlockSpec((N_EXPERTS, tm), lambda i: (0, i)),
                   pl.BlockSpec((N_EXPERTS, tm), lambda i: (0, i)),
                   pl.BlockSpec((N_EXPERTS, LANES), lambda i: (0, 0))],
        out_shape=[jax.ShapeDtypeStruct((t, D_MODEL), _F32),
                   jax.ShapeDtypeStruct((t * ROW_SPLIT, LANES), _F32),
                   jax.ShapeDtypeStruct((N_EXPERTS, t), jnp.int32),
                   jax.ShapeDtypeStruct((N_EXPERTS, t), _F32),
                   jax.ShapeDtypeStruct((N_EXPERTS, LANES), _F32)],
        scratch_shapes=[pltpu.VMEM((N_EXPERTS, LANES), _F32)],
        compiler_params=_params(("arbitrary",), 48),
        name="out_proj_router",
    )(a2, p2, x2, wa, wp, g1, g2, w_router)


def _expert_kernel(te_ref, ch_ref, nv_ref, src_ref, h_hbm, wg_ref, wu_ref, wd_ref, y_ref,
                   buf_ref, xb_ref, acc_ref, sem, *, tm, n_tiles, n_steps):
    i = pl.program_id(0)
    c = pl.program_id(1)
    last_c = n_steps - 1
    nv = nv_ref[0]
    rows = tm * ROW_SPLIT

    def start_row(tile_base, slot, r):
        tok = src_ref[tile_base + r]
        pltpu.make_async_copy(
            h_hbm.at[pl.ds(pl.multiple_of(tok * ROW_SPLIT, ROW_SPLIT), ROW_SPLIT), :],
            buf_ref.at[slot, pl.ds(pl.multiple_of(r * ROW_SPLIT, ROW_SPLIT), ROW_SPLIT), :],
            sem.at[slot]).start()

    def wait_slot(slot):
        pltpu.make_async_copy(h_hbm.at[pl.ds(0, rows), :], buf_ref.at[slot], sem.at[slot]).wait()

    @pl.when(jnp.logical_and(i == 0, c == 0))
    def _():
        def body(r, carry):
            start_row(0, 0, r)
            return carry

        lax.fori_loop(0, tm, body, 0)

    @pl.when(jnp.logical_and(c == 0, i == nv))
    def _():
        wait_slot(i % 2)

    def swiglu_part(lo=0, hi=None):
        x = xb_ref[...]
        part = None
        for k0 in range(lo, wg_ref.shape[1] if hi is None else hi, EXPERT_FF_SUB):
            k1 = k0 + EXPERT_FF_SUB
            gate = jnp.dot(x, wg_ref[:, k0:k1], preferred_element_type=_F32)
            up = jnp.dot(x, wu_ref[:, k0:k1], preferred_element_type=_F32)
            hm = _silu_mul(gate, up).astype(_BF16)
            down = jnp.dot(hm, wd_ref[k0:k1, :], preferred_element_type=_F32)
            part = down if part is None else part + down
        return part

    def take_rows_and_look_ahead():
        slot = i % 2
        wait_slot(slot)
        for s in range(ROW_SPLIT):
            xb_ref[:, s * LANES:(s + 1) * LANES] = (
                buf_ref[slot, pl.ds(s, tm, stride=ROW_SPLIT), :].astype(_BF16))
        next_base = jnp.minimum(i + 1, n_tiles - 1) * tm
        for r in range(tm):
            start_row(next_base, 1 - slot, r)

    def store_rows(total):
        for s in range(ROW_SPLIT):
            y_ref[pl.ds(s, tm, stride=ROW_SPLIT), :] = total[:, s * LANES:(s + 1) * LANES]

    if n_steps == 1:
        half = wg_ref.shape[1] // 2

        @pl.when(i < nv)
        def _():
            take_rows_and_look_ahead()
            acc_ref[...] = swiglu_part(0, half)

        @pl.when(jnp.logical_and(i < nv, c == last_c))
        def _():
            store_rows(acc_ref[...] + swiglu_part(half))
    else:

        @pl.when(jnp.logical_and(i < nv, c == 0))
        def _():
            take_rows_and_look_ahead()
            acc_ref[...] = swiglu_part()

        if n_steps > 2:
            @pl.when(jnp.logical_and(i < nv, jnp.logical_and(c > 0, c < last_c)))
            def _():
                acc_ref[...] += swiglu_part()

        @pl.when(jnp.logical_and(i < nv, c == last_c))
        def _():
            store_rows(acc_ref[...] + swiglu_part())

    @pl.when(jnp.logical_and(c == last_c, i >= nv))
    def _():
        y_ref[...] = jnp.zeros_like(y_ref)

    @pl.when(jnp.logical_and(jnp.logical_and(i == n_tiles - 1, c == last_c), nv == n_tiles))
    def _():
        wait_slot(n_tiles % 2)


def _expert_ffn(tile_expert, step_chunk, n_valid, src_tok, h_rows, wg, wu, wd):
    tm = EXPERT_TILE
    n_tiles = tile_expert.shape[0]
    steps = EXPERT_FF_STEPS
    fc = wg.shape[2] // steps

    col_map = lambda i, c, te, ch, nv, src: (te[i], 0, ch[i * steps + c])
    row_map = lambda i, c, te, ch, nv, src: (te[i], ch[i * steps + c], 0)
    grid_spec = pltpu.PrefetchScalarGridSpec(
        num_scalar_prefetch=4,
        grid=(n_tiles, steps),
        in_specs=[pl.BlockSpec(memory_space=pl.ANY),
                  pl.BlockSpec((None, D_MODEL, fc), col_map),
                  pl.BlockSpec((None, D_MODEL, fc), col_map),
                  pl.BlockSpec((None, fc, D_MODEL), row_map)],
        out_specs=pl.BlockSpec((tm * ROW_SPLIT, LANES), lambda i, c, te, ch, nv, src: (i, 0)),
        scratch_shapes=[pltpu.VMEM((2, tm * ROW_SPLIT, LANES), _F32),
                        pltpu.VMEM((tm, D_MODEL), _BF16),
                        pltpu.VMEM((tm, D_MODEL), _F32),
                        pltpu.SemaphoreType.DMA((2,))],
    )
    return pl.pallas_call(
        functools.partial(_expert_kernel, tm=tm, n_tiles=n_tiles, n_steps=steps),
        grid_spec=grid_spec,
        out_shape=jax.ShapeDtypeStruct((n_tiles * tm * ROW_SPLIT, LANES), _F32),
        compiler_params=_params(("arbitrary", "arbitrary"), 60),
        name="expert_ffn",
    )(tile_expert, step_chunk, n_valid, src_tok, h_rows, wg, wu, wd)


def _combine_kernel(p1_ref, p2_ref, y_hbm, x_ref, rf_ref, g_ref, o_ref, buf_ref, sem, *, tm, n_tiles):
    i = pl.program_id(0)
    rows = tm * ROW_SPLIT

    def start_pair(tile_base, slot, r):
        if isinstance(r, int):
            dst = pl.ds(r * ROW_SPLIT, ROW_SPLIT)
        else:
            dst = pl.ds(pl.multiple_of(r * ROW_SPLIT, ROW_SPLIT), ROW_SPLIT)
        for k, pos_ref in enumerate((p1_ref, p2_ref)):
            pos = pos_ref[tile_base + r]
            pltpu.make_async_copy(
                y_hbm.at[pl.ds(pl.multiple_of(pos * ROW_SPLIT, ROW_SPLIT), ROW_SPLIT), :],
                buf_ref.at[slot, k, dst, :], sem.at[slot]).start(priority=k)

    def wait_slot(slot):
        for k in range(2):
            pltpu.make_async_copy(y_hbm.at[pl.ds(0, rows), :], buf_ref.at[slot, k], sem.at[slot]).wait()

    @pl.when(i == 0)
    def _():
        def body(r, carry):
            start_pair(0, 0, r)
            return carry

        lax.fori_loop(0, tm, body, 0)

    for slot in range(2):
        @pl.when(i % 2 == slot)
        def _():
            wait_slot(slot)
            next_base = jnp.minimum(i + 1, n_tiles - 1) * tm
            for r in range(tm):
                start_pair(next_base, 1 - slot, r)
            rf = rf_ref[...]
            g1 = rf[:, 0:1]
            g2 = rf[:, 1:2]
            parts = []
            for s in range(ROW_SPLIT):
                y1 = buf_ref[slot, 0, pl.ds(s, tm, stride=ROW_SPLIT), :]
                y2 = buf_ref[slot, 1, pl.ds(s, tm, stride=ROW_SPLIT), :]
                parts.append(g1 * y1 + g2 * y2)
            f = jnp.concatenate(parts, axis=1)
            o_ref[...] = x_ref[...] + _rms(f, g_ref[...])

    @pl.when(i == n_tiles - 1)
    def _():
        wait_slot(n_tiles % 2)


def _combine(pos1, pos2, y_rows, x2, route_f, g):
    t = x2.shape[0]
    tm = TOKEN_TILE
    grid_spec = pltpu.PrefetchScalarGridSpec(
        num_scalar_prefetch=2,
        grid=(t // tm,),
        in_specs=[pl.BlockSpec(memory_space=pl.ANY),
                  pl.BlockSpec((tm, D_MODEL), lambda i, p1, p2: (i, 0)),
                  pl.BlockSpec((tm, route_f.shape[1]), lambda i, p1, p2: (i, 0)),
                  pl.BlockSpec(g.shape, lambda i, p1, p2: (0, 0))],
        out_specs=pl.BlockSpec((tm, D_MODEL), lambda i, p1, p2: (i, 0)),
        scratch_shapes=[pltpu.VMEM((2, 2, tm * ROW_SPLIT, LANES), _F32),
                        pltpu.SemaphoreType.DMA((2,))],
    )
    return pl.pallas_call(
        functools.partial(_combine_kernel, tm=tm, n_tiles=t // tm),
        grid_spec=grid_spec,
        out_shape=jax.ShapeDtypeStruct((t, D_MODEL), _F32),
        compiler_params=_params(("arbitrary",), 40),
        name="combine",
    )(pos1, pos2, y_rows, x2, route_f, g)


def _swap_halves(w):
    half = w.shape[-1] // 2
    return jnp.concatenate([w[..., half:], w[..., :half]], axis=-1)


def _layer_weights(w_in, w_uq, w_ukv, pool_w, w_out):
    kr = w_in[:, _C_KR:_C_KR + ROPE_DIM]
    krs = _swap_halves(kr)
    win = jnp.concatenate([w_in[:, :_C_KR], kr, kr, krs, krs, w_in[:, _C_KR + ROPE_DIM:]], axis=1)
    q_rope = w_uq[:, :, NOPE_DIM:]
    wq = jnp.concatenate([w_uq[:, :, :NOPE_DIM], q_rope, _swap_halves(q_rope)], axis=2)
    wq = wq.reshape(Q_LORA, N_HEADS * HEAD_W)
    wkv = jnp.concatenate([w_ukv[:, :, :NOPE_DIM].reshape(KV_LORA, N_HEADS * NOPE_DIM),
                           w_ukv[:, :, NOPE_DIM:].reshape(KV_LORA, N_HEADS * V_DIM)], axis=1)
    pw = jnp.zeros((POOL_WIDTH, POOL_WIDTH), _F32)
    for gidx in range(len(POOL_WINDOWS)):
        lo = gidx * POOL_GROUP_DIM
        pw = pw.at[lo:lo + POOL_GROUP_DIM, lo:lo + POOL_GROUP_DIM].set(pool_w[gidx])
    wa = w_out[:N_HEADS * V_DIM]
    wp = w_out[N_HEADS * V_DIM:]
    return tuple(a.astype(_BF16) for a in (win, wq, wkv, pw, wa, wp))


def _rope_table(seq):
    inv = 1.0 / (ROPE_THETA ** (jnp.arange(0, ROPE_DIM, 2, dtype=_F32) / ROPE_DIM))
    ang = jnp.arange(seq, dtype=_F32)[:, None] * inv[None, :]
    cos, sin = jnp.cos(ang), jnp.sin(ang)
    cos2 = jnp.concatenate([cos, cos], axis=1)
    sin2 = jnp.concatenate([-sin, sin], axis=1)
    scale = (NOPE_DIM + ROPE_DIM) ** -0.5
    qmul = jnp.concatenate([jnp.full((seq, NOPE_DIM), scale, _F32), scale * cos2, scale * sin2], axis=1)
    return jnp.concatenate([qmul, cos2, cos2, sin2, sin2], axis=1)


def _routing_plan(route_i, counts, t):
    tm = EXPERT_TILE
    n_tiles = (2 * t) // tm + N_EXPERTS
    cnt = counts[:, 0].astype(jnp.int32)
    padded = ((cnt + tm - 1) // tm) * tm
    ends = jnp.cumsum(padded)
    base = ends - padded
    n_valid = (ends[-1] // tm).astype(jnp.int32)
    tile = jnp.arange(n_tiles, dtype=jnp.int32)
    tile_expert = jnp.minimum(jnp.sum((tile * tm)[:, None] >= ends[None, :], axis=1), N_EXPERTS - 1)
    first_tile = jnp.sum(jnp.where(tile_expert[:, None] == jnp.arange(N_EXPERTS)[None, :],
                                   (base // tm)[None, :], 0), axis=1)
    backward = ((tile - first_tile) % 2 == 1)[:, None]
    steps = jnp.arange(EXPERT_FF_STEPS, dtype=jnp.int32)[None, :]
    step_chunk = jnp.where(backward, EXPERT_FF_STEPS - 1 - steps, steps)
    used = (tile < n_valid)[:, None]
    last = n_valid - 1
    tile_expert = jnp.where(used[:, 0], tile_expert, tile_expert[last])
    step_chunk = jnp.where(used, step_chunk, step_chunk[last, EXPERT_FF_STEPS - 1])
    e1, e2, pos1, pos2 = (route_i[k] for k in range(4))
    for k in range(N_EXPERTS):
        pos1 = pos1 + jnp.where(e1 == k, base[k], 0)
        pos2 = pos2 + jnp.where(e2 == k, base[k], 0)
    src = _invert_positions(pos1, pos2, n_tiles * tm)
    return (tile_expert.astype(jnp.int32), step_chunk.reshape(-1).astype(jnp.int32), n_valid.reshape(1),
            src, pos1, pos2)


def _invert_kernel(p1_ref, p2_ref, zeros_hbm, src_ref, *, unroll):
    pltpu.sync_copy(zeros_hbm, src_ref)

    def fill(j, carry):
        for u in range(unroll):
            tok = j * unroll + u
            src_ref[p1_ref[tok]] = tok
            src_ref[p2_ref[tok]] = tok
        return carry

    lax.fori_loop(0, p1_ref.shape[0] // unroll, fill, 0)


def _invert_positions(pos1, pos2, n_slots):
    unroll = 8
    assert n_slots % unroll == 0 and pos1.shape[0] % unroll == 0
    smem = pl.BlockSpec(memory_space=pltpu.SMEM)
    return pl.pallas_call(
        functools.partial(_invert_kernel, unroll=unroll),
        in_specs=[smem, smem, pl.BlockSpec(memory_space=pl.ANY)],
        out_specs=smem,
        out_shape=jax.ShapeDtypeStruct((n_slots,), jnp.int32),
        name="invert_positions",
    )(pos1, pos2, jnp.zeros((n_slots,), jnp.int32))


def kernel(x, attn_pre_g, attn_post_g, ffn_pre_g, ffn_post_g, w_in, q_norm_g, kv_norm_g, w_uq, w_ukv,
           pool_w, pool_scale, w_out, w_gate_d, w_up_d, w_down_d, w_router, w_gate_e, w_up_e, w_down_e):
    b, seq, d = x.shape
    t = b * seq
    depth = w_in.shape[0]
    assert d == D_MODEL and seq % TOKEN_TILE == 0 and seq % ATTN_TILE == 0
    tab = _rope_table(seq)
    row = lambda a: a.reshape(1, -1)
    x2 = x.reshape(t, d)
    for l in range(depth):
        win, wq, wkv, pw, wa, wp = _layer_weights(w_in[l], w_uq[l], w_ukv[l], pool_w[l], w_out[l])
        q, k, v, p = _in_proj(x2, row(attn_pre_g[l]), win, row(q_norm_g[l]), row(kv_norm_g[l]),
                              wq, wkv, tab, pw, row(pool_scale[l]), seq=seq)
        i = l // 2
        if l % 2 == 1:
            casts = ((w_up_e[i].reshape(-1, w_up_e.shape[-1]),) + CAST_HEADS_WIDE,
                     (w_down_e[i].reshape(-1, d),) + CAST_HEADS_REST)
        else:
            casts = tuple((w,) + CAST_HEADS_REST for w in (w_gate_d[i], w_up_d[i], w_down_d[i]))
            if l + 1 < depth:
                casts += ((w_gate_e[(l + 1) // 2].reshape(-1, w_gate_e.shape[-1]),) + CAST_HEADS_WIDE,)
        a, converted = _attention(q.reshape(b, seq, -1), k.reshape(b, seq, -1), v.reshape(b, seq, -1), casts)
        a2 = a.reshape(t, -1)
        if l % 2 == 0:
            x2 = _out_proj(a2, p, x2, wa, wp, row(attn_post_g[l]), row(ffn_pre_g[l]),
                           dense=tuple(converted[0:3]) + (row(ffn_post_g[l]),))
            next_gate = converted[3].reshape(w_gate_e.shape[1:]) if len(converted) > 3 else None
        else:
            wr_t = w_router[i].T
            wr_hi = wr_t.astype(_BF16)
            wr = jnp.concatenate([wr_hi, (wr_t - wr_hi.astype(_F32)).astype(_BF16)], axis=0)
            x2, h_rows, route_i, route_f, counts = _out_proj(
                a2, p, x2, wa, wp, row(attn_post_g[l]), row(ffn_pre_g[l]), wr)
            tile_expert, step_chunk, n_valid, src, pos1, pos2 = _routing_plan(route_i, counts, t)
            y_rows = _expert_ffn(tile_expert, step_chunk, n_valid, src, h_rows, next_gate,
                                 converted[0].reshape(w_up_e.shape[1:]),
                                 converted[1].reshape(w_down_e.shape[1:]))
            x2 = _combine(pos1, pos2, y_rows, x2, route_f[0:2].T, row(ffn_post_g[l]))
    return x2.reshape(b, seq, d)
```

```python
import functools

import jax
import jax.numpy as jnp
from jax import lax
from jax.experimental import pallas as pl
from jax.experimental.pallas import tpu as pltpu

D_MODEL = 1024
CHUNK = 64
EPS = 1e-6
N_HEADS = 6
NOPE_DIM = 128
ROPE_DIM = 64
V_DIM = 128
Q_LORA = 384
KV_LORA = 256
ROPE_THETA = 10000.0
POOL_WINDOWS = (2, 4, 8, 16)
POOL_GROUP_DIM = 64
POOL_WIDTH = len(POOL_WINDOWS) * POOL_GROUP_DIM
N_EXPERTS = 8
MAX_WINDOW = max(POOL_WINDOWS)

LANES = 128
SUBLANES = 8
HEAD_W = 2 * LANES
ROW_SPLIT = D_MODEL // LANES

_C_Q = 0
_C_KV = Q_LORA
_C_KR = Q_LORA + KV_LORA
_C_KRS = _C_KR + LANES
_C_POOL = _C_KRS + LANES
IN_AUG = _C_POOL + POOL_WIDTH

TOKEN_TILE = 512
IN_SUB_ROWS = 256
ATTN_TILE = 256
EXPERT_TILE = 512
EXPERT_FF_STEPS = 1
EXPERT_FF_SUB = 256
CAST_HEADS_WIDE = (0, 4)
CAST_HEADS_REST = (4, 6)

_F32 = jnp.float32
_BF16 = jnp.bfloat16
_MIB = 1024 * 1024


def _rms(x, g):
    return x * lax.rsqrt(jnp.mean(x * x, axis=-1, keepdims=True) + EPS) * g


def _silu_mul(g, u):
    hg = 0.5 * g
    return (hg + hg * jnp.tanh(hg)) * u


def _params(sem, vmem_mib):
    return pltpu.CompilerParams(dimension_semantics=sem, vmem_limit_bytes=vmem_mib * _MIB)


def _in_kernel(x_ref, g_ref, win_ref, qg_ref, kvg_ref, wq_ref, wkv_ref, tab_ref, pw_ref, ps_ref,
               q_ref, k_ref, v_ref, p_ref, ext_ref, *, tm, tiles_per_seq):
    i = pl.program_id(0)

    @pl.when(i % tiles_per_seq == 0)
    def _():
        ext_ref[0:MAX_WINDOW, :] = jnp.zeros((MAX_WINDOW, POOL_WIDTH), _F32)

    for r0 in range(0, tm, IN_SUB_ROWS):
        rows = slice(r0, r0 + IN_SUB_ROWS)
        h = _rms(x_ref[rows, :], g_ref[...]).astype(_BF16)
        z = jnp.dot(h, win_ref[...], preferred_element_type=_F32)
        cq = _rms(z[:, _C_Q:_C_Q + Q_LORA], qg_ref[...]).astype(_BF16)
        ckv = _rms(z[:, _C_KV:_C_KV + KV_LORA], kvg_ref[...]).astype(_BF16)
        q = jnp.dot(cq, wq_ref[...], preferred_element_type=_F32)
        kv = jnp.dot(ckv, wkv_ref[...], preferred_element_type=_F32)
        tab = tab_ref[rows, :]
        qmul = tab[:, 0:HEAD_W]
        cc = tab[:, HEAD_W:HEAD_W + LANES]
        ss = tab[:, HEAD_W + LANES:HEAD_W + 2 * LANES]
        krr = (z[:, _C_KR:_C_KR + LANES] * cc + z[:, _C_KRS:_C_KRS + LANES] * ss).astype(_BF16)
        for hd in range(N_HEADS):
            q_ref[rows, hd * HEAD_W:(hd + 1) * HEAD_W] = (q[:, hd * HEAD_W:(hd + 1) * HEAD_W] * qmul).astype(_BF16)
            k_ref[rows, hd * HEAD_W:hd * HEAD_W + NOPE_DIM] = kv[:, hd * NOPE_DIM:(hd + 1) * NOPE_DIM].astype(_BF16)
            k_ref[rows, hd * HEAD_W + NOPE_DIM:(hd + 1) * HEAD_W] = krr
        v_ref[rows, :] = kv[:, N_HEADS * NOPE_DIM:].astype(_BF16)
        ext_ref[MAX_WINDOW + r0:MAX_WINDOW + r0 + IN_SUB_ROWS, :] = z[:, _C_POOL:_C_POOL + POOL_WIDTH]

    u = ext_ref[MAX_WINDOW:MAX_WINDOW + tm, :]

    def window_sum(k0, k1, lo):
        acc = ext_ref[MAX_WINDOW - k0:MAX_WINDOW - k0 + tm, lo:lo + LANES]
        for k in range(k0 + 1, k1):
            acc = acc + ext_ref[MAX_WINDOW - k:MAX_WINDOW - k + tm, lo:lo + LANES]
        return acc

    t_pos = (i % tiles_per_seq) * tm + lax.broadcasted_iota(jnp.int32, (tm, LANES), 0)
    cnt = (t_pos + 1).astype(_F32)
    first_group = lax.broadcasted_iota(jnp.int32, (tm, LANES), 1) < POOL_GROUP_DIM
    w0, w1, w2, w3 = POOL_WINDOWS
    s_a = window_sum(0, w0, 0)
    s_b = s_a + window_sum(w0, w1, 0)
    pooled_lo = jnp.where(first_group, s_a / jnp.minimum(cnt, float(w0)), s_b / jnp.minimum(cnt, float(w1)))
    s_c = window_sum(0, w2, LANES)
    s_d = s_c + window_sum(w2, w3, LANES)
    pooled_hi = jnp.where(first_group, s_c / jnp.minimum(cnt, float(w2)), s_d / jnp.minimum(cnt, float(w3)))
    d = jnp.concatenate([pooled_lo - u[:, :LANES], pooled_hi - u[:, LANES:]], axis=1).astype(_BF16)
    y = jnp.dot(d, pw_ref[...], preferred_element_type=_F32) * ps_ref[...]
    p_ref[...] = y.astype(_BF16)
    ext_ref[0:MAX_WINDOW, :] = ext_ref[tm:tm + MAX_WINDOW, :]


def _of_layer(a, layer):
    return pl.BlockSpec((None,) + a.shape[1:], lambda i: (layer,) + (0,) * (a.ndim - 1))


def _in_proj(x2, g, win, qg, kvg, wq, wkv, tab, pw, ps, *, layer, seq):
    t = x2.shape[0]
    tm = TOKEN_TILE
    tps = seq // tm
    full = lambda a: _of_layer(a, layer)
    return pl.pallas_call(
        functools.partial(_in_kernel, tm=tm, tiles_per_seq=tps),
        grid=(t // tm,),
        in_specs=[pl.BlockSpec((tm, D_MODEL), lambda i: (i, 0)),
                  full(g), full(win), full(qg), full(kvg), full(wq), full(wkv),
                  pl.BlockSpec((tm, tab.shape[1]), lambda i: (i % tps, 0)),
                  full(pw), full(ps)],
        out_specs=[pl.BlockSpec((tm, N_HEADS * HEAD_W), lambda i: (i, 0)),
                   pl.BlockSpec((tm, N_HEADS * HEAD_W), lambda i: (i, 0)),
                   pl.BlockSpec((tm, N_HEADS * V_DIM), lambda i: (i, 0)),
                   pl.BlockSpec((tm, POOL_WIDTH), lambda i: (i, 0))],
        out_shape=[jax.ShapeDtypeStruct((t, N_HEADS * HEAD_W), _BF16),
                   jax.ShapeDtypeStruct((t, N_HEADS * HEAD_W), _BF16),
                   jax.ShapeDtypeStruct((t, N_HEADS * V_DIM), _BF16),
                   jax.ShapeDtypeStruct((t, POOL_WIDTH), _BF16)],
        scratch_shapes=[pltpu.VMEM((tm + MAX_WINDOW, POOL_WIDTH), _F32)],
        compiler_params=_params(("arbitrary",), 48),
        name="in_proj",
    )(x2, g, win, qg, kvg, wq, wkv, tab, pw, ps)


def _attn_kernel(*refs, tq, nq, cast_heads):
    n_cast = len(cast_heads)
    q_ref, k_ref, v_ref = refs[0:3]
    o_ref = refs[3 + n_cast]
    vaug_ref = refs[-1]
    head = pl.program_id(1)
    for (lo, hi), src_ref, dst_ref in zip(cast_heads, refs[3:3 + n_cast], refs[4 + n_cast:4 + 2 * n_cast]):
        @pl.when(jnp.logical_and(head >= lo, head < hi))
        def _():
            dst_ref[...] = src_ref[...].astype(_BF16)

    seq = v_ref.shape[0]
    vaug_ref[:, 0:V_DIM] = v_ref[...]
    ones_col = lax.broadcasted_iota(jnp.int32, (seq, LANES), 1) == 0
    vaug_ref[:, V_DIM:V_DIM + LANES] = jnp.where(ones_col, 1.0, 0.0).astype(_BF16)
    row_chunk = lax.broadcasted_iota(jnp.int32, (tq, tq), 0) // CHUNK
    col_chunk = lax.broadcasted_iota(jnp.int32, (tq, tq), 1) // CHUNK
    visible = col_chunk <= row_chunk
    nt = (((1,), (1,)), ((), ()))
    def scores(i):
        d0 = i * tq
        s = lax.dot_general(q_ref[d0:d0 + tq, :], k_ref[0:d0 + tq, :], nt, preferred_element_type=_F32)
        s_diag = jnp.where(visible, s[:, d0:], -jnp.inf)
        return s_diag if i == 0 else jnp.concatenate([s[:, :d0], s_diag], axis=1)

    order = list(range(nq - 1, -1, -1))
    s_next = scores(order[0])
    for n, i in enumerate(order):
        s = s_next
        d0 = i * tq
        p = jnp.exp2(s - jnp.max(s, axis=1, keepdims=True)).astype(_BF16)
        if n + 1 < nq:
            s_next = scores(order[n + 1])
        o = jnp.dot(p, vaug_ref[0:d0 + tq, :], preferred_element_type=_F32)
        o_ref[d0:d0 + tq, :] = (o[:, 0:V_DIM] / o[:, V_DIM:V_DIM + 1]).astype(_BF16)


def _attention(q, k, v, casts=()):
    b, seq, _ = q.shape
    tq = ATTN_TILE
    head = lambda w: pl.BlockSpec((None, seq, w), lambda bb, hh: (bb, 0, hh))
    cast_specs, cast_shapes = [], []
    for w, lo, hi in casts:
        n_h = hi - lo
        rows, cols = w.shape
        assert rows % (b * n_h * 2 * SUBLANES) == 0
        block_of = lambda bb, hh, lo=lo, n_h=n_h: (bb * n_h + jnp.clip(hh - lo, 0, n_h - 1), 0)
        cast_specs.append(pl.BlockSpec((rows // (b * n_h), cols), block_of))
        cast_shapes.append(jax.ShapeDtypeStruct(w.shape, _BF16))
    outs = pl.pallas_call(
        functools.partial(_attn_kernel, tq=tq, nq=seq // tq, cast_heads=tuple((lo, hi) for _, lo, hi in casts)),
        grid=(b, N_HEADS),
        in_specs=[head(HEAD_W), head(HEAD_W), head(V_DIM)] + cast_specs,
        out_specs=[head(V_DIM)] + cast_specs,
        out_shape=[jax.ShapeDtypeStruct((b, seq, N_HEADS * V_DIM), _BF16)] + cast_shapes,
        scratch_shapes=[pltpu.VMEM((seq, V_DIM + LANES), _BF16)],
        compiler_params=_params(("arbitrary", "arbitrary"), 56),
        name="attention",
    )(q, k, v, *(w for w, _, _ in casts))
    return outs[0], outs[1:]


def _mix_residual(a_ref, p_ref, x_ref, wa_ref, wp_ref, g1_ref):
    mix = (jnp.dot(a_ref[...], wa_ref[...], preferred_element_type=_F32)
           + jnp.dot(p_ref[...], wp_ref[...], preferred_element_type=_F32))
    return x_ref[...] + _rms(mix, g1_ref[...])


def _out_dense_kernel(a_ref, p_ref, x_ref, wa_ref, wp_ref, g1_ref, g2_ref,
                      wg_ref, wu_ref, wd_ref, g3_ref, o_ref):
    xn = _mix_residual(a_ref, p_ref, x_ref, wa_ref, wp_ref, g1_ref)
    h = _rms(xn, g2_ref[...]).astype(_BF16)
    gate = jnp.dot(h, wg_ref[...], preferred_element_type=_F32)
    up = jnp.dot(h, wu_ref[...], preferred_element_type=_F32)
    hm = _silu_mul(gate, up).astype(_BF16)
    f = jnp.dot(hm, wd_ref[...], preferred_element_type=_F32)
    o_ref[...] = xn + _rms(f, g3_ref[...])


def _out_moe_kernel(a_ref, p_ref, x_ref, wa_ref, wp_ref, g1_ref, g2_ref, wr_ref,
                    xo_ref, ho_ref, ri_ref, rf_ref, cnt_ref, carry_ref, *, tm):
    @pl.when(pl.program_id(0) == 0)
    def _():
        carry_ref[...] = jnp.zeros_like(carry_ref)

    xn = _mix_residual(a_ref, p_ref, x_ref, wa_ref, wp_ref, g1_ref)
    xo_ref[...] = xn
    h = _rms(xn, g2_ref[...])
    for s in range(ROW_SPLIT):
        ho_ref[pl.ds(s, tm, stride=ROW_SPLIT), :] = h[:, s * LANES:(s + 1) * LANES]

    h_hi = h.astype(_BF16)
    h_lo = (h - h_hi.astype(_F32)).astype(_BF16)
    wr = wr_ref[...]
    nt = (((1,), (1,)), ((), ()))
    by_hi = lax.dot_general(wr, h_hi, nt, preferred_element_type=_F32)
    by_lo = lax.dot_general(wr, h_lo, nt, preferred_element_type=_F32)
    logits = by_hi[0:N_EXPERTS] + by_hi[N_EXPERTS:] + by_lo[0:N_EXPERTS]
    sub = lax.broadcasted_iota(jnp.int32, (N_EXPERTS, tm), 0)
    v1 = jnp.max(logits, axis=0, keepdims=True)
    e1 = jnp.min(jnp.where(logits == v1, sub, N_EXPERTS), axis=0, keepdims=True)
    rest = jnp.where(sub == e1, -jnp.inf, logits)
    v2 = jnp.max(rest, axis=0, keepdims=True)
    e2 = jnp.min(jnp.where(rest == v2, sub, N_EXPERTS), axis=0, keepdims=True)
    ex = jnp.exp(v2 - v1)
    gate1 = 1.0 / (1.0 + ex)
    gate2 = ex * gate1

    sel1 = sub == e1
    sel2 = sub == e2
    member = jnp.logical_or(sel1, sel2).astype(_F32)
    earlier = (lax.broadcasted_iota(jnp.int32, (tm, tm), 0)
               < lax.broadcasted_iota(jnp.int32, (tm, tm), 1)).astype(_BF16)
    member_lhs = jnp.concatenate([member, jnp.zeros_like(member)], axis=0).astype(_BF16)
    carry = carry_ref[:, 0:1]
    before = jnp.dot(member_lhs, earlier, preferred_element_type=_F32)[0:N_EXPERTS] + carry
    r1 = jnp.sum(jnp.where(sel1, before, 0.0), axis=0, keepdims=True).astype(jnp.int32)
    r2 = jnp.sum(jnp.where(sel2, before, 0.0), axis=0, keepdims=True).astype(jnp.int32)
    total = carry + jnp.sum(member, axis=1, keepdims=True)
    carry_ref[...] = jnp.broadcast_to(total, carry_ref.shape)
    cnt_ref[...] = jnp.broadcast_to(total, cnt_ref.shape)

    ri_ref[...] = jnp.where(sub == 0, e1, jnp.where(sub == 1, e2, jnp.where(sub == 2, r1, r2)))
    rf_ref[...] = jnp.where(sub == 0, gate1, gate2)


def _out_proj(a2, p2, x2, wo, g1, g2, *, layer, w_router=None, dense=None):
    t = x2.shape[0]
    tm = TOKEN_TILE
    full = lambda a: pl.BlockSpec(a.shape, lambda i: (0,) * a.ndim)
    of_layer = lambda a: _of_layer(a, layer)
    attn_w, pool_w = a2.shape[1], p2.shape[1]
    assert attn_w % pool_w == 0
    wa_spec = pl.BlockSpec((None, attn_w, D_MODEL), lambda i: (layer, 0, 0))
    wp_spec = pl.BlockSpec((None, pool_w, D_MODEL), lambda i: (layer, attn_w // pool_w, 0))
    resident = lambda a: pl.BlockSpec(a.shape, lambda i: (0,) * a.ndim, pipeline_mode=pl.Buffered(1))
    row = lambda w: pl.BlockSpec((tm, w), lambda i: (i, 0))
    in_specs = [row(attn_w), row(pool_w), row(D_MODEL), wa_spec, wp_spec, of_layer(g1), of_layer(g2)]
    if w_router is None:
        wg, wu, wd, g3 = dense
        return pl.pallas_call(
            _out_dense_kernel,
            grid=(t // tm,),
            in_specs=in_specs + [resident(wg), resident(wu), resident(wd), of_layer(g3)],
            out_specs=row(D_MODEL),
            out_shape=jax.ShapeDtypeStruct((t, D_MODEL), _F32),
            compiler_params=_params(("arbitrary",), 56),
            name="out_proj_dense_ffn",
        )(a2, p2, x2, wo, wo, g1, g2, wg, wu, wd, g3)
    return pl.pallas_call(
        functools.partial(_out_moe_kernel, tm=tm),
        grid=(t // tm,),
        in_specs=in_specs + [full(w_router)],
        out_specs=[row(D_MODEL),
                   pl.BlockSpec((tm * ROW_SPLIT, LANES), lambda i: (i, 0)),
                   pl.BlockSpec((N_EXPERTS, tm), lambda i: (0, i)),
                   pl.BlockSpec((N_EXPERTS, tm), lambda i: (0, i)),
                   pl.BlockSpec((N_EXPERTS, LANES), lambda i: (0, 0))],
        out_shape=[jax.ShapeDtypeStruct((t, D_MODEL), _F32),
                   jax.ShapeDtypeStruct((t * ROW_SPLIT, LANES), _F32),
                   jax.ShapeDtypeStruct((N_EXPERTS, t), jnp.int32),
                   jax.ShapeDtypeStruct((N_EXPERTS, t), _F32),
                   jax.ShapeDtypeStruct((N_EXPERTS, LANES), _F32)],
        scratch_shapes=[pltpu.VMEM((N_EXPERTS, LANES), _F32)],
        compiler_params=_params(("arbitrary",), 48),
        name="out_proj_router",
    )(a2, p2, x2, wo, wo, g1, g2, w_router)


def _expert_kernel(te_ref, ch_ref, nv_ref, src_ref, h_hbm, wg_ref, wu_ref, wd_ref, y_ref,
                   buf_ref, xb_ref, acc_ref, sem, *, tm, n_tiles, n_steps):
    i = pl.program_id(0)
    c = pl.program_id(1)
    last_c = n_steps - 1
    nv = nv_ref[0]
    rows = tm * ROW_SPLIT

    def start_row(tile_base, slot, r):
        tok = src_ref[tile_base + r]
        pltpu.make_async_copy(
            h_hbm.at[pl.ds(pl.multiple_of(tok * ROW_SPLIT, ROW_SPLIT), ROW_SPLIT), :],
            buf_ref.at[slot, pl.ds(pl.multiple_of(r * ROW_SPLIT, ROW_SPLIT), ROW_SPLIT), :],
            sem.at[slot]).start()

    def wait_slot(slot):
        pltpu.make_async_copy(h_hbm.at[pl.ds(0, rows), :], buf_ref.at[slot], sem.at[slot]).wait()

    @pl.when(jnp.logical_and(i == 0, c == 0))
    def _():
        def body(r, carry):
            start_row(0, 0, r)
            return carry

        lax.fori_loop(0, tm, body, 0)

    @pl.when(jnp.logical_and(c == 0, i == nv))
    def _():
        wait_slot(i % 2)

    def swiglu_part(lo=0, hi=None):
        x = xb_ref[...]
        part = None
        for k0 in range(lo, wg_ref.shape[1] if hi is None else hi, EXPERT_FF_SUB):
            k1 = k0 + EXPERT_FF_SUB
            gate = jnp.dot(x, wg_ref[:, k0:k1], preferred_element_type=_F32)
            up = jnp.dot(x, wu_ref[:, k0:k1], preferred_element_type=_F32)
            hm = _silu_mul(gate, up).astype(_BF16)
            down = jnp.dot(hm, wd_ref[k0:k1, :], preferred_element_type=_F32)
            part = down if part is None else part + down
        return part

    def take_rows_and_look_ahead():
        slot = i % 2
        wait_slot(slot)
        for s in range(ROW_SPLIT):
            xb_ref[:, s * LANES:(s + 1) * LANES] = (
                buf_ref[slot, pl.ds(s, tm, stride=ROW_SPLIT), :].astype(_BF16))
        next_base = jnp.minimum(i + 1, n_tiles - 1) * tm
        for r in range(tm):
            start_row(next_base, 1 - slot, r)

    def store_rows(total):
        for s in range(ROW_SPLIT):
            y_ref[pl.ds(s, tm, stride=ROW_SPLIT), :] = total[:, s * LANES:(s + 1) * LANES]

    if n_steps == 1:
        half = wg_ref.shape[1] // 2

        @pl.when(i < nv)
        def _():
            take_rows_and_look_ahead()
            acc_ref[...] = swiglu_part(0, half)

        @pl.when(jnp.logical_and(i < nv, c == last_c))
        def _():
            store_rows(acc_ref[...] + swiglu_part(half))
    else:

        @pl.when(jnp.logical_and(i < nv, c == 0))
        def _():
            take_rows_and_look_ahead()
            acc_ref[...] = swiglu_part()

        if n_steps > 2:
            @pl.when(jnp.logical_and(i < nv, jnp.logical_and(c > 0, c < last_c)))
            def _():
                acc_ref[...] += swiglu_part()

        @pl.when(jnp.logical_and(i < nv, c == last_c))
        def _():
            store_rows(acc_ref[...] + swiglu_part())

    @pl.when(jnp.logical_and(c == last_c, i >= nv))
    def _():
        y_ref[...] = jnp.zeros_like(y_ref)

    @pl.when(jnp.logical_and(jnp.logical_and(i == n_tiles - 1, c == last_c), nv == n_tiles))
    def _():
        wait_slot(n_tiles % 2)


def _expert_ffn(tile_expert, step_chunk, n_valid, src_tok, h_rows, wg, wu, wd):
    tm = EXPERT_TILE
    n_tiles = tile_expert.shape[0]
    steps = EXPERT_FF_STEPS
    fc = wg.shape[2] // steps

    col_map = lambda i, c, te, ch, nv, src: (te[i], 0, ch[i * steps + c])
    row_map = lambda i, c, te, ch, nv, src: (te[i], ch[i * steps + c], 0)
    grid_spec = pltpu.PrefetchScalarGridSpec(
        num_scalar_prefetch=4,
        grid=(n_tiles, steps),
        in_specs=[pl.BlockSpec(memory_space=pl.ANY),
                  pl.BlockSpec((None, D_MODEL, fc), col_map),
                  pl.BlockSpec((None, D_MODEL, fc), col_map),
                  pl.BlockSpec((None, fc, D_MODEL), row_map)],
        out_specs=pl.BlockSpec((tm * ROW_SPLIT, LANES), lambda i, c, te, ch, nv, src: (i, 0)),
        scratch_shapes=[pltpu.VMEM((2, tm * ROW_SPLIT, LANES), _F32),
                        pltpu.VMEM((tm, D_MODEL), _BF16),
                        pltpu.VMEM((tm, D_MODEL), _F32),
                        pltpu.SemaphoreType.DMA((2,))],
    )
    return pl.pallas_call(
        functools.partial(_expert_kernel, tm=tm, n_tiles=n_tiles, n_steps=steps),
        grid_spec=grid_spec,
        out_shape=jax.ShapeDtypeStruct((n_tiles * tm * ROW_SPLIT, LANES), _F32),
        compiler_params=_params(("arbitrary", "arbitrary"), 60),
        name="expert_ffn",
    )(tile_expert, step_chunk, n_valid, src_tok, h_rows, wg, wu, wd)


def _combine_kernel(p1_ref, p2_ref, y_hbm, x_ref, rf_ref, g_ref, o_ref, buf_ref, sem, *, tm, n_tiles):
    i = pl.program_id(0)
    rows = tm * ROW_SPLIT

    def start_pair(tile_base, slot, r):
        if isinstance(r, int):
            dst = pl.ds(r * ROW_SPLIT, ROW_SPLIT)
        else:
            dst = pl.ds(pl.multiple_of(r * ROW_SPLIT, ROW_SPLIT), ROW_SPLIT)
        for k, pos_ref in enumerate((p1_ref, p2_ref)):
            pos = pos_ref[tile_base + r]
            pltpu.make_async_copy(
                y_hbm.at[pl.ds(pl.multiple_of(pos * ROW_SPLIT, ROW_SPLIT), ROW_SPLIT), :],
                buf_ref.at[slot, k, dst, :], sem.at[slot]).start(priority=k)

    def wait_slot(slot):
        for k in range(2):
            pltpu.make_async_copy(y_hbm.at[pl.ds(0, rows), :], buf_ref.at[slot, k], sem.at[slot]).wait()

    @pl.when(i == 0)
    def _():
        def body(r, carry):
            start_pair(0, 0, r)
            return carry

        lax.fori_loop(0, tm, body, 0)

    for slot in range(2):
        @pl.when(i % 2 == slot)
        def _():
            wait_slot(slot)
            next_base = jnp.minimum(i + 1, n_tiles - 1) * tm
            for r in range(tm):
                start_pair(next_base, 1 - slot, r)
            rf = rf_ref[...]
            g1 = rf[:, 0:1]
            g2 = rf[:, 1:2]
            parts = []
            for s in range(ROW_SPLIT):
                y1 = buf_ref[slot, 0, pl.ds(s, tm, stride=ROW_SPLIT), :]
                y2 = buf_ref[slot, 1, pl.ds(s, tm, stride=ROW_SPLIT), :]
                parts.append(g1 * y1 + g2 * y2)
            f = jnp.concatenate(parts, axis=1)
            o_ref[...] = x_ref[...] + _rms(f, g_ref[...])

    @pl.when(i == n_tiles - 1)
    def _():
        wait_slot(n_tiles % 2)


def _combine(pos1, pos2, y_rows, x2, route_f, g):
    t = x2.shape[0]
    tm = TOKEN_TILE
    grid_spec = pltpu.PrefetchScalarGridSpec(
        num_scalar_prefetch=2,
        grid=(t // tm,),
        in_specs=[pl.BlockSpec(memory_space=pl.ANY),
                  pl.BlockSpec((tm, D_MODEL), lambda i, p1, p2: (i, 0)),
                  pl.BlockSpec((tm, route_f.shape[1]), lambda i, p1, p2: (i, 0)),
                  pl.BlockSpec(g.shape, lambda i, p1, p2: (0, 0))],
        out_specs=pl.BlockSpec((tm, D_MODEL), lambda i, p1, p2: (i, 0)),
        scratch_shapes=[pltpu.VMEM((2, 2, tm * ROW_SPLIT, LANES), _F32),
                        pltpu.SemaphoreType.DMA((2,))],
    )
    return pl.pallas_call(
        functools.partial(_combine_kernel, tm=tm, n_tiles=t // tm),
        grid_spec=grid_spec,
        out_shape=jax.ShapeDtypeStruct((t, D_MODEL), _F32),
        compiler_params=_params(("arbitrary",), 40),
        name="combine",
    )(pos1, pos2, y_rows, x2, route_f, g)


def _swap_halves(w):
    half = w.shape[-1] // 2
    return jnp.concatenate([w[..., half:], w[..., :half]], axis=-1)


def _layer_weights(w_in, w_uq, w_ukv, pool_w, w_out):
    kr = w_in[:, _C_KR:_C_KR + ROPE_DIM]
    krs = _swap_halves(kr)
    win = jnp.concatenate([w_in[:, :_C_KR], kr, kr, krs, krs, w_in[:, _C_KR + ROPE_DIM:]], axis=1)
    q_rope = w_uq[:, :, NOPE_DIM:]
    wq = jnp.concatenate([w_uq[:, :, :NOPE_DIM], q_rope, _swap_halves(q_rope)], axis=2)
    wq = wq.reshape(Q_LORA, N_HEADS * HEAD_W)
    wkv = jnp.concatenate([w_ukv[:, :, :NOPE_DIM].reshape(KV_LORA, N_HEADS * NOPE_DIM),
                           w_ukv[:, :, NOPE_DIM:].reshape(KV_LORA, N_HEADS * V_DIM)], axis=1)
    pw = jnp.zeros((POOL_WIDTH, POOL_WIDTH), _F32)
    for gidx in range(len(POOL_WINDOWS)):
        lo = gidx * POOL_GROUP_DIM
        pw = pw.at[lo:lo + POOL_GROUP_DIM, lo:lo + POOL_GROUP_DIM].set(pool_w[gidx])
    return tuple(a.astype(_BF16) for a in (win, wq, wkv, pw, w_out))


def _rope_table(seq):
    inv = 1.0 / (ROPE_THETA ** (jnp.arange(0, ROPE_DIM, 2, dtype=_F32) / ROPE_DIM))
    ang = jnp.arange(seq, dtype=_F32)[:, None] * inv[None, :]
    cos, sin = jnp.cos(ang), jnp.sin(ang)
    cos2 = jnp.concatenate([cos, cos], axis=1)
    sin2 = jnp.concatenate([-sin, sin], axis=1)
    scale = (NOPE_DIM + ROPE_DIM) ** -0.5 * 1.4426950408889634
    qmul = jnp.concatenate([jnp.full((seq, NOPE_DIM), scale, _F32), scale * cos2, scale * sin2], axis=1)
    return jnp.concatenate([qmul, cos2, cos2, sin2, sin2], axis=1)


def _routing_plan(route_i, counts, t):
    tm = EXPERT_TILE
    n_tiles = (2 * t) // tm + N_EXPERTS
    cnt = counts[:, 0].astype(jnp.int32)
    padded = ((cnt + tm - 1) // tm) * tm
    ends = jnp.cumsum(padded)
    base = ends - padded
    n_valid = (ends[-1] // tm).astype(jnp.int32)
    tile = jnp.arange(n_tiles, dtype=jnp.int32)
    tile_expert = jnp.minimum(jnp.sum((tile * tm)[:, None] >= ends[None, :], axis=1), N_EXPERTS - 1)
    first_tile = jnp.sum(jnp.where(tile_expert[:, None] == jnp.arange(N_EXPERTS)[None, :],
                                   (base // tm)[None, :], 0), axis=1)
    backward = ((tile - first_tile) % 2 == 1)[:, None]
    steps = jnp.arange(EXPERT_FF_STEPS, dtype=jnp.int32)[None, :]
    step_chunk = jnp.where(backward, EXPERT_FF_STEPS - 1 - steps, steps)
    used = (tile < n_valid)[:, None]
    last = n_valid - 1
    tile_expert = jnp.where(used[:, 0], tile_expert, tile_expert[last])
    step_chunk = jnp.where(used, step_chunk, step_chunk[last, EXPERT_FF_STEPS - 1])
    e1, e2, pos1, pos2 = (route_i[k] for k in range(4))
    for k in range(N_EXPERTS):
        pos1 = pos1 + jnp.where(e1 == k, base[k], 0)
        pos2 = pos2 + jnp.where(e2 == k, base[k], 0)
    src = _invert_positions(pos1, pos2, n_tiles * tm)
    return (tile_expert.astype(jnp.int32), step_chunk.reshape(-1).astype(jnp.int32), n_valid.reshape(1),
            src, pos1, pos2)


def _invert_kernel(p1_ref, p2_ref, zeros_hbm, src_ref, *, unroll):
    pltpu.sync_copy(zeros_hbm, src_ref)

    def fill(j, carry):
        for u in range(unroll):
            tok = j * unroll + u
            src_ref[p1_ref[tok]] = tok
            src_ref[p2_ref[tok]] = tok
        return carry

    lax.fori_loop(0, p1_ref.shape[0] // unroll, fill, 0)


def _invert_positions(pos1, pos2, n_slots):
    unroll = 8
    assert n_slots % unroll == 0 and pos1.shape[0] % unroll == 0
    smem = pl.BlockSpec(memory_space=pltpu.SMEM)
    return pl.pallas_call(
        functools.partial(_invert_kernel, unroll=unroll),
        in_specs=[smem, smem, pl.BlockSpec(memory_space=pl.ANY)],
        out_specs=smem,
        out_shape=jax.ShapeDtypeStruct((n_slots,), jnp.int32),
        name="invert_positions",
    )(pos1, pos2, jnp.zeros((n_slots,), jnp.int32))


def kernel(x, attn_pre_g, attn_post_g, ffn_pre_g, ffn_post_g, w_in, q_norm_g, kv_norm_g, w_uq, w_ukv,
           pool_w, pool_scale, w_out, w_gate_d, w_up_d, w_down_d, w_router, w_gate_e, w_up_e, w_down_e):
    b, seq, d = x.shape
    t = b * seq
    depth = w_in.shape[0]
    assert d == D_MODEL and seq % TOKEN_TILE == 0 and seq % ATTN_TILE == 0
    tab = _rope_table(seq)
    win, wq, wkv, pw, wo = jax.vmap(_layer_weights)(w_in, w_uq, w_ukv, pool_w, w_out)
    rows = lambda a: a.reshape(depth, 1, -1)
    attn_pre_g, attn_post_g, ffn_pre_g, ffn_post_g, q_norm_g, kv_norm_g, pool_scale = map(
        rows, (attn_pre_g, attn_post_g, ffn_pre_g, ffn_post_g, q_norm_g, kv_norm_g, pool_scale))
    x2 = x.reshape(t, d)
    for l in range(depth):
        q, k, v, p = _in_proj(x2, attn_pre_g, win, q_norm_g, kv_norm_g, wq, wkv, tab, pw, pool_scale,
                              layer=l, seq=seq)
        i = l // 2
        if l % 2 == 1:
            casts = ((w_up_e[i].reshape(-1, w_up_e.shape[-1]),) + CAST_HEADS_WIDE,
                     (w_down_e[i].reshape(-1, d),) + CAST_HEADS_REST)
        else:
            casts = tuple((w,) + CAST_HEADS_REST for w in (w_gate_d[i], w_up_d[i], w_down_d[i]))
            if l + 1 < depth:
                casts += ((w_gate_e[(l + 1) // 2].reshape(-1, w_gate_e.shape[-1]),) + CAST_HEADS_WIDE,)
        a, converted = _attention(q.reshape(b, seq, -1), k.reshape(b, seq, -1), v.reshape(b, seq, -1), casts)
        a2 = a.reshape(t, -1)
        if l % 2 == 0:
            x2 = _out_proj(a2, p, x2, wo, attn_post_g, ffn_pre_g, layer=l,
                           dense=tuple(converted[0:3]) + (ffn_post_g,))
            next_gate = converted[3].reshape(w_gate_e.shape[1:]) if len(converted) > 3 else None
        else:
            wr_t = w_router[i].T
            wr_hi = wr_t.astype(_BF16)
            wr = jnp.concatenate([wr_hi, (wr_t - wr_hi.astype(_F32)).astype(_BF16)], axis=0)
            x2, h_rows, route_i, route_f, counts = _out_proj(
                a2, p, x2, wo, attn_post_g, ffn_pre_g, layer=l, w_router=wr)
            tile_expert, step_chunk, n_valid, src, pos1, pos2 = _routing_plan(route_i, counts, t)
            y_rows = _expert_ffn(tile_expert, step_chunk, n_valid, src, h_rows, next_gate,
                                 converted[0].reshape(w_up_e.shape[1:]),
                                 converted[1].reshape(w_down_e.shape[1:]))
            x2 = _combine(pos1, pos2, y_rows, x2, route_f[0:2].T, ffn_post_g[l])
    return x2.reshape(b, seq, d)
```

```python
import functools

import jax
import jax.numpy as jnp
from jax import lax
from jax.experimental import pallas as pl
from jax.experimental.pallas import tpu as pltpu

D_MODEL = 1024
CHUNK = 64
EPS = 1e-6
N_HEADS = 6
NOPE_DIM = 128
ROPE_DIM = 64
V_DIM = 128
Q_LORA = 384
KV_LORA = 256
ROPE_THETA = 10000.0
POOL_WINDOWS = (2, 4, 8, 16)
POOL_GROUP_DIM = 64
POOL_WIDTH = len(POOL_WINDOWS) * POOL_GROUP_DIM
N_EXPERTS = 8
MAX_WINDOW = max(POOL_WINDOWS)

LANES = 128
SUBLANES = 8
HEAD_W = 2 * LANES
ROW_SPLIT = D_MODEL // LANES

_C_Q = 0
_C_KV = Q_LORA
_C_KR = Q_LORA + KV_LORA
_C_KRS = _C_KR + LANES
_C_POOL = _C_KRS + LANES
IN_AUG = _C_POOL + POOL_WIDTH

TOKEN_TILE = 512
IN_SUB_ROWS = 256
ATTN_TILE = 256
EXPERT_TILE = 512
EXPERT_FF_STEPS = 1
EXPERT_FF_SUB = 256
CAST_HEADS_WIDE = (0, 4)
CAST_HEADS_REST = (4, 6)

_F32 = jnp.float32
_BF16 = jnp.bfloat16
_MIB = 1024 * 1024


def _rms(x, g):
    return x * lax.rsqrt(jnp.mean(x * x, axis=-1, keepdims=True) + EPS) * g


def _silu_mul(g, u):
    hg = 0.5 * g
    return (hg + hg * jnp.tanh(hg)) * u


def _params(sem, vmem_mib):
    return pltpu.CompilerParams(dimension_semantics=sem, vmem_limit_bytes=vmem_mib * _MIB)


def _in_kernel(x_ref, g_ref, win_ref, qg_ref, kvg_ref, wq_ref, wkv_ref, tab_ref, pw_ref, ps_ref,
               q_ref, k_ref, v_ref, p_ref, ext_ref, *, tm, tiles_per_seq):
    i = pl.program_id(0)

    @pl.when(i % tiles_per_seq == 0)
    def _():
        ext_ref[0:MAX_WINDOW, :] = jnp.zeros((MAX_WINDOW, POOL_WIDTH), _F32)

    for r0 in range(0, tm, IN_SUB_ROWS):
        rows = slice(r0, r0 + IN_SUB_ROWS)
        h = _rms(x_ref[rows, :], g_ref[...]).astype(_BF16)
        z = jnp.dot(h, win_ref[...], preferred_element_type=_F32)
        cq = _rms(z[:, _C_Q:_C_Q + Q_LORA], qg_ref[...]).astype(_BF16)
        ckv = _rms(z[:, _C_KV:_C_KV + KV_LORA], kvg_ref[...]).astype(_BF16)
        q = jnp.dot(cq, wq_ref[...], preferred_element_type=_F32)
        kv = jnp.dot(ckv, wkv_ref[...], preferred_element_type=_F32)
        tab = tab_ref[rows, :]
        qmul = tab[:, 0:HEAD_W]
        cc = tab[:, HEAD_W:HEAD_W + LANES]
        ss = tab[:, HEAD_W + LANES:HEAD_W + 2 * LANES]
        krr = (z[:, _C_KR:_C_KR + LANES] * cc + z[:, _C_KRS:_C_KRS + LANES] * ss).astype(_BF16)
        for hd in range(N_HEADS):
            q_ref[rows, hd * HEAD_W:(hd + 1) * HEAD_W] = (q[:, hd * HEAD_W:(hd + 1) * HEAD_W] * qmul).astype(_BF16)
            k_ref[rows, hd * HEAD_W:hd * HEAD_W + NOPE_DIM] = kv[:, hd * NOPE_DIM:(hd + 1) * NOPE_DIM].astype(_BF16)
            k_ref[rows, hd * HEAD_W + NOPE_DIM:(hd + 1) * HEAD_W] = krr
        v_ref[rows, :] = kv[:, N_HEADS * NOPE_DIM:].astype(_BF16)
        ext_ref[MAX_WINDOW + r0:MAX_WINDOW + r0 + IN_SUB_ROWS, :] = z[:, _C_POOL:_C_POOL + POOL_WIDTH]

    u = ext_ref[MAX_WINDOW:MAX_WINDOW + tm, :]

    def window_sum(k0, k1, lo):
        acc = ext_ref[MAX_WINDOW - k0:MAX_WINDOW - k0 + tm, lo:lo + LANES]
        for k in range(k0 + 1, k1):
            acc = acc + ext_ref[MAX_WINDOW - k:MAX_WINDOW - k + tm, lo:lo + LANES]
        return acc

    t_pos = (i % tiles_per_seq) * tm + lax.broadcasted_iota(jnp.int32, (tm, LANES), 0)
    cnt = (t_pos + 1).astype(_F32)
    first_group = lax.broadcasted_iota(jnp.int32, (tm, LANES), 1) < POOL_GROUP_DIM
    w0, w1, w2, w3 = POOL_WINDOWS
    s_a = window_sum(0, w0, 0)
    s_b = s_a + window_sum(w0, w1, 0)
    pooled_lo = jnp.where(first_group, s_a / jnp.minimum(cnt, float(w0)), s_b / jnp.minimum(cnt, float(w1)))
    s_c = window_sum(0, w2, LANES)
    s_d = s_c + window_sum(w2, w3, LANES)
    pooled_hi = jnp.where(first_group, s_c / jnp.minimum(cnt, float(w2)), s_d / jnp.minimum(cnt, float(w3)))
    d = jnp.concatenate([pooled_lo - u[:, :LANES], pooled_hi - u[:, LANES:]], axis=1).astype(_BF16)
    y = jnp.dot(d, pw_ref[...], preferred_element_type=_F32) * ps_ref[...]
    p_ref[...] = y.astype(_BF16)
    ext_ref[0:MAX_WINDOW, :] = ext_ref[tm:tm + MAX_WINDOW, :]


def _of_layer(a, layer):
    return pl.BlockSpec((None,) + a.shape[1:], lambda i: (layer,) + (0,) * (a.ndim - 1))


def _in_proj(x2, g, win, qg, kvg, wq, wkv, tab, pw, ps, *, layer, seq):
    t = x2.shape[0]
    tm = TOKEN_TILE
    tps = seq // tm
    full = lambda a: _of_layer(a, layer)
    return pl.pallas_call(
        functools.partial(_in_kernel, tm=tm, tiles_per_seq=tps),
        grid=(t // tm,),
        in_specs=[pl.BlockSpec((tm, D_MODEL), lambda i: (i, 0)),
                  full(g), full(win), full(qg), full(kvg), full(wq), full(wkv),
                  pl.BlockSpec((tm, tab.shape[1]), lambda i: (i % tps, 0)),
                  full(pw), full(ps)],
        out_specs=[pl.BlockSpec((tm, N_HEADS * HEAD_W), lambda i: (i, 0)),
                   pl.BlockSpec((tm, N_HEADS * HEAD_W), lambda i: (i, 0)),
                   pl.BlockSpec((tm, N_HEADS * V_DIM), lambda i: (i, 0)),
                   pl.BlockSpec((tm, POOL_WIDTH), lambda i: (i, 0))],
        out_shape=[jax.ShapeDtypeStruct((t, N_HEADS * HEAD_W), _BF16),
                   jax.ShapeDtypeStruct((t, N_HEADS * HEAD_W), _BF16),
                   jax.ShapeDtypeStruct((t, N_HEADS * V_DIM), _BF16),
                   jax.ShapeDtypeStruct((t, POOL_WIDTH), _BF16)],
        scratch_shapes=[pltpu.VMEM((tm + MAX_WINDOW, POOL_WIDTH), _F32)],
        compiler_params=_params(("arbitrary",), 48),
        name="in_proj",
    )(x2, g, win, qg, kvg, wq, wkv, tab, pw, ps)


def _attn_kernel(*refs, tq, nq, cast_heads):
    n_cast = len(cast_heads)
    q_ref, k_ref, v_ref = refs[0:3]
    o_ref = refs[3 + n_cast]
    vaug_ref = refs[-1]
    head = pl.program_id(1)
    for (lo, hi), src_ref, dst_ref in zip(cast_heads, refs[3:3 + n_cast], refs[4 + n_cast:4 + 2 * n_cast]):
        @pl.when(jnp.logical_and(head >= lo, head < hi))
        def _():
            dst_ref[...] = src_ref[...].astype(_BF16)

    seq = v_ref.shape[0]
    vaug_ref[:, 0:V_DIM] = v_ref[...]
    ones_col = lax.broadcasted_iota(jnp.int32, (seq, LANES), 1) == 0
    vaug_ref[:, V_DIM:V_DIM + LANES] = jnp.where(ones_col, 1.0, 0.0).astype(_BF16)
    row_chunk = lax.broadcasted_iota(jnp.int32, (tq, tq), 0) // CHUNK
    col_chunk = lax.broadcasted_iota(jnp.int32, (tq, tq), 1) // CHUNK
    visible = col_chunk <= row_chunk
    nt = (((1,), (1,)), ((), ()))
    def scores(i):
        d0 = i * tq
        s = lax.dot_general(q_ref[d0:d0 + tq, :], k_ref[0:d0 + tq, :], nt, preferred_element_type=_F32)
        s_diag = jnp.where(visible, s[:, d0:], -jnp.inf)
        return s_diag if i == 0 else jnp.concatenate([s[:, :d0], s_diag], axis=1)

    order = list(range(nq - 1, -1, -1))
    s_next = scores(order[0])
    for n, i in enumerate(order):
        s = s_next
        d0 = i * tq
        p = jnp.exp2(s - jnp.max(s, axis=1, keepdims=True)).astype(_BF16)
        if n + 1 < nq:
            s_next = scores(order[n + 1])
        o = jnp.dot(p, vaug_ref[0:d0 + tq, :], preferred_element_type=_F32)
        o_ref[d0:d0 + tq, :] = (o[:, 0:V_DIM] / o[:, V_DIM:V_DIM + 1]).astype(_BF16)


def _attention(q, k, v, casts=()):
    b, seq, _ = q.shape
    tq = ATTN_TILE
    head = lambda w: pl.BlockSpec((None, seq, w), lambda bb, hh: (bb, 0, hh))
    cast_specs, cast_shapes = [], []
    for w, lo, hi in casts:
        n_h = hi - lo
        rows, cols = w.shape
        assert rows % (b * n_h * 2 * SUBLANES) == 0
        block_of = lambda bb, hh, lo=lo, n_h=n_h: (bb * n_h + jnp.clip(hh - lo, 0, n_h - 1), 0)
        cast_specs.append(pl.BlockSpec((rows // (b * n_h), cols), block_of))
        cast_shapes.append(jax.ShapeDtypeStruct(w.shape, _BF16))
    outs = pl.pallas_call(
        functools.partial(_attn_kernel, tq=tq, nq=seq // tq, cast_heads=tuple((lo, hi) for _, lo, hi in casts)),
        grid=(b, N_HEADS),
        in_specs=[head(HEAD_W), head(HEAD_W), head(V_DIM)] + cast_specs,
        out_specs=[head(V_DIM)] + cast_specs,
        out_shape=[jax.ShapeDtypeStruct((b, seq, N_HEADS * V_DIM), _BF16)] + cast_shapes,
        scratch_shapes=[pltpu.VMEM((seq, V_DIM + LANES), _BF16)],
        compiler_params=_params(("arbitrary", "arbitrary"), 56),
        name="attention",
    )(q, k, v, *(w for w, _, _ in casts))
    return outs[0], outs[1:]


def _mix_residual(a_ref, p_ref, x_ref, wa_ref, wp_ref, g1_ref):
    mix = (jnp.dot(a_ref[...], wa_ref[...], preferred_element_type=_F32)
           + jnp.dot(p_ref[...], wp_ref[...], preferred_element_type=_F32))
    return x_ref[...] + _rms(mix, g1_ref[...])


def _out_dense_kernel(a_ref, p_ref, x_ref, wa_ref, wp_ref, g1_ref, g2_ref,
                      wg_ref, wu_ref, wd_ref, g3_ref, o_ref):
    xn = _mix_residual(a_ref, p_ref, x_ref, wa_ref, wp_ref, g1_ref)
    h = _rms(xn, g2_ref[...]).astype(_BF16)
    gate = jnp.dot(h, wg_ref[...], preferred_element_type=_F32)
    up = jnp.dot(h, wu_ref[...], preferred_element_type=_F32)
    hm = _silu_mul(gate, up).astype(_BF16)
    f = jnp.dot(hm, wd_ref[...], preferred_element_type=_F32)
    o_ref[...] = xn + _rms(f, g3_ref[...])


def _out_moe_kernel(a_ref, p_ref, x_ref, wa_ref, wp_ref, g1_ref, g2_ref, wr_ref,
                    xo_ref, ho_ref, ri_ref, rf_ref, cnt_ref, carry_ref, *, tm):
    @pl.when(pl.program_id(0) == 0)
    def _():
        carry_ref[...] = jnp.zeros_like(carry_ref)

    xn = _mix_residual(a_ref, p_ref, x_ref, wa_ref, wp_ref, g1_ref)
    xo_ref[...] = xn
    h = _rms(xn, g2_ref[...])
    for s in range(ROW_SPLIT):
        ho_ref[pl.ds(s, tm, stride=ROW_SPLIT), :] = h[:, s * LANES:(s + 1) * LANES]

    h_hi = h.astype(_BF16)
    h_lo = (h - h_hi.astype(_F32)).astype(_BF16)
    wr = wr_ref[...]
    nt = (((1,), (1,)), ((), ()))
    by_hi = lax.dot_general(wr, h_hi, nt, preferred_element_type=_F32)
    by_lo = lax.dot_general(wr, h_lo, nt, preferred_element_type=_F32)
    logits = by_hi[0:N_EXPERTS] + by_hi[N_EXPERTS:] + by_lo[0:N_EXPERTS]
    sub = lax.broadcasted_iota(jnp.int32, (N_EXPERTS, tm), 0)
    v1 = jnp.max(logits, axis=0, keepdims=True)
    e1 = jnp.min(jnp.where(logits == v1, sub, N_EXPERTS), axis=0, keepdims=True)
    rest = jnp.where(sub == e1, -jnp.inf, logits)
    v2 = jnp.max(rest, axis=0, keepdims=True)
    e2 = jnp.min(jnp.where(rest == v2, sub, N_EXPERTS), axis=0, keepdims=True)
    ex = jnp.exp(v2 - v1)
    gate1 = 1.0 / (1.0 + ex)
    gate2 = ex * gate1

    sel1 = sub == e1
    sel2 = sub == e2
    member = jnp.logical_or(sel1, sel2).astype(_F32)
    earlier = (lax.broadcasted_iota(jnp.int32, (tm, tm), 0)
               < lax.broadcasted_iota(jnp.int32, (tm, tm), 1)).astype(_BF16)
    member_lhs = jnp.concatenate([member, jnp.zeros_like(member)], axis=0).astype(_BF16)
    carry = carry_ref[:, 0:1]
    before = jnp.dot(member_lhs, earlier, preferred_element_type=_F32)[0:N_EXPERTS] + carry
    r1 = jnp.sum(jnp.where(sel1, before, 0.0), axis=0, keepdims=True).astype(jnp.int32)
    r2 = jnp.sum(jnp.where(sel2, before, 0.0), axis=0, keepdims=True).astype(jnp.int32)
    total = carry + jnp.sum(member, axis=1, keepdims=True)
    carry_ref[...] = jnp.broadcast_to(total, carry_ref.shape)
    cnt_ref[...] = jnp.broadcast_to(total, cnt_ref.shape)

    ri_ref[...] = jnp.where(sub == 0, e1, jnp.where(sub == 1, e2, jnp.where(sub == 2, r1, r2)))
    rf_ref[...] = jnp.where(sub == 0, gate1, gate2)


def _out_proj(a2, p2, x2, wo, g1, g2, *, layer, w_router=None, dense=None):
    t = x2.shape[0]
    tm = TOKEN_TILE
    full = lambda a: pl.BlockSpec(a.shape, lambda i: (0,) * a.ndim)
    of_layer = lambda a: _of_layer(a, layer)
    attn_w, pool_w = a2.shape[1], p2.shape[1]
    assert attn_w % pool_w == 0
    wa_spec = pl.BlockSpec((None, attn_w, D_MODEL), lambda i: (layer, 0, 0))
    wp_spec = pl.BlockSpec((None, pool_w, D_MODEL), lambda i: (layer, attn_w // pool_w, 0))
    resident = lambda a: pl.BlockSpec(a.shape, lambda i: (0,) * a.ndim, pipeline_mode=pl.Buffered(1))
    row = lambda w: pl.BlockSpec((tm, w), lambda i: (i, 0))
    in_specs = [row(attn_w), row(pool_w), row(D_MODEL), wa_spec, wp_spec, of_layer(g1), of_layer(g2)]
    if w_router is None:
        wg, wu, wd, g3 = dense
        return pl.pallas_call(
            _out_dense_kernel,
            grid=(t // tm,),
            in_specs=in_specs + [resident(wg), resident(wu), resident(wd), of_layer(g3)],
            out_specs=row(D_MODEL),
            out_shape=jax.ShapeDtypeStruct((t, D_MODEL), _F32),
            compiler_params=_params(("arbitrary",), 56),
            name="out_proj_dense_ffn",
        )(a2, p2, x2, wo, wo, g1, g2, wg, wu, wd, g3)
    return pl.pallas_call(
        functools.partial(_out_moe_kernel, tm=tm),
        grid=(t // tm,),
        in_specs=in_specs + [full(w_router)],
        out_specs=[row(D_MODEL),
                   pl.BlockSpec((tm * ROW_SPLIT, LANES), lambda i: (i, 0)),
                   pl.BlockSpec((N_EXPERTS, tm), lambda i: (0, i)),
                   pl.BlockSpec((N_EXPERTS, tm), lambda i: (0, i)),
                   pl.BlockSpec((N_EXPERTS, LANES), lambda i: (0, 0))],
        out_shape=[jax.ShapeDtypeStruct((t, D_MODEL), _F32),
                   jax.ShapeDtypeStruct((t * ROW_SPLIT, LANES), _F32),
                   jax.ShapeDtypeStruct((N_EXPERTS, t), jnp.int32),
                   jax.ShapeDtypeStruct((N_EXPERTS, t), _F32),
                   jax.ShapeDtypeStruct((N_EXPERTS, LANES), _F32)],
        scratch_shapes=[pltpu.VMEM((N_EXPERTS, LANES), _F32)],
        compiler_params=_params(("arbitrary",), 48),
        name="out_proj_router",
    )(a2, p2, x2, wo, wo, g1, g2, w_router)


def _expert_kernel(te_ref, ch_ref, nv_ref, src_ref, h_hbm, wg_ref, wu_ref, wd_ref, y_ref,
                   buf_ref, xb_ref, acc_ref, sem, *, tm, n_tiles, n_steps):
    i = pl.program_id(0)
    c = pl.program_id(1)
    last_c = n_steps - 1
    nv = nv_ref[0]
    rows = tm * ROW_SPLIT

    def start_row(tile_base, slot, r):
        tok = src_ref[tile_base + r]
        pltpu.make_async_copy(
            h_hbm.at[pl.ds(pl.multiple_of(tok * ROW_SPLIT, ROW_SPLIT), ROW_SPLIT), :],
            buf_ref.at[slot, pl.ds(pl.multiple_of(r * ROW_SPLIT, ROW_SPLIT), ROW_SPLIT), :],
            sem.at[slot]).start()

    def wait_slot(slot):
        pltpu.make_async_copy(h_hbm.at[pl.ds(0, rows), :], buf_ref.at[slot], sem.at[slot]).wait()

    @pl.when(jnp.logical_and(i == 0, c == 0))
    def _():
        def body(r, carry):
            start_row(0, 0, r)
            return carry

        lax.fori_loop(0, tm, body, 0)

    @pl.when(jnp.logical_and(c == 0, i == nv))
    def _():
        wait_slot(i % 2)

    def swiglu_part(lo=0, hi=None):
        x = xb_ref[...]
        part = None
        for k0 in range(lo, wg_ref.shape[1] if hi is None else hi, EXPERT_FF_SUB):
            k1 = k0 + EXPERT_FF_SUB
            gate = jnp.dot(x, wg_ref[:, k0:k1], preferred_element_type=_F32)
            up = jnp.dot(x, wu_ref[:, k0:k1], preferred_element_type=_F32)
            hm = _silu_mul(gate, up).astype(_BF16)
            down = jnp.dot(hm, wd_ref[k0:k1, :], preferred_element_type=_F32)
            part = down if part is None else part + down
        return part

    def take_rows_and_look_ahead():
        slot = i % 2
        wait_slot(slot)
        for s in range(ROW_SPLIT):
            xb_ref[:, s * LANES:(s + 1) * LANES] = (
                buf_ref[slot, pl.ds(s, tm, stride=ROW_SPLIT), :].astype(_BF16))
        next_base = jnp.minimum(i + 1, n_tiles - 1) * tm
        for r in range(tm):
            start_row(next_base, 1 - slot, r)

    def store_rows(total):
        for s in range(ROW_SPLIT):
            y_ref[pl.ds(s, tm, stride=ROW_SPLIT), :] = total[:, s * LANES:(s + 1) * LANES]

    if n_steps == 1:
        half = wg_ref.shape[1] // 2

        @pl.when(i < nv)
        def _():
            take_rows_and_look_ahead()
            acc_ref[...] = swiglu_part(0, half)

        @pl.when(jnp.logical_and(i < nv, c == last_c))
        def _():
            store_rows(acc_ref[...] + swiglu_part(half))
    else:

        @pl.when(jnp.logical_and(i < nv, c == 0))
        def _():
            take_rows_and_look_ahead()
            acc_ref[...] = swiglu_part()

        if n_steps > 2:
            @pl.when(jnp.logical_and(i < nv, jnp.logical_and(c > 0, c < last_c)))
            def _():
                acc_ref[...] += swiglu_part()

        @pl.when(jnp.logical_and(i < nv, c == last_c))
        def _():
            store_rows(acc_ref[...] + swiglu_part())

    @pl.when(jnp.logical_and(c == last_c, i >= nv))
    def _():
        y_ref[...] = jnp.zeros_like(y_ref)

    @pl.when(jnp.logical_and(jnp.logical_and(i == n_tiles - 1, c == last_c), nv == n_tiles))
    def _():
        wait_slot(n_tiles % 2)


def _expert_ffn(tile_expert, step_chunk, n_valid, src_tok, h_rows, wg, wu, wd):
    tm = EXPERT_TILE
    n_tiles = tile_expert.shape[0]
    steps = EXPERT_FF_STEPS
    fc = wg.shape[2] // steps

    col_map = lambda i, c, te, ch, nv, src: (te[i], 0, ch[i * steps + c])
    row_map = lambda i, c, te, ch, nv, src: (te[i], ch[i * steps + c], 0)
    grid_spec = pltpu.PrefetchScalarGridSpec(
        num_scalar_prefetch=4,
        grid=(n_tiles, steps),
        in_specs=[pl.BlockSpec(memory_space=pl.ANY),
                  pl.BlockSpec((None, D_MODEL, fc), col_map),
                  pl.BlockSpec((None, D_MODEL, fc), col_map),
                  pl.BlockSpec((None, fc, D_MODEL), row_map)],
        out_specs=pl.BlockSpec((tm * ROW_SPLIT, LANES), lambda i, c, te, ch, nv, src: (i, 0)),
        scratch_shapes=[pltpu.VMEM((2, tm * ROW_SPLIT, LANES), _F32),
                        pltpu.VMEM((tm, D_MODEL), _BF16),
                        pltpu.VMEM((tm, D_MODEL), _F32),
                        pltpu.SemaphoreType.DMA((2,))],
    )
    return pl.pallas_call(
        functools.partial(_expert_kernel, tm=tm, n_tiles=n_tiles, n_steps=steps),
        grid_spec=grid_spec,
        out_shape=jax.ShapeDtypeStruct((n_tiles * tm * ROW_SPLIT, LANES), _F32),
        compiler_params=_params(("arbitrary", "arbitrary"), 60),
        name="expert_ffn",
    )(tile_expert, step_chunk, n_valid, src_tok, h_rows, wg, wu, wd)


def _combine_kernel(p1_ref, p2_ref, y_hbm, x_ref, rf_ref, g_ref, o_ref, buf_ref, sem, *, tm, n_tiles):
    i = pl.program_id(0)
    rows = tm * ROW_SPLIT

    def start_pair(tile_base, slot, r):
        if isinstance(r, int):
            dst = pl.ds(r * ROW_SPLIT, ROW_SPLIT)
        else:
            dst = pl.ds(pl.multiple_of(r * ROW_SPLIT, ROW_SPLIT), ROW_SPLIT)
        for k, pos_ref in enumerate((p1_ref, p2_ref)):
            pos = pos_ref[tile_base + r]
            pltpu.make_async_copy(
                y_hbm.at[pl.ds(pl.multiple_of(pos * ROW_SPLIT, ROW_SPLIT), ROW_SPLIT), :],
                buf_ref.at[slot, k, dst, :], sem.at[slot]).start(priority=k)

    def wait_slot(slot):
        for k in range(2):
            pltpu.make_async_copy(y_hbm.at[pl.ds(0, rows), :], buf_ref.at[slot, k], sem.at[slot]).wait()

    @pl.when(i == 0)
    def _():
        def body(r, carry):
            start_pair(0, 0, r)
            return carry

        lax.fori_loop(0, tm, body, 0)

    for slot in range(2):
        @pl.when(i % 2 == slot)
        def _():
            wait_slot(slot)
            next_base = jnp.minimum(i + 1, n_tiles - 1) * tm
            for r in range(tm):
                start_pair(next_base, 1 - slot, r)
            rf = rf_ref[...]
            g1 = rf[:, 0:1]
            g2 = rf[:, 1:2]
            parts = []
            for s in range(ROW_SPLIT):
                y1 = buf_ref[slot, 0, pl.ds(s, tm, stride=ROW_SPLIT), :]
                y2 = buf_ref[slot, 1, pl.ds(s, tm, stride=ROW_SPLIT), :]
                parts.append(g1 * y1 + g2 * y2)
            f = jnp.concatenate(parts, axis=1)
            o_ref[...] = x_ref[...] + _rms(f, g_ref[...])

    @pl.when(i == n_tiles - 1)
    def _():
        wait_slot(n_tiles % 2)


def _combine(pos1, pos2, y_rows, x2, route_f, g):
    t = x2.shape[0]
    tm = TOKEN_TILE
    grid_spec = pltpu.PrefetchScalarGridSpec(
        num_scalar_prefetch=2,
        grid=(t // tm,),
        in_specs=[pl.BlockSpec(memory_space=pl.ANY),
                  pl.BlockSpec((tm, D_MODEL), lambda i, p1, p2: (i, 0)),
                  pl.BlockSpec((tm, route_f.shape[1]), lambda i, p1, p2: (i, 0)),
                  pl.BlockSpec(g.shape, lambda i, p1, p2: (0, 0))],
        out_specs=pl.BlockSpec((tm, D_MODEL), lambda i, p1, p2: (i, 0)),
        scratch_shapes=[pltpu.VMEM((2, 2, tm * ROW_SPLIT, LANES), _F32),
                        pltpu.SemaphoreType.DMA((2,))],
    )
    return pl.pallas_call(
        functools.partial(_combine_kernel, tm=tm, n_tiles=t // tm),
        grid_spec=grid_spec,
        out_shape=jax.ShapeDtypeStruct((t, D_MODEL), _F32),
        compiler_params=_params(("arbitrary",), 40),
        name="combine",
    )(pos1, pos2, y_rows, x2, route_f, g)


def _swap_halves(w):
    half = w.shape[-1] // 2
    return jnp.concatenate([w[..., half:], w[..., :half]], axis=-1)


def _layer_weights(w_in, w_uq, w_ukv, pool_w, w_out):
    depth = w_in.shape[0]
    kr = w_in[..., _C_KR:_C_KR + ROPE_DIM]
    krs = _swap_halves(kr)
    win = jnp.concatenate([w_in[..., :_C_KR], kr, kr, krs, krs, w_in[..., _C_KR + ROPE_DIM:]], axis=-1)
    q_rope = w_uq[..., NOPE_DIM:]
    wq = jnp.concatenate([w_uq[..., :NOPE_DIM], q_rope, _swap_halves(q_rope)], axis=-1)
    wq = wq.reshape(depth, Q_LORA, N_HEADS * HEAD_W)
    wkv = jnp.concatenate([w_ukv[..., :NOPE_DIM].reshape(depth, KV_LORA, N_HEADS * NOPE_DIM),
                           w_ukv[..., NOPE_DIM:].reshape(depth, KV_LORA, N_HEADS * V_DIM)], axis=-1)
    same_group = jnp.eye(len(POOL_WINDOWS), dtype=pool_w.dtype)[:, None, :, None]
    pw = (pool_w[:, :, :, None, :] * same_group).reshape(depth, POOL_WIDTH, POOL_WIDTH)
    return tuple(a.astype(_BF16) for a in (win, wq, wkv, pw, w_out))


def _rope_table(seq):
    inv = 1.0 / (ROPE_THETA ** (jnp.arange(0, ROPE_DIM, 2, dtype=_F32) / ROPE_DIM))
    ang = jnp.arange(seq, dtype=_F32)[:, None] * inv[None, :]
    cos, sin = jnp.cos(ang), jnp.sin(ang)
    cos2 = jnp.concatenate([cos, cos], axis=1)
    sin2 = jnp.concatenate([-sin, sin], axis=1)
    scale = (NOPE_DIM + ROPE_DIM) ** -0.5 * 1.4426950408889634
    qmul = jnp.concatenate([jnp.full((seq, NOPE_DIM), scale, _F32), scale * cos2, scale * sin2], axis=1)
    return jnp.concatenate([qmul, cos2, cos2, sin2, sin2], axis=1)


def _routing_plan(route_i, counts, t):
    tm = EXPERT_TILE
    n_tiles = (2 * t) // tm + N_EXPERTS
    cnt = counts[:, 0].astype(jnp.int32)
    padded = ((cnt + tm - 1) // tm) * tm
    ends = jnp.cumsum(padded)
    base = ends - padded
    n_valid = (ends[-1] // tm).astype(jnp.int32)
    tile = jnp.arange(n_tiles, dtype=jnp.int32)
    tile_expert = jnp.minimum(jnp.sum((tile * tm)[:, None] >= ends[None, :], axis=1), N_EXPERTS - 1)
    first_tile = jnp.sum(jnp.where(tile_expert[:, None] == jnp.arange(N_EXPERTS)[None, :],
                                   (base // tm)[None, :], 0), axis=1)
    backward = ((tile - first_tile) % 2 == 1)[:, None]
    steps = jnp.arange(EXPERT_FF_STEPS, dtype=jnp.int32)[None, :]
    step_chunk = jnp.where(backward, EXPERT_FF_STEPS - 1 - steps, steps)
    used = (tile < n_valid)[:, None]
    last = n_valid - 1
    tile_expert = jnp.where(used[:, 0], tile_expert, tile_expert[last])
    step_chunk = jnp.where(used, step_chunk, step_chunk[last, EXPERT_FF_STEPS - 1])
    e1, e2, pos1, pos2 = (route_i[k] for k in range(4))
    for k in range(N_EXPERTS):
        pos1 = pos1 + jnp.where(e1 == k, base[k], 0)
        pos2 = pos2 + jnp.where(e2 == k, base[k], 0)
    src = _invert_positions(pos1, pos2, n_tiles * tm)
    return (tile_expert.astype(jnp.int32), step_chunk.reshape(-1).astype(jnp.int32), n_valid.reshape(1),
            src, pos1, pos2)


def _invert_kernel(p1_ref, p2_ref, zeros_hbm, src_ref, *, unroll):
    pltpu.sync_copy(zeros_hbm, src_ref)

    def fill(j, carry):
        for u in range(unroll):
            tok = j * unroll + u
            src_ref[p1_ref[tok]] = tok
            src_ref[p2_ref[tok]] = tok
        return carry

    lax.fori_loop(0, p1_ref.shape[0] // unroll, fill, 0)


def _invert_positions(pos1, pos2, n_slots):
    unroll = 8
    assert n_slots % unroll == 0 and pos1.shape[0] % unroll == 0
    smem = pl.BlockSpec(memory_space=pltpu.SMEM)
    return pl.pallas_call(
        functools.partial(_invert_kernel, unroll=unroll),
        in_specs=[smem, smem, pl.BlockSpec(memory_space=pl.ANY)],
        out_specs=smem,
        out_shape=jax.ShapeDtypeStruct((n_slots,), jnp.int32),
        name="invert_positions",
    )(pos1, pos2, jnp.zeros((n_slots,), jnp.int32))


def kernel(x, attn_pre_g, attn_post_g, ffn_pre_g, ffn_post_g, w_in, q_norm_g, kv_norm_g, w_uq, w_ukv,
           pool_w, pool_scale, w_out, w_gate_d, w_up_d, w_down_d, w_router, w_gate_e, w_up_e, w_down_e):
    b, seq, d = x.shape
    t = b * seq
    depth = w_in.shape[0]
    assert d == D_MODEL and seq % TOKEN_TILE == 0 and seq % ATTN_TILE == 0
    tab = _rope_table(seq)
    win, wq, wkv, pw, wo = _layer_weights(w_in, w_uq, w_ukv, pool_w, w_out)
    rows = lambda a: a.reshape(depth, 1, -1)
    attn_pre_g, attn_post_g, ffn_pre_g, ffn_post_g, q_norm_g, kv_norm_g, pool_scale = map(
        rows, (attn_pre_g, attn_post_g, ffn_pre_g, ffn_post_g, q_norm_g, kv_norm_g, pool_scale))
    x2 = x.reshape(t, d)
    for l in range(depth):
        q, k, v, p = _in_proj(x2, attn_pre_g, win, q_norm_g, kv_norm_g, wq, wkv, tab, pw, pool_scale,
                              layer=l, seq=seq)
        i = l // 2
        if l % 2 == 1:
            casts = ((w_up_e[i].reshape(-1, w_up_e.shape[-1]),) + CAST_HEADS_WIDE,
                     (w_down_e[i].reshape(-1, d),) + CAST_HEADS_REST)
        else:
            casts = tuple((w,) + CAST_HEADS_REST for w in (w_gate_d[i], w_up_d[i], w_down_d[i]))
            if l + 1 < depth:
                casts += ((w_gate_e[(l + 1) // 2].reshape(-1, w_gate_e.shape[-1]),) + CAST_HEADS_WIDE,)
        a, converted = _attention(q.reshape(b, seq, -1), k.reshape(b, seq, -1), v.reshape(b, seq, -1), casts)
        a2 = a.reshape(t, -1)
        if l % 2 == 0:
            x2 = _out_proj(a2, p, x2, wo, attn_post_g, ffn_pre_g, layer=l,
                           dense=tuple(converted[0:3]) + (ffn_post_g,))
            next_gate = converted[3].reshape(w_gate_e.shape[1:]) if len(converted) > 3 else None
        else:
            wr_t = w_router[i].T
            wr_hi = wr_t.astype(_BF16)
            wr = jnp.concatenate([wr_hi, (wr_t - wr_hi.astype(_F32)).astype(_BF16)], axis=0)
            x2, h_rows, route_i, route_f, counts = _out_proj(
                a2, p, x2, wo, attn_post_g, ffn_pre_g, layer=l, w_router=wr)
            tile_expert, step_chunk, n_valid, src, pos1, pos2 = _routing_plan(route_i, counts, t)
            y_rows = _expert_ffn(tile_expert, step_chunk, n_valid, src, h_rows, next_gate,
                                 converted[0].reshape(w_up_e.shape[1:]),
                                 converted[1].reshape(w_down_e.shape[1:]))
            x2 = _combine(pos1, pos2, y_rows, x2, route_f[0:2].T, ffn_post_g[l])
    return x2.reshape(b, seq, d)
```

```python
import functools

import jax
import jax.numpy as jnp
from jax import lax
from jax.experimental import pallas as pl
from jax.experimental.pallas import tpu as pltpu

D_MODEL = 1024
CHUNK = 64
EPS = 1e-6
N_HEADS = 6
NOPE_DIM = 128
ROPE_DIM = 64
V_DIM = 128
Q_LORA = 384
KV_LORA = 256
ROPE_THETA = 10000.0
POOL_WINDOWS = (2, 4, 8, 16)
POOL_GROUP_DIM = 64
POOL_WIDTH = len(POOL_WINDOWS) * POOL_GROUP_DIM
N_EXPERTS = 8
MAX_WINDOW = max(POOL_WINDOWS)

LANES = 128
SUBLANES = 8
HEAD_W = 2 * LANES
ROW_SPLIT = D_MODEL // LANES

_C_Q = 0
_C_KV = Q_LORA
_C_KR = Q_LORA + KV_LORA
_C_KRS = _C_KR + LANES
_C_POOL = _C_KRS + LANES
IN_AUG = _C_POOL + POOL_WIDTH

TOKEN_TILE = 512
WIDE_TOKEN_TILE = 1024
IN_SUB_ROWS = 256
ATTN_TILE = 256
EXPERT_TILE = 512
EXPERT_FF_STEPS = 1
EXPERT_FF_SUB = 256
CAST_HEADS_WIDE = (0, 4)
CAST_HEADS_REST = (4, 6)

_F32 = jnp.float32
_BF16 = jnp.bfloat16
_MIB = 1024 * 1024


def _rms(x, g):
    return x * lax.rsqrt(jnp.mean(x * x, axis=-1, keepdims=True) + EPS) * g


def _silu_mul(g, u):
    hg = 0.5 * g
    return (hg + hg * jnp.tanh(hg)) * u


def _params(sem, vmem_mib):
    return pltpu.CompilerParams(dimension_semantics=sem, vmem_limit_bytes=vmem_mib * _MIB)


def _in_kernel(x_ref, g_ref, win_ref, qg_ref, kvg_ref, wq_ref, wkv_ref, tab_ref, pw_ref, ps_ref,
               q_ref, k_ref, v_ref, p_ref, ext_ref, *, tm, tiles_per_seq):
    i = pl.program_id(0)

    @pl.when(i % tiles_per_seq == 0)
    def _():
        ext_ref[0:MAX_WINDOW, :] = jnp.zeros((MAX_WINDOW, POOL_WIDTH), _F32)

    for r0 in range(0, tm, IN_SUB_ROWS):
        rows = slice(r0, r0 + IN_SUB_ROWS)
        h = _rms(x_ref[rows, :], g_ref[...]).astype(_BF16)
        z = jnp.dot(h, win_ref[...], preferred_element_type=_F32)
        cq = _rms(z[:, _C_Q:_C_Q + Q_LORA], qg_ref[...]).astype(_BF16)
        ckv = _rms(z[:, _C_KV:_C_KV + KV_LORA], kvg_ref[...]).astype(_BF16)
        q = jnp.dot(cq, wq_ref[...], preferred_element_type=_F32)
        kv = jnp.dot(ckv, wkv_ref[...], preferred_element_type=_F32)
        tab = tab_ref[rows, :]
        qmul = tab[:, 0:HEAD_W]
        cc = tab[:, HEAD_W:HEAD_W + LANES]
        ss = tab[:, HEAD_W + LANES:HEAD_W + 2 * LANES]
        krr = (z[:, _C_KR:_C_KR + LANES] * cc + z[:, _C_KRS:_C_KRS + LANES] * ss).astype(_BF16)
        for hd in range(N_HEADS):
            q_ref[rows, hd * HEAD_W:(hd + 1) * HEAD_W] = (q[:, hd * HEAD_W:(hd + 1) * HEAD_W] * qmul).astype(_BF16)
            k_ref[rows, hd * HEAD_W:hd * HEAD_W + NOPE_DIM] = kv[:, hd * NOPE_DIM:(hd + 1) * NOPE_DIM].astype(_BF16)
            k_ref[rows, hd * HEAD_W + NOPE_DIM:(hd + 1) * HEAD_W] = krr
        v_ref[rows, :] = kv[:, N_HEADS * NOPE_DIM:].astype(_BF16)
        ext_ref[MAX_WINDOW + r0:MAX_WINDOW + r0 + IN_SUB_ROWS, :] = z[:, _C_POOL:_C_POOL + POOL_WIDTH]

    u = ext_ref[MAX_WINDOW:MAX_WINDOW + tm, :]

    def window_sum(k0, k1, lo):
        acc = ext_ref[MAX_WINDOW - k0:MAX_WINDOW - k0 + tm, lo:lo + LANES]
        for k in range(k0 + 1, k1):
            acc = acc + ext_ref[MAX_WINDOW - k:MAX_WINDOW - k + tm, lo:lo + LANES]
        return acc

    t_pos = (i % tiles_per_seq) * tm + lax.broadcasted_iota(jnp.int32, (tm, LANES), 0)
    cnt = (t_pos + 1).astype(_F32)
    first_group = lax.broadcasted_iota(jnp.int32, (tm, LANES), 1) < POOL_GROUP_DIM
    w0, w1, w2, w3 = POOL_WINDOWS
    s_a = window_sum(0, w0, 0)
    s_b = s_a + window_sum(w0, w1, 0)
    pooled_lo = jnp.where(first_group, s_a / jnp.minimum(cnt, float(w0)), s_b / jnp.minimum(cnt, float(w1)))
    s_c = window_sum(0, w2, LANES)
    s_d = s_c + window_sum(w2, w3, LANES)
    pooled_hi = jnp.where(first_group, s_c / jnp.minimum(cnt, float(w2)), s_d / jnp.minimum(cnt, float(w3)))
    d = jnp.concatenate([pooled_lo - u[:, :LANES], pooled_hi - u[:, LANES:]], axis=1).astype(_BF16)
    y = jnp.dot(d, pw_ref[...], preferred_element_type=_F32) * ps_ref[...]
    p_ref[...] = y.astype(_BF16)
    ext_ref[0:MAX_WINDOW, :] = ext_ref[tm:tm + MAX_WINDOW, :]


def _of_layer(a, layer):
    return pl.BlockSpec((None,) + a.shape[1:], lambda i: (layer,) + (0,) * (a.ndim - 1))


def _in_proj(x2, g, win, qg, kvg, wq, wkv, tab, pw, ps, *, layer, seq):
    t = x2.shape[0]
    tm = WIDE_TOKEN_TILE
    tps = seq // tm
    full = lambda a: _of_layer(a, layer)
    return pl.pallas_call(
        functools.partial(_in_kernel, tm=tm, tiles_per_seq=tps),
        grid=(t // tm,),
        in_specs=[pl.BlockSpec((tm, D_MODEL), lambda i: (i, 0)),
                  full(g), full(win), full(qg), full(kvg), full(wq), full(wkv),
                  pl.BlockSpec((tm, tab.shape[1]), lambda i: (i % tps, 0)),
                  full(pw), full(ps)],
        out_specs=[pl.BlockSpec((tm, N_HEADS * HEAD_W), lambda i: (i, 0)),
                   pl.BlockSpec((tm, N_HEADS * HEAD_W), lambda i: (i, 0)),
                   pl.BlockSpec((tm, N_HEADS * V_DIM), lambda i: (i, 0)),
                   pl.BlockSpec((tm, POOL_WIDTH), lambda i: (i, 0))],
        out_shape=[jax.ShapeDtypeStruct((t, N_HEADS * HEAD_W), _BF16),
                   jax.ShapeDtypeStruct((t, N_HEADS * HEAD_W), _BF16),
                   jax.ShapeDtypeStruct((t, N_HEADS * V_DIM), _BF16),
                   jax.ShapeDtypeStruct((t, POOL_WIDTH), _BF16)],
        scratch_shapes=[pltpu.VMEM((tm + MAX_WINDOW, POOL_WIDTH), _F32)],
        compiler_params=_params(("arbitrary",), 48),
        name="in_proj",
    )(x2, g, win, qg, kvg, wq, wkv, tab, pw, ps)


def _attn_kernel(*refs, tq, nq, cast_heads):
    n_cast = len(cast_heads)
    q_ref, k_ref, v_ref = refs[0:3]
    o_ref = refs[3 + n_cast]
    vaug_ref = refs[-1]
    head = pl.program_id(1)
    for (lo, hi), src_ref, dst_ref in zip(cast_heads, refs[3:3 + n_cast], refs[4 + n_cast:4 + 2 * n_cast]):
        @pl.when(jnp.logical_and(head >= lo, head < hi))
        def _():
            dst_ref[...] = src_ref[...].astype(_BF16)

    seq = v_ref.shape[0]
    vaug_ref[:, 0:V_DIM] = v_ref[...]
    ones_col = lax.broadcasted_iota(jnp.int32, (seq, LANES), 1) == 0
    vaug_ref[:, V_DIM:V_DIM + LANES] = jnp.where(ones_col, 1.0, 0.0).astype(_BF16)
    row_chunk = lax.broadcasted_iota(jnp.int32, (tq, tq), 0) // CHUNK
    col_chunk = lax.broadcasted_iota(jnp.int32, (tq, tq), 1) // CHUNK
    visible = col_chunk <= row_chunk
    nt = (((1,), (1,)), ((), ()))
    def scores(i):
        d0 = i * tq
        s = lax.dot_general(q_ref[d0:d0 + tq, :], k_ref[0:d0 + tq, :], nt, preferred_element_type=_F32)
        s_diag = jnp.where(visible, s[:, d0:], -jnp.inf)
        return s_diag if i == 0 else jnp.concatenate([s[:, :d0], s_diag], axis=1)

    order = list(range(nq - 1, -1, -1))
    s_next = scores(order[0])
    for n, i in enumerate(order):
        s = s_next
        d0 = i * tq
        p = jnp.exp2(s - jnp.max(s, axis=1, keepdims=True)).astype(_BF16)
        if n + 1 < nq:
            s_next = scores(order[n + 1])
        o = jnp.dot(p, vaug_ref[0:d0 + tq, :], preferred_element_type=_F32)
        o_ref[d0:d0 + tq, :] = (o[:, 0:V_DIM] / o[:, V_DIM:V_DIM + 1]).astype(_BF16)


def _attention(q, k, v, casts=()):
    b, seq, _ = q.shape
    tq = ATTN_TILE
    head = lambda w: pl.BlockSpec((None, seq, w), lambda bb, hh: (bb, 0, hh))
    cast_specs, cast_shapes = [], []
    for w, lo, hi in casts:
        n_h = hi - lo
        rows, cols = w.shape
        assert rows % (b * n_h * 2 * SUBLANES) == 0
        block_of = lambda bb, hh, lo=lo, n_h=n_h: (bb * n_h + jnp.clip(hh - lo, 0, n_h - 1), 0)
        cast_specs.append(pl.BlockSpec((rows // (b * n_h), cols), block_of))
        cast_shapes.append(jax.ShapeDtypeStruct(w.shape, _BF16))
    outs = pl.pallas_call(
        functools.partial(_attn_kernel, tq=tq, nq=seq // tq, cast_heads=tuple((lo, hi) for _, lo, hi in casts)),
        grid=(b, N_HEADS),
        in_specs=[head(HEAD_W), head(HEAD_W), head(V_DIM)] + cast_specs,
        out_specs=[head(V_DIM)] + cast_specs,
        out_shape=[jax.ShapeDtypeStruct((b, seq, N_HEADS * V_DIM), _BF16)] + cast_shapes,
        scratch_shapes=[pltpu.VMEM((seq, V_DIM + LANES), _BF16)],
        compiler_params=_params(("arbitrary", "arbitrary"), 56),
        name="attention",
    )(q, k, v, *(w for w, _, _ in casts))
    return outs[0], outs[1:]


def _mix_residual(a_ref, p_ref, x_ref, wa_ref, wp_ref, g1_ref):
    mix = (jnp.dot(a_ref[...], wa_ref[...], preferred_element_type=_F32)
           + jnp.dot(p_ref[...], wp_ref[...], preferred_element_type=_F32))
    return x_ref[...] + _rms(mix, g1_ref[...])


def _out_dense_kernel(a_ref, p_ref, x_ref, wa_ref, wp_ref, g1_ref, g2_ref,
                      wg_ref, wu_ref, wd_ref, g3_ref, o_ref):
    xn = _mix_residual(a_ref, p_ref, x_ref, wa_ref, wp_ref, g1_ref)
    h = _rms(xn, g2_ref[...]).astype(_BF16)
    gate = jnp.dot(h, wg_ref[...], preferred_element_type=_F32)
    up = jnp.dot(h, wu_ref[...], preferred_element_type=_F32)
    hm = _silu_mul(gate, up).astype(_BF16)
    f = jnp.dot(hm, wd_ref[...], preferred_element_type=_F32)
    o_ref[...] = xn + _rms(f, g3_ref[...])


def _out_moe_kernel(a_ref, p_ref, x_ref, wa_ref, wp_ref, g1_ref, g2_ref, wr_ref,
                    xo_ref, ho_ref, ri_ref, rf_ref, cnt_ref, carry_ref, *, tm):
    @pl.when(pl.program_id(0) == 0)
    def _():
        carry_ref[...] = jnp.zeros_like(carry_ref)

    xn = _mix_residual(a_ref, p_ref, x_ref, wa_ref, wp_ref, g1_ref)
    xo_ref[...] = xn
    h = _rms(xn, g2_ref[...])
    for s in range(ROW_SPLIT):
        ho_ref[pl.ds(s, tm, stride=ROW_SPLIT), :] = h[:, s * LANES:(s + 1) * LANES]

    h_hi = h.astype(_BF16)
    h_lo = (h - h_hi.astype(_F32)).astype(_BF16)
    wr = wr_ref[...]
    nt = (((1,), (1,)), ((), ()))
    by_hi = lax.dot_general(wr, h_hi, nt, preferred_element_type=_F32)
    by_lo = lax.dot_general(wr, h_lo, nt, preferred_element_type=_F32)
    logits = by_hi[0:N_EXPERTS] + by_hi[N_EXPERTS:] + by_lo[0:N_EXPERTS]
    sub = lax.broadcasted_iota(jnp.int32, (N_EXPERTS, tm), 0)
    v1 = jnp.max(logits, axis=0, keepdims=True)
    e1 = jnp.min(jnp.where(logits == v1, sub, N_EXPERTS), axis=0, keepdims=True)
    rest = jnp.where(sub == e1, -jnp.inf, logits)
    v2 = jnp.max(rest, axis=0, keepdims=True)
    e2 = jnp.min(jnp.where(rest == v2, sub, N_EXPERTS), axis=0, keepdims=True)
    ex = jnp.exp(v2 - v1)
    gate1 = 1.0 / (1.0 + ex)
    gate2 = ex * gate1

    sel1 = sub == e1
    sel2 = sub == e2
    member = jnp.logical_or(sel1, sel2).astype(_F32)
    earlier = (lax.broadcasted_iota(jnp.int32, (tm, tm), 0)
               < lax.broadcasted_iota(jnp.int32, (tm, tm), 1)).astype(_BF16)
    member_lhs = jnp.concatenate([member, jnp.zeros_like(member)], axis=0).astype(_BF16)
    carry = carry_ref[:, 0:1]
    before = jnp.dot(member_lhs, earlier, preferred_element_type=_F32)[0:N_EXPERTS] + carry
    r1 = jnp.sum(jnp.where(sel1, before, 0.0), axis=0, keepdims=True).astype(jnp.int32)
    r2 = jnp.sum(jnp.where(sel2, before, 0.0), axis=0, keepdims=True).astype(jnp.int32)
    total = carry + jnp.sum(member, axis=1, keepdims=True)
    carry_ref[...] = jnp.broadcast_to(total, carry_ref.shape)
    cnt_ref[...] = jnp.broadcast_to(total, cnt_ref.shape)

    ri_ref[...] = jnp.where(sub == 0, e1, jnp.where(sub == 1, e2, jnp.where(sub == 2, r1, r2)))
    rf_ref[...] = jnp.where(sub == 0, gate1, gate2)


def _out_proj(a2, p2, x2, wo, g1, g2, *, layer, w_router=None, dense=None):
    t = x2.shape[0]
    tm = TOKEN_TILE if w_router is None else WIDE_TOKEN_TILE
    full = lambda a: pl.BlockSpec(a.shape, lambda i: (0,) * a.ndim)
    of_layer = lambda a: _of_layer(a, layer)
    attn_w, pool_w = a2.shape[1], p2.shape[1]
    assert attn_w % pool_w == 0
    wa_spec = pl.BlockSpec((None, attn_w, D_MODEL), lambda i: (layer, 0, 0))
    wp_spec = pl.BlockSpec((None, pool_w, D_MODEL), lambda i: (layer, attn_w // pool_w, 0))
    resident = lambda a: pl.BlockSpec(a.shape, lambda i: (0,) * a.ndim, pipeline_mode=pl.Buffered(1))
    row = lambda w: pl.BlockSpec((tm, w), lambda i: (i, 0))
    in_specs = [row(attn_w), row(pool_w), row(D_MODEL), wa_spec, wp_spec, of_layer(g1), of_layer(g2)]
    if w_router is None:
        wg, wu, wd, g3 = dense
        return pl.pallas_call(
            _out_dense_kernel,
            grid=(t // tm,),
            in_specs=in_specs + [resident(wg), resident(wu), resident(wd), of_layer(g3)],
            out_specs=row(D_MODEL),
            out_shape=jax.ShapeDtypeStruct((t, D_MODEL), _F32),
            compiler_params=_params(("arbitrary",), 56),
            name="out_proj_dense_ffn",
        )(a2, p2, x2, wo, wo, g1, g2, wg, wu, wd, g3)
    return pl.pallas_call(
        functools.partial(_out_moe_kernel, tm=tm),
        grid=(t // tm,),
        in_specs=in_specs + [full(w_router)],
        out_specs=[row(D_MODEL),
                   pl.BlockSpec((tm * ROW_SPLIT, LANES), lambda i: (i, 0)),
                   pl.BlockSpec((N_EXPERTS, tm), lambda i: (0, i)),
                   pl.BlockSpec((N_EXPERTS, tm), lambda i: (0, i)),
                   pl.BlockSpec((N_EXPERTS, LANES), lambda i: (0, 0))],
        out_shape=[jax.ShapeDtypeStruct((t, D_MODEL), _F32),
                   jax.ShapeDtypeStruct((t * ROW_SPLIT, LANES), _F32),
                   jax.ShapeDtypeStruct((N_EXPERTS, t), jnp.int32),
                   jax.ShapeDtypeStruct((N_EXPERTS, t), _F32),
                   jax.ShapeDtypeStruct((N_EXPERTS, LANES), _F32)],
        scratch_shapes=[pltpu.VMEM((N_EXPERTS, LANES), _F32)],
        compiler_params=_params(("arbitrary",), 48),
        name="out_proj_router",
    )(a2, p2, x2, wo, wo, g1, g2, w_router)


def _expert_kernel(te_ref, ch_ref, nv_ref, src_ref, h_hbm, wg_ref, wu_ref, wd_ref, y_ref,
                   buf_ref, xb_ref, acc_ref, sem, *, tm, n_tiles, n_steps):
    i = pl.program_id(0)
    c = pl.program_id(1)
    last_c = n_steps - 1
    nv = nv_ref[0]
    rows = tm * ROW_SPLIT

    def start_row(tile_base, slot, r):
        tok = src_ref[tile_base + r]
        pltpu.make_async_copy(
            h_hbm.at[pl.ds(pl.multiple_of(tok * ROW_SPLIT, ROW_SPLIT), ROW_SPLIT), :],
            buf_ref.at[slot, pl.ds(pl.multiple_of(r * ROW_SPLIT, ROW_SPLIT), ROW_SPLIT), :],
            sem.at[slot]).start()

    def wait_slot(slot):
        pltpu.make_async_copy(h_hbm.at[pl.ds(0, rows), :], buf_ref.at[slot], sem.at[slot]).wait()

    @pl.when(jnp.logical_and(i == 0, c == 0))
    def _():
        def body(r, carry):
            start_row(0, 0, r)
            return carry

        lax.fori_loop(0, tm, body, 0)

    @pl.when(jnp.logical_and(c == 0, i == nv))
    def _():
        wait_slot(i % 2)

    def swiglu_part(lo=0, hi=None):
        x = xb_ref[...]
        part = None
        for k0 in range(lo, wg_ref.shape[1] if hi is None else hi, EXPERT_FF_SUB):
            k1 = k0 + EXPERT_FF_SUB
            gate = jnp.dot(x, wg_ref[:, k0:k1], preferred_element_type=_F32)
            up = jnp.dot(x, wu_ref[:, k0:k1], preferred_element_type=_F32)
            hm = _silu_mul(gate, up).astype(_BF16)
            down = jnp.dot(hm, wd_ref[k0:k1, :], preferred_element_type=_F32)
            part = down if part is None else part + down
        return part

    def take_rows_and_look_ahead():
        slot = i % 2
        wait_slot(slot)
        for s in range(ROW_SPLIT):
            xb_ref[:, s * LANES:(s + 1) * LANES] = (
                buf_ref[slot, pl.ds(s, tm, stride=ROW_SPLIT), :].astype(_BF16))
        next_base = jnp.minimum(i + 1, n_tiles - 1) * tm
        for r in range(tm):
            start_row(next_base, 1 - slot, r)

    def store_rows(total):
        for s in range(ROW_SPLIT):
            y_ref[pl.ds(s, tm, stride=ROW_SPLIT), :] = total[:, s * LANES:(s + 1) * LANES]

    if n_steps == 1:
        half = wg_ref.shape[1] // 2

        @pl.when(i < nv)
        def _():
            take_rows_and_look_ahead()
            acc_ref[...] = swiglu_part(0, half)

        @pl.when(jnp.logical_and(i < nv, c == last_c))
        def _():
            store_rows(acc_ref[...] + swiglu_part(half))
    else:

        @pl.when(jnp.logical_and(i < nv, c == 0))
        def _():
            take_rows_and_look_ahead()
            acc_ref[...] = swiglu_part()

        if n_steps > 2:
            @pl.when(jnp.logical_and(i < nv, jnp.logical_and(c > 0, c < last_c)))
            def _():
                acc_ref[...] += swiglu_part()

        @pl.when(jnp.logical_and(i < nv, c == last_c))
        def _():
            store_rows(acc_ref[...] + swiglu_part())

    @pl.when(jnp.logical_and(c == last_c, i >= nv))
    def _():
        y_ref[...] = jnp.zeros_like(y_ref)

    @pl.when(jnp.logical_and(jnp.logical_and(i == n_tiles - 1, c == last_c), nv == n_tiles))
    def _():
        wait_slot(n_tiles % 2)


def _expert_ffn(tile_expert, step_chunk, n_valid, src_tok, h_rows, wg, wu, wd):
    tm = EXPERT_TILE
    n_tiles = tile_expert.shape[0]
    steps = EXPERT_FF_STEPS
    fc = wg.shape[2] // steps

    col_map = lambda i, c, te, ch, nv, src: (te[i], 0, ch[i * steps + c])
    row_map = lambda i, c, te, ch, nv, src: (te[i], ch[i * steps + c], 0)
    grid_spec = pltpu.PrefetchScalarGridSpec(
        num_scalar_prefetch=4,
        grid=(n_tiles, steps),
        in_specs=[pl.BlockSpec(memory_space=pl.ANY),
                  pl.BlockSpec((None, D_MODEL, fc), col_map),
                  pl.BlockSpec((None, D_MODEL, fc), col_map),
                  pl.BlockSpec((None, fc, D_MODEL), row_map)],
        out_specs=pl.BlockSpec((tm * ROW_SPLIT, LANES), lambda i, c, te, ch, nv, src: (i, 0)),
        scratch_shapes=[pltpu.VMEM((2, tm * ROW_SPLIT, LANES), _F32),
                        pltpu.VMEM((tm, D_MODEL), _BF16),
                        pltpu.VMEM((tm, D_MODEL), _F32),
                        pltpu.SemaphoreType.DMA((2,))],
    )
    return pl.pallas_call(
        functools.partial(_expert_kernel, tm=tm, n_tiles=n_tiles, n_steps=steps),
        grid_spec=grid_spec,
        out_shape=jax.ShapeDtypeStruct((n_tiles * tm * ROW_SPLIT, LANES), _F32),
        compiler_params=_params(("arbitrary", "arbitrary"), 60),
        name="expert_ffn",
    )(tile_expert, step_chunk, n_valid, src_tok, h_rows, wg, wu, wd)


def _combine_kernel(p1_ref, p2_ref, y_hbm, x_ref, rf_ref, g_ref, o_ref, buf_ref, sem, *, tm, n_tiles):
    i = pl.program_id(0)
    rows = tm * ROW_SPLIT

    def start_pair(tile_base, slot, r):
        if isinstance(r, int):
            dst = pl.ds(r * ROW_SPLIT, ROW_SPLIT)
        else:
            dst = pl.ds(pl.multiple_of(r * ROW_SPLIT, ROW_SPLIT), ROW_SPLIT)
        for k, pos_ref in enumerate((p1_ref, p2_ref)):
            pos = pos_ref[tile_base + r]
            pltpu.make_async_copy(
                y_hbm.at[pl.ds(pl.multiple_of(pos * ROW_SPLIT, ROW_SPLIT), ROW_SPLIT), :],
                buf_ref.at[slot, k, dst, :], sem.at[slot]).start(priority=k)

    def wait_slot(slot):
        for k in range(2):
            pltpu.make_async_copy(y_hbm.at[pl.ds(0, rows), :], buf_ref.at[slot, k], sem.at[slot]).wait()

    @pl.when(i == 0)
    def _():
        def body(r, carry):
            start_pair(0, 0, r)
            return carry

        lax.fori_loop(0, tm, body, 0)

    for slot in range(2):
        @pl.when(i % 2 == slot)
        def _():
            wait_slot(slot)
            next_base = jnp.minimum(i + 1, n_tiles - 1) * tm
            for r in range(tm):
                start_pair(next_base, 1 - slot, r)
            rf = rf_ref[...]
            g1 = rf[:, 0:1]
            g2 = rf[:, 1:2]
            parts = []
            for s in range(ROW_SPLIT):
                y1 = buf_ref[slot, 0, pl.ds(s, tm, stride=ROW_SPLIT), :]
                y2 = buf_ref[slot, 1, pl.ds(s, tm, stride=ROW_SPLIT), :]
                parts.append(g1 * y1 + g2 * y2)
            f = jnp.concatenate(parts, axis=1)
            o_ref[...] = x_ref[...] + _rms(f, g_ref[...])

    @pl.when(i == n_tiles - 1)
    def _():
        wait_slot(n_tiles % 2)


def _combine(pos1, pos2, y_rows, x2, route_f, g):
    t = x2.shape[0]
    tm = TOKEN_TILE
    grid_spec = pltpu.PrefetchScalarGridSpec(
        num_scalar_prefetch=2,
        grid=(t // tm,),
        in_specs=[pl.BlockSpec(memory_space=pl.ANY),
                  pl.BlockSpec((tm, D_MODEL), lambda i, p1, p2: (i, 0)),
                  pl.BlockSpec((tm, route_f.shape[1]), lambda i, p1, p2: (i, 0)),
                  pl.BlockSpec(g.shape, lambda i, p1, p2: (0, 0))],
        out_specs=pl.BlockSpec((tm, D_MODEL), lambda i, p1, p2: (i, 0)),
        scratch_shapes=[pltpu.VMEM((2, 2, tm * ROW_SPLIT, LANES), _F32),
                        pltpu.SemaphoreType.DMA((2,))],
    )
    return pl.pallas_call(
        functools.partial(_combine_kernel, tm=tm, n_tiles=t // tm),
        grid_spec=grid_spec,
        out_shape=jax.ShapeDtypeStruct((t, D_MODEL), _F32),
        compiler_params=_params(("arbitrary",), 40),
        name="combine",
    )(pos1, pos2, y_rows, x2, route_f, g)


def _swap_halves(w):
    half = w.shape[-1] // 2
    return jnp.concatenate([w[..., half:], w[..., :half]], axis=-1)


def _layer_weights(w_in, w_uq, w_ukv, pool_w, w_out):
    depth = w_in.shape[0]
    kr = w_in[..., _C_KR:_C_KR + ROPE_DIM]
    krs = _swap_halves(kr)
    win = jnp.concatenate([w_in[..., :_C_KR], kr, kr, krs, krs, w_in[..., _C_KR + ROPE_DIM:]], axis=-1)
    q_rope = w_uq[..., NOPE_DIM:]
    wq = jnp.concatenate([w_uq[..., :NOPE_DIM], q_rope, _swap_halves(q_rope)], axis=-1)
    wq = wq.reshape(depth, Q_LORA, N_HEADS * HEAD_W)
    wkv = jnp.concatenate([w_ukv[..., :NOPE_DIM].reshape(depth, KV_LORA, N_HEADS * NOPE_DIM),
                           w_ukv[..., NOPE_DIM:].reshape(depth, KV_LORA, N_HEADS * V_DIM)], axis=-1)
    same_group = jnp.eye(len(POOL_WINDOWS), dtype=pool_w.dtype)[:, None, :, None]
    pw = (pool_w[:, :, :, None, :] * same_group).reshape(depth, POOL_WIDTH, POOL_WIDTH)
    return tuple(a.astype(_BF16) for a in (win, wq, wkv, pw, w_out))


def _rope_table(seq):
    inv = 1.0 / (ROPE_THETA ** (jnp.arange(0, ROPE_DIM, 2, dtype=_F32) / ROPE_DIM))
    ang = jnp.arange(seq, dtype=_F32)[:, None] * inv[None, :]
    cos, sin = jnp.cos(ang), jnp.sin(ang)
    cos2 = jnp.concatenate([cos, cos], axis=1)
    sin2 = jnp.concatenate([-sin, sin], axis=1)
    scale = (NOPE_DIM + ROPE_DIM) ** -0.5 * 1.4426950408889634
    qmul = jnp.concatenate([jnp.full((seq, NOPE_DIM), scale, _F32), scale * cos2, scale * sin2], axis=1)
    return jnp.concatenate([qmul, cos2, cos2, sin2, sin2], axis=1)


def _routing_plan(route_i, counts, t):
    tm = EXPERT_TILE
    n_tiles = (2 * t) // tm + N_EXPERTS
    cnt = counts[:, 0].astype(jnp.int32)
    padded = ((cnt + tm - 1) // tm) * tm
    ends = jnp.cumsum(padded)
    base = ends - padded
    n_valid = (ends[-1] // tm).astype(jnp.int32)
    tile = jnp.arange(n_tiles, dtype=jnp.int32)
    tile_expert = jnp.minimum(jnp.sum((tile * tm)[:, None] >= ends[None, :], axis=1), N_EXPERTS - 1)
    first_tile = jnp.sum(jnp.where(tile_expert[:, None] == jnp.arange(N_EXPERTS)[None, :],
                                   (base // tm)[None, :], 0), axis=1)
    backward = ((tile - first_tile) % 2 == 1)[:, None]
    steps = jnp.arange(EXPERT_FF_STEPS, dtype=jnp.int32)[None, :]
    step_chunk = jnp.where(backward, EXPERT_FF_STEPS - 1 - steps, steps)
    used = (tile < n_valid)[:, None]
    last = n_valid - 1
    tile_expert = jnp.where(used[:, 0], tile_expert, tile_expert[last])
    step_chunk = jnp.where(used, step_chunk, step_chunk[last, EXPERT_FF_STEPS - 1])
    e1, e2, pos1, pos2 = (route_i[k] for k in range(4))
    for k in range(N_EXPERTS):
        pos1 = pos1 + jnp.where(e1 == k, base[k], 0)
        pos2 = pos2 + jnp.where(e2 == k, base[k], 0)
    src = _invert_positions(pos1, pos2, n_tiles * tm)
    return (tile_expert.astype(jnp.int32), step_chunk.reshape(-1).astype(jnp.int32), n_valid.reshape(1),
            src, pos1, pos2)


def _invert_kernel(p1_ref, p2_ref, zeros_hbm, src_ref, *, unroll):
    pltpu.sync_copy(zeros_hbm, src_ref)

    def fill(j, carry):
        for u in range(unroll):
            tok = j * unroll + u
            src_ref[p1_ref[tok]] = tok
            src_ref[p2_ref[tok]] = tok
        return carry

    lax.fori_loop(0, p1_ref.shape[0] // unroll, fill, 0)


def _invert_positions(pos1, pos2, n_slots):
    unroll = 8
    assert n_slots % unroll == 0 and pos1.shape[0] % unroll == 0
    smem = pl.BlockSpec(memory_space=pltpu.SMEM)
    return pl.pallas_call(
        functools.partial(_invert_kernel, unroll=unroll),
        in_specs=[smem, smem, pl.BlockSpec(memory_space=pl.ANY)],
        out_specs=smem,
        out_shape=jax.ShapeDtypeStruct((n_slots,), jnp.int32),
        name="invert_positions",
    )(pos1, pos2, jnp.zeros((n_slots,), jnp.int32))


def kernel(x, attn_pre_g, attn_post_g, ffn_pre_g, ffn_post_g, w_in, q_norm_g, kv_norm_g, w_uq, w_ukv,
           pool_w, pool_scale, w_out, w_gate_d, w_up_d, w_down_d, w_router, w_gate_e, w_up_e, w_down_e):
    b, seq, d = x.shape
    t = b * seq
    depth = w_in.shape[0]
    assert d == D_MODEL and seq % WIDE_TOKEN_TILE == 0 and seq % ATTN_TILE == 0
    tab = _rope_table(seq)
    win, wq, wkv, pw, wo = _layer_weights(w_in, w_uq, w_ukv, pool_w, w_out)
    rows = lambda a: a.reshape(depth, 1, -1)
    attn_pre_g, attn_post_g, ffn_pre_g, ffn_post_g, q_norm_g, kv_norm_g, pool_scale = map(
        rows, (attn_pre_g, attn_post_g, ffn_pre_g, ffn_post_g, q_norm_g, kv_norm_g, pool_scale))
    x2 = x.reshape(t, d)
    for l in range(depth):
        q, k, v, p = _in_proj(x2, attn_pre_g, win, q_norm_g, kv_norm_g, wq, wkv, tab, pw, pool_scale,
                              layer=l, seq=seq)
        i = l // 2
        if l % 2 == 1:
            casts = ((w_up_e[i].reshape(-1, w_up_e.shape[-1]),) + CAST_HEADS_WIDE,
                     (w_down_e[i].reshape(-1, d),) + CAST_HEADS_REST)
        else:
            casts = tuple((w,) + CAST_HEADS_REST for w in (w_gate_d[i], w_up_d[i], w_down_d[i]))
            if l + 1 < depth:
                casts += ((w_gate_e[(l + 1) // 2].reshape(-1, w_gate_e.shape[-1]),) + CAST_HEADS_WIDE,)
        a, converted = _attention(q.reshape(b, seq, -1), k.reshape(b, seq, -1), v.reshape(b, seq, -1), casts)
        a2 = a.reshape(t, -1)
        if l % 2 == 0:
            x2 = _out_proj(a2, p, x2, wo, attn_post_g, ffn_pre_g, layer=l,
                           dense=tuple(converted[0:3]) + (ffn_post_g,))
            next_gate = converted[3].reshape(w_gate_e.shape[1:]) if len(converted) > 3 else None
        else:
            wr_t = w_router[i].T
            wr_hi = wr_t.astype(_BF16)
            wr = jnp.concatenate([wr_hi, (wr_t - wr_hi.astype(_F32)).astype(_BF16)], axis=0)
            x2, h_rows, route_i, route_f, counts = _out_proj(
                a2, p, x2, wo, attn_post_g, ffn_pre_g, layer=l, w_router=wr)
            tile_expert, step_chunk, n_valid, src, pos1, pos2 = _routing_plan(route_i, counts, t)
            y_rows = _expert_ffn(tile_expert, step_chunk, n_valid, src, h_rows, next_gate,
                                 converted[0].reshape(w_up_e.shape[1:]),
                                 converted[1].reshape(w_down_e.shape[1:]))
            x2 = _combine(pos1, pos2, y_rows, x2, route_f[0:2].T, ffn_post_g[l])
    return x2.reshape(b, seq, d)
```

```python
import functools

import jax
import jax.numpy as jnp
from jax import lax
from jax.experimental import pallas as pl
from jax.experimental.pallas import tpu as pltpu

D_MODEL = 1024
CHUNK = 64
EPS = 1e-6
N_HEADS = 6
NOPE_DIM = 128
ROPE_DIM = 64
V_DIM = 128
Q_LORA = 384
KV_LORA = 256
ROPE_THETA = 10000.0
POOL_WINDOWS = (2, 4, 8, 16)
POOL_GROUP_DIM = 64
POOL_WIDTH = len(POOL_WINDOWS) * POOL_GROUP_DIM
N_EXPERTS = 8
MAX_WINDOW = max(POOL_WINDOWS)

LANES = 128
SUBLANES = 8
HEAD_W = 2 * LANES
ROW_SPLIT = D_MODEL // LANES

_C_Q = 0
_C_KV = Q_LORA
_C_KR = Q_LORA + KV_LORA
_C_KRS = _C_KR + LANES
_C_POOL = _C_KRS + LANES
IN_AUG = _C_POOL + POOL_WIDTH

TOKEN_TILE = 512
WIDE_TOKEN_TILE = 1024
IN_SUB_ROWS = 256
ATTN_TILE = 256
EXPERT_TILE = 512
EXPERT_FF_STEPS = 1
EXPERT_FF_SUB = 256
CAST_HEADS_WIDE = (0, 4)
CAST_HEADS_REST = (4, 6)

_F32 = jnp.float32
_BF16 = jnp.bfloat16
_MIB = 1024 * 1024


def _rms(x, g):
    return x * lax.rsqrt(jnp.mean(x * x, axis=-1, keepdims=True) + EPS) * g


def _silu_mul(g, u):
    hg = 0.5 * g
    return (hg + hg * jnp.tanh(hg)) * u


def _params(sem, vmem_mib):
    return pltpu.CompilerParams(dimension_semantics=sem, vmem_limit_bytes=vmem_mib * _MIB)


def _in_kernel(x_ref, g_ref, win_ref, qg_ref, kvg_ref, wq_ref, wkv_ref, tab_ref, pw_ref, ps_ref,
               q_ref, k_ref, v_ref, p_ref, ext_ref, *, tm, tiles_per_seq):
    i = pl.program_id(0)

    @pl.when(i % tiles_per_seq == 0)
    def _():
        ext_ref[0:MAX_WINDOW, :] = jnp.zeros((MAX_WINDOW, POOL_WIDTH), _F32)

    for r0 in range(0, tm, IN_SUB_ROWS):
        rows = slice(r0, r0 + IN_SUB_ROWS)
        h = _rms(x_ref[rows, :], g_ref[...]).astype(_BF16)
        z = jnp.dot(h, win_ref[...], preferred_element_type=_F32)
        cq = _rms(z[:, _C_Q:_C_Q + Q_LORA], qg_ref[...]).astype(_BF16)
        ckv = _rms(z[:, _C_KV:_C_KV + KV_LORA], kvg_ref[...]).astype(_BF16)
        q = jnp.dot(cq, wq_ref[...], preferred_element_type=_F32)
        kv = jnp.dot(ckv, wkv_ref[...], preferred_element_type=_F32)
        tab = tab_ref[rows, :]
        qmul = tab[:, 0:HEAD_W]
        cc = tab[:, HEAD_W:HEAD_W + LANES]
        ss = tab[:, HEAD_W + LANES:HEAD_W + 2 * LANES]
        krr = (z[:, _C_KR:_C_KR + LANES] * cc + z[:, _C_KRS:_C_KRS + LANES] * ss).astype(_BF16)
        for hd in range(N_HEADS):
            q_ref[rows, hd * HEAD_W:(hd + 1) * HEAD_W] = (q[:, hd * HEAD_W:(hd + 1) * HEAD_W] * qmul).astype(_BF16)
            k_ref[rows, hd * HEAD_W:hd * HEAD_W + NOPE_DIM] = kv[:, hd * NOPE_DIM:(hd + 1) * NOPE_DIM].astype(_BF16)
            k_ref[rows, hd * HEAD_W + NOPE_DIM:(hd + 1) * HEAD_W] = krr
        v_ref[rows, :] = kv[:, N_HEADS * NOPE_DIM:].astype(_BF16)
        ext_ref[MAX_WINDOW + r0:MAX_WINDOW + r0 + IN_SUB_ROWS, :] = z[:, _C_POOL:_C_POOL + POOL_WIDTH]

    u = ext_ref[MAX_WINDOW:MAX_WINDOW + tm, :]

    def window_sum(k0, k1, lo):
        acc = ext_ref[MAX_WINDOW - k0:MAX_WINDOW - k0 + tm, lo:lo + LANES]
        for k in range(k0 + 1, k1):
            acc = acc + ext_ref[MAX_WINDOW - k:MAX_WINDOW - k + tm, lo:lo + LANES]
        return acc

    t_pos = (i % tiles_per_seq) * tm + lax.broadcasted_iota(jnp.int32, (tm, LANES), 0)
    cnt = (t_pos + 1).astype(_F32)
    first_group = lax.broadcasted_iota(jnp.int32, (tm, LANES), 1) < POOL_GROUP_DIM
    w0, w1, w2, w3 = POOL_WINDOWS
    s_a = window_sum(0, w0, 0)
    s_b = s_a + window_sum(w0, w1, 0)
    pooled_lo = jnp.where(first_group, s_a / jnp.minimum(cnt, float(w0)), s_b / jnp.minimum(cnt, float(w1)))
    s_c = window_sum(0, w2, LANES)
    s_d = s_c + window_sum(w2, w3, LANES)
    pooled_hi = jnp.where(first_group, s_c / jnp.minimum(cnt, float(w2)), s_d / jnp.minimum(cnt, float(w3)))
    d = jnp.concatenate([pooled_lo - u[:, :LANES], pooled_hi - u[:, LANES:]], axis=1).astype(_BF16)
    y = jnp.dot(d, pw_ref[...], preferred_element_type=_F32) * ps_ref[...]
    p_ref[...] = y.astype(_BF16)
    ext_ref[0:MAX_WINDOW, :] = ext_ref[tm:tm + MAX_WINDOW, :]


def _of_layer(a, layer):
    return pl.BlockSpec((None,) + a.shape[1:], lambda i: (layer,) + (0,) * (a.ndim - 1))


def _in_proj(x2, g, win, qg, kvg, wq, wkv, tab, pw, ps, *, layer, seq):
    t = x2.shape[0]
    tm = WIDE_TOKEN_TILE
    tps = seq // tm
    full = lambda a: _of_layer(a, layer)
    return pl.pallas_call(
        functools.partial(_in_kernel, tm=tm, tiles_per_seq=tps),
        grid=(t // tm,),
        in_specs=[pl.BlockSpec((tm, D_MODEL), lambda i: (i, 0)),
                  full(g), full(win), full(qg), full(kvg), full(wq), full(wkv),
                  pl.BlockSpec((tm, tab.shape[1]), lambda i: (i % tps, 0)),
                  full(pw), full(ps)],
        out_specs=[pl.BlockSpec((tm, N_HEADS * HEAD_W), lambda i: (i, 0)),
                   pl.BlockSpec((tm, N_HEADS * HEAD_W), lambda i: (i, 0)),
                   pl.BlockSpec((tm, N_HEADS * V_DIM), lambda i: (i, 0)),
                   pl.BlockSpec((tm, POOL_WIDTH), lambda i: (i, 0))],
        out_shape=[jax.ShapeDtypeStruct((t, N_HEADS * HEAD_W), _BF16),
                   jax.ShapeDtypeStruct((t, N_HEADS * HEAD_W), _BF16),
                   jax.ShapeDtypeStruct((t, N_HEADS * V_DIM), _BF16),
                   jax.ShapeDtypeStruct((t, POOL_WIDTH), _BF16)],
        scratch_shapes=[pltpu.VMEM((tm + MAX_WINDOW, POOL_WIDTH), _F32)],
        compiler_params=_params(("arbitrary",), 48),
        name="in_proj",
    )(x2, g, win, qg, kvg, wq, wkv, tab, pw, ps)


def _attn_kernel(*refs, tq, nq, cast_heads):
    n_cast = len(cast_heads)
    q_ref, k_ref, v_ref = refs[0:3]
    o_ref = refs[3 + n_cast]
    vaug_ref = refs[-1]
    head = pl.program_id(1)
    for (lo, hi), src_ref, dst_ref in zip(cast_heads, refs[3:3 + n_cast], refs[4 + n_cast:4 + 2 * n_cast]):
        @pl.when(jnp.logical_and(head >= lo, head < hi))
        def _():
            dst_ref[...] = src_ref[...].astype(_BF16)

    seq = v_ref.shape[0]
    vaug_ref[:, 0:V_DIM] = v_ref[...]
    ones_col = lax.broadcasted_iota(jnp.int32, (seq, LANES), 1) == 0
    vaug_ref[:, V_DIM:V_DIM + LANES] = jnp.where(ones_col, 1.0, 0.0).astype(_BF16)
    row_chunk = lax.broadcasted_iota(jnp.int32, (tq, tq), 0) // CHUNK
    col_chunk = lax.broadcasted_iota(jnp.int32, (tq, tq), 1) // CHUNK
    visible = col_chunk <= row_chunk
    nt = (((1,), (1,)), ((), ()))
    def scores(i):
        d0 = i * tq
        s = lax.dot_general(q_ref[d0:d0 + tq, :], k_ref[0:d0 + tq, :], nt, preferred_element_type=_F32)
        s_diag = jnp.where(visible, s[:, d0:], -jnp.inf)
        return s_diag if i == 0 else jnp.concatenate([s[:, :d0], s_diag], axis=1)

    order = list(range(nq - 1, -1, -1))
    s_next = scores(order[0])
    for n, i in enumerate(order):
        s = s_next
        d0 = i * tq
        p = jnp.exp2(s - jnp.max(s, axis=1, keepdims=True)).astype(_BF16)
        if n + 1 < nq:
            s_next = scores(order[n + 1])
        o = jnp.dot(p, vaug_ref[0:d0 + tq, :], preferred_element_type=_F32)
        o_ref[d0:d0 + tq, :] = (o[:, 0:V_DIM] / o[:, V_DIM:V_DIM + 1]).astype(_BF16)


def _attention(q, k, v, casts=()):
    b, seq, _ = q.shape
    tq = ATTN_TILE
    head = lambda w: pl.BlockSpec((None, seq, w), lambda bb, hh: (bb, 0, hh))
    cast_in, cast_out, cast_shapes = [], [], []
    for w, lo, hi, *cols in casts:
        n_h = hi - lo
        rows = w.shape[0]
        c0, c1 = cols if cols else (0, w.shape[1])
        assert rows % (b * n_h * 2 * SUBLANES) == 0 and c0 % (c1 - c0) == 0
        row_block = lambda bb, hh, lo=lo, n_h=n_h: bb * n_h + jnp.clip(hh - lo, 0, n_h - 1)
        block = (rows // (b * n_h), c1 - c0)
        cast_in.append(pl.BlockSpec(block, lambda bb, hh, f=row_block, cb=c0 // (c1 - c0): (f(bb, hh), cb)))
        cast_out.append(pl.BlockSpec(block, lambda bb, hh, f=row_block: (f(bb, hh), 0)))
        cast_shapes.append(jax.ShapeDtypeStruct((rows, c1 - c0), _BF16))
    outs = pl.pallas_call(
        functools.partial(_attn_kernel, tq=tq, nq=seq // tq, cast_heads=tuple((c[1], c[2]) for c in casts)),
        grid=(b, N_HEADS),
        in_specs=[head(HEAD_W), head(HEAD_W), head(V_DIM)] + cast_in,
        out_specs=[head(V_DIM)] + cast_out,
        out_shape=[jax.ShapeDtypeStruct((b, seq, N_HEADS * V_DIM), _BF16)] + cast_shapes,
        scratch_shapes=[pltpu.VMEM((seq, V_DIM + LANES), _BF16)],
        compiler_params=_params(("arbitrary", "arbitrary"), 56),
        name="attention",
    )(q, k, v, *(c[0] for c in casts))
    return outs[0], outs[1:]


def _mix_residual(a_ref, p_ref, x_ref, wa_ref, wp_ref, g1_ref):
    mix = (jnp.dot(a_ref[...], wa_ref[...], preferred_element_type=_F32)
           + jnp.dot(p_ref[...], wp_ref[...], preferred_element_type=_F32))
    return x_ref[...] + _rms(mix, g1_ref[...])


def _out_dense_kernel(a_ref, p_ref, x_ref, wa_ref, wp_ref, g1_ref, g2_ref,
                      wg_ref, wu_ref, wd_ref, g3_ref, o_ref):
    xn = _mix_residual(a_ref, p_ref, x_ref, wa_ref, wp_ref, g1_ref)
    h = _rms(xn, g2_ref[...]).astype(_BF16)
    gate = jnp.dot(h, wg_ref[...], preferred_element_type=_F32)
    up = jnp.dot(h, wu_ref[...], preferred_element_type=_F32)
    hm = _silu_mul(gate, up).astype(_BF16)
    f = jnp.dot(hm, wd_ref[...], preferred_element_type=_F32)
    o_ref[...] = xn + _rms(f, g3_ref[...])


def _out_moe_kernel(a_ref, p_ref, x_ref, wa_ref, wp_ref, g1_ref, g2_ref, wr_ref,
                    xo_ref, ho_ref, ri_ref, rf_ref, cnt_ref, carry_ref, *, tm):
    @pl.when(pl.program_id(0) == 0)
    def _():
        carry_ref[...] = jnp.zeros_like(carry_ref)

    xn = _mix_residual(a_ref, p_ref, x_ref, wa_ref, wp_ref, g1_ref)
    xo_ref[...] = xn
    h = _rms(xn, g2_ref[...])
    for s in range(ROW_SPLIT):
        ho_ref[pl.ds(s, tm, stride=ROW_SPLIT), :] = h[:, s * LANES:(s + 1) * LANES]

    h_hi = h.astype(_BF16)
    h_lo = (h - h_hi.astype(_F32)).astype(_BF16)
    wr = wr_ref[...]
    nt = (((1,), (1,)), ((), ()))
    by_hi = lax.dot_general(wr, h_hi, nt, preferred_element_type=_F32)
    by_lo = lax.dot_general(wr, h_lo, nt, preferred_element_type=_F32)
    logits = by_hi[0:N_EXPERTS] + by_hi[N_EXPERTS:] + by_lo[0:N_EXPERTS]
    sub = lax.broadcasted_iota(jnp.int32, (N_EXPERTS, tm), 0)
    v1 = jnp.max(logits, axis=0, keepdims=True)
    e1 = jnp.min(jnp.where(logits == v1, sub, N_EXPERTS), axis=0, keepdims=True)
    rest = jnp.where(sub == e1, -jnp.inf, logits)
    v2 = jnp.max(rest, axis=0, keepdims=True)
    e2 = jnp.min(jnp.where(rest == v2, sub, N_EXPERTS), axis=0, keepdims=True)
    ex = jnp.exp(v2 - v1)
    gate1 = 1.0 / (1.0 + ex)
    gate2 = ex * gate1

    sel1 = sub == e1
    sel2 = sub == e2
    member = jnp.logical_or(sel1, sel2).astype(_F32)
    earlier = (lax.broadcasted_iota(jnp.int32, (tm, tm), 0)
               < lax.broadcasted_iota(jnp.int32, (tm, tm), 1)).astype(_BF16)
    member_lhs = jnp.concatenate([member, jnp.zeros_like(member)], axis=0).astype(_BF16)
    carry = carry_ref[:, 0:1]
    before = jnp.dot(member_lhs, earlier, preferred_element_type=_F32)[0:N_EXPERTS] + carry
    r1 = jnp.sum(jnp.where(sel1, before, 0.0), axis=0, keepdims=True).astype(jnp.int32)
    r2 = jnp.sum(jnp.where(sel2, before, 0.0), axis=0, keepdims=True).astype(jnp.int32)
    total = carry + jnp.sum(member, axis=1, keepdims=True)
    carry_ref[...] = jnp.broadcast_to(total, carry_ref.shape)
    cnt_ref[...] = jnp.broadcast_to(total, cnt_ref.shape)

    ri_ref[...] = jnp.where(sub == 0, e1, jnp.where(sub == 1, e2, jnp.where(sub == 2, r1, r2)))
    rf_ref[...] = jnp.where(sub == 0, gate1, gate2)


def _out_proj(a2, p2, x2, wo, g1, g2, *, layer, w_router=None, dense=None):
    t = x2.shape[0]
    tm = TOKEN_TILE if w_router is None else WIDE_TOKEN_TILE
    full = lambda a: pl.BlockSpec(a.shape, lambda i: (0,) * a.ndim)
    of_layer = lambda a: _of_layer(a, layer)
    attn_w, pool_w = a2.shape[1], p2.shape[1]
    assert attn_w % pool_w == 0
    wa_spec = pl.BlockSpec((None, attn_w, D_MODEL), lambda i: (layer, 0, 0))
    wp_spec = pl.BlockSpec((None, pool_w, D_MODEL), lambda i: (layer, attn_w // pool_w, 0))
    resident = lambda a: pl.BlockSpec(a.shape, lambda i: (0,) * a.ndim, pipeline_mode=pl.Buffered(1))
    row = lambda w: pl.BlockSpec((tm, w), lambda i: (i, 0))
    in_specs = [row(attn_w), row(pool_w), row(D_MODEL), wa_spec, wp_spec, of_layer(g1), of_layer(g2)]
    if w_router is None:
        wg, wu, wd, g3 = dense
        return pl.pallas_call(
            _out_dense_kernel,
            grid=(t // tm,),
            in_specs=in_specs + [resident(wg), resident(wu), resident(wd), of_layer(g3)],
            out_specs=row(D_MODEL),
            out_shape=jax.ShapeDtypeStruct((t, D_MODEL), _F32),
            compiler_params=_params(("arbitrary",), 56),
            name="out_proj_dense_ffn",
        )(a2, p2, x2, wo, wo, g1, g2, wg, wu, wd, g3)
    return pl.pallas_call(
        functools.partial(_out_moe_kernel, tm=tm),
        grid=(t // tm,),
        in_specs=in_specs + [full(w_router)],
        out_specs=[row(D_MODEL),
                   pl.BlockSpec((tm * ROW_SPLIT, LANES), lambda i: (i, 0)),
                   pl.BlockSpec((N_EXPERTS, tm), lambda i: (0, i)),
                   pl.BlockSpec((N_EXPERTS, tm), lambda i: (0, i)),
                   pl.BlockSpec((N_EXPERTS, LANES), lambda i: (0, 0))],
        out_shape=[jax.ShapeDtypeStruct((t, D_MODEL), _F32),
                   jax.ShapeDtypeStruct((t * ROW_SPLIT, LANES), _F32),
                   jax.ShapeDtypeStruct((N_EXPERTS, t), jnp.int32),
                   jax.ShapeDtypeStruct((N_EXPERTS, t), _F32),
                   jax.ShapeDtypeStruct((N_EXPERTS, LANES), _F32)],
        scratch_shapes=[pltpu.VMEM((N_EXPERTS, LANES), _F32)],
        compiler_params=_params(("arbitrary",), 48),
        name="out_proj_router",
    )(a2, p2, x2, wo, wo, g1, g2, w_router)


def _expert_kernel(te_ref, ch_ref, nv_ref, src_ref, h_hbm, wg_ref, wu_ref, wda_ref, wdb_ref, y_ref,
                   buf_ref, xb_ref, acc_ref, sem, *, tm, n_tiles, n_steps):
    i = pl.program_id(0)
    c = pl.program_id(1)
    last_c = n_steps - 1
    nv = nv_ref[0]
    rows = tm * ROW_SPLIT

    def start_row(tile_base, slot, r):
        tok = src_ref[tile_base + r]
        pltpu.make_async_copy(
            h_hbm.at[pl.ds(pl.multiple_of(tok * ROW_SPLIT, ROW_SPLIT), ROW_SPLIT), :],
            buf_ref.at[slot, pl.ds(pl.multiple_of(r * ROW_SPLIT, ROW_SPLIT), ROW_SPLIT), :],
            sem.at[slot]).start()

    def wait_slot(slot):
        pltpu.make_async_copy(h_hbm.at[pl.ds(0, rows), :], buf_ref.at[slot], sem.at[slot]).wait()

    @pl.when(jnp.logical_and(i == 0, c == 0))
    def _():
        def body(r, carry):
            start_row(0, 0, r)
            return carry

        lax.fori_loop(0, tm, body, 0)

    @pl.when(jnp.logical_and(c == 0, i == nv))
    def _():
        wait_slot(i % 2)

    def swiglu_part(lo=0, hi=None):
        x = xb_ref[...]
        part = None
        for k0 in range(lo, wg_ref.shape[1] if hi is None else hi, EXPERT_FF_SUB):
            k1 = k0 + EXPERT_FF_SUB
            gate = jnp.dot(x, wg_ref[:, k0:k1], preferred_element_type=_F32)
            up = jnp.dot(x, wu_ref[:, k0:k1], preferred_element_type=_F32)
            hm = _silu_mul(gate, up).astype(_BF16)
            down = jnp.concatenate([jnp.dot(hm, w_ref[k0:k1, :], preferred_element_type=_F32)
                                    for w_ref in (wda_ref, wdb_ref)], axis=1)
            part = down if part is None else part + down
        return part

    def take_rows_and_look_ahead():
        slot = i % 2
        wait_slot(slot)
        for s in range(ROW_SPLIT):
            xb_ref[:, s * LANES:(s + 1) * LANES] = (
                buf_ref[slot, pl.ds(s, tm, stride=ROW_SPLIT), :].astype(_BF16))
        next_base = jnp.minimum(i + 1, n_tiles - 1) * tm
        for r in range(tm):
            start_row(next_base, 1 - slot, r)

    def store_rows(total):
        for s in range(ROW_SPLIT):
            y_ref[pl.ds(s, tm, stride=ROW_SPLIT), :] = total[:, s * LANES:(s + 1) * LANES]

    if n_steps == 1:
        half = wg_ref.shape[1] // 2

        @pl.when(i < nv)
        def _():
            take_rows_and_look_ahead()
            acc_ref[...] = swiglu_part(0, half)

        @pl.when(jnp.logical_and(i < nv, c == last_c))
        def _():
            store_rows(acc_ref[...] + swiglu_part(half))
    else:

        @pl.when(jnp.logical_and(i < nv, c == 0))
        def _():
            take_rows_and_look_ahead()
            acc_ref[...] = swiglu_part()

        if n_steps > 2:
            @pl.when(jnp.logical_and(i < nv, jnp.logical_and(c > 0, c < last_c)))
            def _():
                acc_ref[...] += swiglu_part()

        @pl.when(jnp.logical_and(i < nv, c == last_c))
        def _():
            store_rows(acc_ref[...] + swiglu_part())

    @pl.when(jnp.logical_and(c == last_c, i >= nv))
    def _():
        y_ref[...] = jnp.zeros_like(y_ref)

    @pl.when(jnp.logical_and(jnp.logical_and(i == n_tiles - 1, c == last_c), nv == n_tiles))
    def _():
        wait_slot(n_tiles % 2)


def _expert_ffn(tile_expert, step_chunk, n_valid, src_tok, h_rows, wg, wu, wd_halves):
    tm = EXPERT_TILE
    n_tiles = tile_expert.shape[0]
    steps = EXPERT_FF_STEPS
    fc = wg.shape[2] // steps

    col_map = lambda i, c, te, ch, nv, src: (te[i], 0, ch[i * steps + c])
    row_map = lambda i, c, te, ch, nv, src: (te[i], ch[i * steps + c], 0)
    grid_spec = pltpu.PrefetchScalarGridSpec(
        num_scalar_prefetch=4,
        grid=(n_tiles, steps),
        in_specs=[pl.BlockSpec(memory_space=pl.ANY),
                  pl.BlockSpec((None, D_MODEL, fc), col_map),
                  pl.BlockSpec((None, D_MODEL, fc), col_map),
                  pl.BlockSpec((None, fc, D_MODEL // 2), row_map),
                  pl.BlockSpec((None, fc, D_MODEL // 2), row_map)],
        out_specs=pl.BlockSpec((tm * ROW_SPLIT, LANES), lambda i, c, te, ch, nv, src: (i, 0)),
        scratch_shapes=[pltpu.VMEM((2, tm * ROW_SPLIT, LANES), _F32),
                        pltpu.VMEM((tm, D_MODEL), _BF16),
                        pltpu.VMEM((tm, D_MODEL), _F32),
                        pltpu.SemaphoreType.DMA((2,))],
    )
    return pl.pallas_call(
        functools.partial(_expert_kernel, tm=tm, n_tiles=n_tiles, n_steps=steps),
        grid_spec=grid_spec,
        out_shape=jax.ShapeDtypeStruct((n_tiles * tm * ROW_SPLIT, LANES), _F32),
        compiler_params=_params(("arbitrary", "arbitrary"), 60),
        name="expert_ffn",
    )(tile_expert, step_chunk, n_valid, src_tok, h_rows, wg, wu, *wd_halves)


def _combine_kernel(p1_ref, p2_ref, y_hbm, x_ref, rf_ref, g_ref, o_ref, buf_ref, sem, *, tm, n_tiles):
    i = pl.program_id(0)
    rows = tm * ROW_SPLIT

    def start_pair(tile_base, slot, r):
        if isinstance(r, int):
            dst = pl.ds(r * ROW_SPLIT, ROW_SPLIT)
        else:
            dst = pl.ds(pl.multiple_of(r * ROW_SPLIT, ROW_SPLIT), ROW_SPLIT)
        for k, pos_ref in enumerate((p1_ref, p2_ref)):
            pos = pos_ref[tile_base + r]
            pltpu.make_async_copy(
                y_hbm.at[pl.ds(pl.multiple_of(pos * ROW_SPLIT, ROW_SPLIT), ROW_SPLIT), :],
                buf_ref.at[slot, k, dst, :], sem.at[slot]).start(priority=k)

    def wait_slot(slot):
        for k in range(2):
            pltpu.make_async_copy(y_hbm.at[pl.ds(0, rows), :], buf_ref.at[slot, k], sem.at[slot]).wait()

    @pl.when(i == 0)
    def _():
        def body(r, carry):
            start_pair(0, 0, r)
            return carry

        lax.fori_loop(0, tm, body, 0)

    for slot in range(2):
        @pl.when(i % 2 == slot)
        def _():
            wait_slot(slot)
            next_base = jnp.minimum(i + 1, n_tiles - 1) * tm
            for r in range(tm):
                start_pair(next_base, 1 - slot, r)
            rf = rf_ref[...]
            g1 = rf[:, 0:1]
            g2 = rf[:, 1:2]
            parts = []
            for s in range(ROW_SPLIT):
                y1 = buf_ref[slot, 0, pl.ds(s, tm, stride=ROW_SPLIT), :]
                y2 = buf_ref[slot, 1, pl.ds(s, tm, stride=ROW_SPLIT), :]
                parts.append(g1 * y1 + g2 * y2)
            f = jnp.concatenate(parts, axis=1)
            o_ref[...] = x_ref[...] + _rms(f, g_ref[...])

    @pl.when(i == n_tiles - 1)
    def _():
        wait_slot(n_tiles % 2)


def _combine(pos1, pos2, y_rows, x2, route_f, g):
    t = x2.shape[0]
    tm = TOKEN_TILE
    grid_spec = pltpu.PrefetchScalarGridSpec(
        num_scalar_prefetch=2,
        grid=(t // tm,),
        in_specs=[pl.BlockSpec(memory_space=pl.ANY),
                  pl.BlockSpec((tm, D_MODEL), lambda i, p1, p2: (i, 0)),
                  pl.BlockSpec((tm, route_f.shape[1]), lambda i, p1, p2: (i, 0)),
                  pl.BlockSpec(g.shape, lambda i, p1, p2: (0, 0))],
        out_specs=pl.BlockSpec((tm, D_MODEL), lambda i, p1, p2: (i, 0)),
        scratch_shapes=[pltpu.VMEM((2, 2, tm * ROW_SPLIT, LANES), _F32),
                        pltpu.SemaphoreType.DMA((2,))],
    )
    return pl.pallas_call(
        functools.partial(_combine_kernel, tm=tm, n_tiles=t // tm),
        grid_spec=grid_spec,
        out_shape=jax.ShapeDtypeStruct((t, D_MODEL), _F32),
        compiler_params=_params(("arbitrary",), 40),
        name="combine",
    )(pos1, pos2, y_rows, x2, route_f, g)


def _swap_halves(w):
    half = w.shape[-1] // 2
    return jnp.concatenate([w[..., half:], w[..., :half]], axis=-1)


def _layer_weights(w_in, w_uq, w_ukv, pool_w, w_out):
    depth = w_in.shape[0]
    kr = w_in[..., _C_KR:_C_KR + ROPE_DIM]
    krs = _swap_halves(kr)
    win = jnp.concatenate([w_in[..., :_C_KR], kr, kr, krs, krs, w_in[..., _C_KR + ROPE_DIM:]], axis=-1)
    q_rope = w_uq[..., NOPE_DIM:]
    wq = jnp.concatenate([w_uq[..., :NOPE_DIM], q_rope, _swap_halves(q_rope)], axis=-1)
    wq = wq.reshape(depth, Q_LORA, N_HEADS * HEAD_W)
    wkv = jnp.concatenate([w_ukv[..., :NOPE_DIM].reshape(depth, KV_LORA, N_HEADS * NOPE_DIM),
                           w_ukv[..., NOPE_DIM:].reshape(depth, KV_LORA, N_HEADS * V_DIM)], axis=-1)
    same_group = jnp.eye(len(POOL_WINDOWS), dtype=pool_w.dtype)[:, None, :, None]
    pw = (pool_w[:, :, :, None, :] * same_group).reshape(depth, POOL_WIDTH, POOL_WIDTH)
    return tuple(a.astype(_BF16) for a in (win, wq, wkv, pw, w_out))


def _rope_table(seq):
    inv = 1.0 / (ROPE_THETA ** (jnp.arange(0, ROPE_DIM, 2, dtype=_F32) / ROPE_DIM))
    ang = jnp.arange(seq, dtype=_F32)[:, None] * inv[None, :]
    cos, sin = jnp.cos(ang), jnp.sin(ang)
    cos2 = jnp.concatenate([cos, cos], axis=1)
    sin2 = jnp.concatenate([-sin, sin], axis=1)
    scale = (NOPE_DIM + ROPE_DIM) ** -0.5 * 1.4426950408889634
    qmul = jnp.concatenate([jnp.full((seq, NOPE_DIM), scale, _F32), scale * cos2, scale * sin2], axis=1)
    return jnp.concatenate([qmul, cos2, cos2, sin2, sin2], axis=1)


def _routing_plan(route_i, counts, t):
    tm = EXPERT_TILE
    n_tiles = (2 * t) // tm + N_EXPERTS
    cnt = counts[:, 0].astype(jnp.int32)
    padded = ((cnt + tm - 1) // tm) * tm
    ends = jnp.cumsum(padded)
    base = ends - padded
    n_valid = (ends[-1] // tm).astype(jnp.int32)
    tile = jnp.arange(n_tiles, dtype=jnp.int32)
    tile_expert = jnp.minimum(jnp.sum((tile * tm)[:, None] >= ends[None, :], axis=1), N_EXPERTS - 1)
    first_tile = jnp.sum(jnp.where(tile_expert[:, None] == jnp.arange(N_EXPERTS)[None, :],
                                   (base // tm)[None, :], 0), axis=1)
    backward = ((tile - first_tile) % 2 == 1)[:, None]
    steps = jnp.arange(EXPERT_FF_STEPS, dtype=jnp.int32)[None, :]
    step_chunk = jnp.where(backward, EXPERT_FF_STEPS - 1 - steps, steps)
    used = (tile < n_valid)[:, None]
    last = n_valid - 1
    tile_expert = jnp.where(used[:, 0], tile_expert, tile_expert[last])
    step_chunk = jnp.where(used, step_chunk, step_chunk[last, EXPERT_FF_STEPS - 1])
    e1, e2, pos1, pos2 = (route_i[k] for k in range(4))
    for k in range(N_EXPERTS):
        pos1 = pos1 + jnp.where(e1 == k, base[k], 0)
        pos2 = pos2 + jnp.where(e2 == k, base[k], 0)
    src = _invert_positions(pos1, pos2, n_tiles * tm)
    return (tile_expert.astype(jnp.int32), step_chunk.reshape(-1).astype(jnp.int32), n_valid.reshape(1),
            src, pos1, pos2)


def _invert_kernel(p1_ref, p2_ref, zeros_hbm, src_ref, *, unroll):
    pltpu.sync_copy(zeros_hbm, src_ref)

    def fill(j, carry):
        for u in range(unroll):
            tok = j * unroll + u
            src_ref[p1_ref[tok]] = tok
            src_ref[p2_ref[tok]] = tok
        return carry

    lax.fori_loop(0, p1_ref.shape[0] // unroll, fill, 0)


def _invert_positions(pos1, pos2, n_slots):
    unroll = 8
    assert n_slots % unroll == 0 and pos1.shape[0] % unroll == 0
    smem = pl.BlockSpec(memory_space=pltpu.SMEM)
    return pl.pallas_call(
        functools.partial(_invert_kernel, unroll=unroll),
        in_specs=[smem, smem, pl.BlockSpec(memory_space=pl.ANY)],
        out_specs=smem,
        out_shape=jax.ShapeDtypeStruct((n_slots,), jnp.int32),
        name="invert_positions",
    )(pos1, pos2, jnp.zeros((n_slots,), jnp.int32))


def kernel(x, attn_pre_g, attn_post_g, ffn_pre_g, ffn_post_g, w_in, q_norm_g, kv_norm_g, w_uq, w_ukv,
           pool_w, pool_scale, w_out, w_gate_d, w_up_d, w_down_d, w_router, w_gate_e, w_up_e, w_down_e):
    b, seq, d = x.shape
    t = b * seq
    depth = w_in.shape[0]
    assert d == D_MODEL and seq % WIDE_TOKEN_TILE == 0 and seq % ATTN_TILE == 0
    tab = _rope_table(seq)
    win, wq, wkv, pw, wo = _layer_weights(w_in, w_uq, w_ukv, pool_w, w_out)
    rows = lambda a: a.reshape(depth, 1, -1)
    attn_pre_g, attn_post_g, ffn_pre_g, ffn_post_g, q_norm_g, kv_norm_g, pool_scale = map(
        rows, (attn_pre_g, attn_post_g, ffn_pre_g, ffn_post_g, q_norm_g, kv_norm_g, pool_scale))
    x2 = x.reshape(t, d)
    for l in range(depth):
        q, k, v, p = _in_proj(x2, attn_pre_g, win, q_norm_g, kv_norm_g, wq, wkv, tab, pw, pool_scale,
                              layer=l, seq=seq)
        i = l // 2
        if l % 2 == 1:
            casts = ((w_up_e[i].reshape(-1, w_up_e.shape[-1]),) + CAST_HEADS_WIDE,
                     (w_down_e[i].reshape(-1, d),) + CAST_HEADS_REST + (d // 2, d))
        else:
            casts = tuple((w,) + CAST_HEADS_REST for w in (w_gate_d[i], w_up_d[i], w_down_d[i]))
            if l + 1 < depth:
                nxt = (l + 1) // 2
                casts += ((w_gate_e[nxt].reshape(-1, w_gate_e.shape[-1]),) + CAST_HEADS_WIDE,
                          (w_down_e[nxt].reshape(-1, d),) + CAST_HEADS_REST + (0, d // 2))
        a, converted = _attention(q.reshape(b, seq, -1), k.reshape(b, seq, -1), v.reshape(b, seq, -1), casts)
        a2 = a.reshape(t, -1)
        if l % 2 == 0:
            x2 = _out_proj(a2, p, x2, wo, attn_post_g, ffn_pre_g, layer=l,
                           dense=tuple(converted[0:3]) + (ffn_post_g,))
            if len(converted) > 3:
                next_gate = converted[3].reshape(w_gate_e.shape[1:])
                next_down_left = converted[4]
        else:
            wr_t = w_router[i].T
            wr_hi = wr_t.astype(_BF16)
            wr = jnp.concatenate([wr_hi, (wr_t - wr_hi.astype(_F32)).astype(_BF16)], axis=0)
            x2, h_rows, route_i, route_f, counts = _out_proj(
                a2, p, x2, wo, attn_post_g, ffn_pre_g, layer=l, w_router=wr)
            tile_expert, step_chunk, n_valid, src, pos1, pos2 = _routing_plan(route_i, counts, t)
            down_halves = [half.reshape(N_EXPERTS, -1, d // 2) for half in (next_down_left, converted[1])]
            y_rows = _expert_ffn(tile_expert, step_chunk, n_valid, src, h_rows, next_gate,
                                 converted[0].reshape(w_up_e.shape[1:]), down_halves)
            x2 = _combine(pos1, pos2, y_rows, x2, route_f[0:2].T, ffn_post_g[l])
    return x2.reshape(b, seq, d)
```

```python
import functools

import jax
import jax.numpy as jnp
from jax import lax
from jax.experimental import pallas as pl
from jax.experimental.pallas import tpu as pltpu

D_MODEL = 1024
CHUNK = 64
EPS = 1e-6
N_HEADS = 6
NOPE_DIM = 128
ROPE_DIM = 64
V_DIM = 128
Q_LORA = 384
KV_LORA = 256
ROPE_THETA = 10000.0
POOL_WINDOWS = (2, 4, 8, 16)
POOL_GROUP_DIM = 64
POOL_WIDTH = len(POOL_WINDOWS) * POOL_GROUP_DIM
N_EXPERTS = 8
MAX_WINDOW = max(POOL_WINDOWS)

LANES = 128
SUBLANES = 8
HEAD_W = 2 * LANES
ROW_SPLIT = D_MODEL // LANES

_C_Q = 0
_C_KV = Q_LORA
_C_KR = Q_LORA + KV_LORA
_C_KRS = _C_KR + LANES
_C_POOL = _C_KRS + LANES
IN_AUG = _C_POOL + POOL_WIDTH

TOKEN_TILE = 512
WIDE_TOKEN_TILE = 1024
IN_SUB_ROWS = 256
ATTN_TILE = 256
ATTN_HEADS_PER_STEP = 2
EXPERT_TILE = 512
EXPERT_FF_STEPS = 1
EXPERT_FF_SUB = 256
CAST_STEPS_WIDE = (0, 2)
CAST_STEPS_REST = (2, 3)

_F32 = jnp.float32
_BF16 = jnp.bfloat16
_MIB = 1024 * 1024


def _rms(x, g):
    return x * lax.rsqrt(jnp.mean(x * x, axis=-1, keepdims=True) + EPS) * g


def _silu_mul(g, u):
    hg = 0.5 * g
    return (hg + hg * jnp.tanh(hg)) * u


def _params(sem, vmem_mib):
    return pltpu.CompilerParams(dimension_semantics=sem, vmem_limit_bytes=vmem_mib * _MIB)


def _in_kernel(x_ref, g_ref, win_ref, qg_ref, kvg_ref, wq_ref, wkv_ref, tab_ref, pw_ref, ps_ref,
               q_ref, k_ref, v_ref, p_ref, ext_ref, *, tm, tiles_per_seq):
    i = pl.program_id(0)

    @pl.when(i % tiles_per_seq == 0)
    def _():
        ext_ref[0:MAX_WINDOW, :] = jnp.zeros((MAX_WINDOW, POOL_WIDTH), _F32)

    for r0 in range(0, tm, IN_SUB_ROWS):
        rows = slice(r0, r0 + IN_SUB_ROWS)
        h = _rms(x_ref[rows, :], g_ref[...]).astype(_BF16)
        z = jnp.dot(h, win_ref[...], preferred_element_type=_F32)
        cq = _rms(z[:, _C_Q:_C_Q + Q_LORA], qg_ref[...]).astype(_BF16)
        ckv = _rms(z[:, _C_KV:_C_KV + KV_LORA], kvg_ref[...]).astype(_BF16)
        q = jnp.dot(cq, wq_ref[...], preferred_element_type=_F32)
        kv = jnp.dot(ckv, wkv_ref[...], preferred_element_type=_F32)
        tab = tab_ref[rows, :]
        qmul = tab[:, 0:HEAD_W]
        cc = tab[:, HEAD_W:HEAD_W + LANES]
        ss = tab[:, HEAD_W + LANES:HEAD_W + 2 * LANES]
        krr = (z[:, _C_KR:_C_KR + LANES] * cc + z[:, _C_KRS:_C_KRS + LANES] * ss).astype(_BF16)
        for hd in range(N_HEADS):
            q_ref[rows, hd * HEAD_W:(hd + 1) * HEAD_W] = (q[:, hd * HEAD_W:(hd + 1) * HEAD_W] * qmul).astype(_BF16)
            k_ref[rows, hd * HEAD_W:hd * HEAD_W + NOPE_DIM] = kv[:, hd * NOPE_DIM:(hd + 1) * NOPE_DIM].astype(_BF16)
            k_ref[rows, hd * HEAD_W + NOPE_DIM:(hd + 1) * HEAD_W] = krr
        v_ref[rows, :] = kv[:, N_HEADS * NOPE_DIM:].astype(_BF16)
        ext_ref[MAX_WINDOW + r0:MAX_WINDOW + r0 + IN_SUB_ROWS, :] = z[:, _C_POOL:_C_POOL + POOL_WIDTH]

    u = ext_ref[MAX_WINDOW:MAX_WINDOW + tm, :]

    def window_sum(k0, k1, lo):
        acc = ext_ref[MAX_WINDOW - k0:MAX_WINDOW - k0 + tm, lo:lo + LANES]
        for k in range(k0 + 1, k1):
            acc = acc + ext_ref[MAX_WINDOW - k:MAX_WINDOW - k + tm, lo:lo + LANES]
        return acc

    t_pos = (i % tiles_per_seq) * tm + lax.broadcasted_iota(jnp.int32, (tm, LANES), 0)
    cnt = (t_pos + 1).astype(_F32)
    first_group = lax.broadcasted_iota(jnp.int32, (tm, LANES), 1) < POOL_GROUP_DIM
    w0, w1, w2, w3 = POOL_WINDOWS
    s_a = window_sum(0, w0, 0)
    s_b = s_a + window_sum(w0, w1, 0)
    pooled_lo = jnp.where(first_group, s_a / jnp.minimum(cnt, float(w0)), s_b / jnp.minimum(cnt, float(w1)))
    s_c = window_sum(0, w2, LANES)
    s_d = s_c + window_sum(w2, w3, LANES)
    pooled_hi = jnp.where(first_group, s_c / jnp.minimum(cnt, float(w2)), s_d / jnp.minimum(cnt, float(w3)))
    d = jnp.concatenate([pooled_lo - u[:, :LANES], pooled_hi - u[:, LANES:]], axis=1).astype(_BF16)
    y = jnp.dot(d, pw_ref[...], preferred_element_type=_F32) * ps_ref[...]
    p_ref[...] = y.astype(_BF16)
    ext_ref[0:MAX_WINDOW, :] = ext_ref[tm:tm + MAX_WINDOW, :]


def _of_layer(a, layer):
    return pl.BlockSpec((None,) + a.shape[1:], lambda i: (layer,) + (0,) * (a.ndim - 1))


def _in_proj(x2, g, win, qg, kvg, wq, wkv, tab, pw, ps, *, layer, seq):
    t = x2.shape[0]
    tm = WIDE_TOKEN_TILE
    tps = seq // tm
    full = lambda a: _of_layer(a, layer)
    return pl.pallas_call(
        functools.partial(_in_kernel, tm=tm, tiles_per_seq=tps),
        grid=(t // tm,),
        in_specs=[pl.BlockSpec((tm, D_MODEL), lambda i: (i, 0)),
                  full(g), full(win), full(qg), full(kvg), full(wq), full(wkv),
                  pl.BlockSpec((tm, tab.shape[1]), lambda i: (i % tps, 0)),
                  full(pw), full(ps)],
        out_specs=[pl.BlockSpec((tm, N_HEADS * HEAD_W), lambda i: (i, 0)),
                   pl.BlockSpec((tm, N_HEADS * HEAD_W), lambda i: (i, 0)),
                   pl.BlockSpec((tm, N_HEADS * V_DIM), lambda i: (i, 0)),
                   pl.BlockSpec((tm, POOL_WIDTH), lambda i: (i, 0))],
        out_shape=[jax.ShapeDtypeStruct((t, N_HEADS * HEAD_W), _BF16),
                   jax.ShapeDtypeStruct((t, N_HEADS * HEAD_W), _BF16),
                   jax.ShapeDtypeStruct((t, N_HEADS * V_DIM), _BF16),
                   jax.ShapeDtypeStruct((t, POOL_WIDTH), _BF16)],
        scratch_shapes=[pltpu.VMEM((tm + MAX_WINDOW, POOL_WIDTH), _F32)],
        compiler_params=_params(("arbitrary",), 48),
        name="in_proj",
    )(x2, g, win, qg, kvg, wq, wkv, tab, pw, ps)


def _attn_kernel(*refs, tq, nq, cast_heads):
    n_cast = len(cast_heads)
    q_ref, k_ref, v_ref = refs[0:3]
    o_ref = refs[3 + n_cast]
    vaug_ref = refs[-1]
    head = pl.program_id(1)
    for (lo, hi), src_ref, dst_ref in zip(cast_heads, refs[3:3 + n_cast], refs[4 + n_cast:4 + 2 * n_cast]):
        @pl.when(jnp.logical_and(head >= lo, head < hi))
        def _():
            dst_ref[...] = src_ref[...].astype(_BF16)

    seq = v_ref.shape[0]
    n_heads = v_ref.shape[1] // V_DIM
    ones_col = lax.broadcasted_iota(jnp.int32, (seq, LANES), 1) == 0
    for g in range(n_heads):
        vaug_ref[g, :, 0:V_DIM] = v_ref[:, g * V_DIM:(g + 1) * V_DIM]
        vaug_ref[g, :, V_DIM:V_DIM + LANES] = jnp.where(ones_col, 1.0, 0.0).astype(_BF16)
    row_chunk = lax.broadcasted_iota(jnp.int32, (tq, tq), 0) // CHUNK
    col_chunk = lax.broadcasted_iota(jnp.int32, (tq, tq), 1) // CHUNK
    visible = col_chunk <= row_chunk
    nt = (((1,), (1,)), ((), ()))

    def scores(g, i):
        d0 = i * tq
        cols = slice(g * HEAD_W, (g + 1) * HEAD_W)
        s = lax.dot_general(q_ref[d0:d0 + tq, cols], k_ref[0:d0 + tq, cols], nt, preferred_element_type=_F32)
        s_diag = jnp.where(visible, s[:, d0:], -jnp.inf)
        return s_diag if i == 0 else jnp.concatenate([s[:, :d0], s_diag], axis=1)

    items = [(g, i) for i in range(nq - 1, -1, -1) for g in range(n_heads)]
    s_next = scores(*items[0])
    for n, (g, i) in enumerate(items):
        s = s_next
        d0 = i * tq
        p = jnp.exp2(s - jnp.max(s, axis=1, keepdims=True)).astype(_BF16)
        if n + 1 < len(items):
            s_next = scores(*items[n + 1])
        o = jnp.dot(p, vaug_ref[g, 0:d0 + tq, :], preferred_element_type=_F32)
        o_ref[d0:d0 + tq, g * V_DIM:(g + 1) * V_DIM] = (o[:, 0:V_DIM] / o[:, V_DIM:V_DIM + 1]).astype(_BF16)


def _attention(q, k, v, casts=()):
    b, seq, _ = q.shape
    tq = ATTN_TILE
    hp = ATTN_HEADS_PER_STEP
    head = lambda w: pl.BlockSpec((None, seq, hp * w), lambda bb, hh: (bb, 0, hh))
    cast_specs, cast_shapes = [], []
    for w, lo, hi in casts:
        n_h = hi - lo
        rows, cols = w.shape
        assert rows % (b * n_h * 2 * SUBLANES) == 0
        block_of = lambda bb, hh, lo=lo, n_h=n_h: (bb * n_h + jnp.clip(hh - lo, 0, n_h - 1), 0)
        cast_specs.append(pl.BlockSpec((rows // (b * n_h), cols), block_of))
        cast_shapes.append(jax.ShapeDtypeStruct(w.shape, _BF16))
    outs = pl.pallas_call(
        functools.partial(_attn_kernel, tq=tq, nq=seq // tq, cast_heads=tuple((lo, hi) for _, lo, hi in casts)),
        grid=(b, N_HEADS // hp),
        in_specs=[head(HEAD_W), head(HEAD_W), head(V_DIM)] + cast_specs,
        out_specs=[head(V_DIM)] + cast_specs,
        out_shape=[jax.ShapeDtypeStruct((b, seq, N_HEADS * V_DIM), _BF16)] + cast_shapes,
        scratch_shapes=[pltpu.VMEM((hp, seq, V_DIM + LANES), _BF16)],
        compiler_params=_params(("arbitrary", "arbitrary"), 56),
        name="attention",
    )(q, k, v, *(w for w, _, _ in casts))
    return outs[0], outs[1:]


def _mix_residual(a_ref, p_ref, x_ref, wa_ref, wp_ref, g1_ref):
    mix = (jnp.dot(a_ref[...], wa_ref[...], preferred_element_type=_F32)
           + jnp.dot(p_ref[...], wp_ref[...], preferred_element_type=_F32))
    return x_ref[...] + _rms(mix, g1_ref[...])


def _out_dense_kernel(a_ref, p_ref, x_ref, wa_ref, wp_ref, g1_ref, g2_ref,
                      wg_ref, wu_ref, wd_ref, g3_ref, *rest):
    o_ref = rest[-2] if len(rest) == 3 else rest[0]
    if len(rest) == 3:
        rest[2][...] = rest[0][...].astype(_BF16)
    xn = _mix_residual(a_ref, p_ref, x_ref, wa_ref, wp_ref, g1_ref)
    h = _rms(xn, g2_ref[...]).astype(_BF16)
    gate = jnp.dot(h, wg_ref[...], preferred_element_type=_F32)
    up = jnp.dot(h, wu_ref[...], preferred_element_type=_F32)
    hm = _silu_mul(gate, up).astype(_BF16)
    f = jnp.dot(hm, wd_ref[...], preferred_element_type=_F32)
    o_ref[...] = xn + _rms(f, g3_ref[...])


def _out_moe_kernel(a_ref, p_ref, x_ref, wa_ref, wp_ref, g1_ref, g2_ref, wr_ref,
                    xo_ref, ho_ref, ri_ref, rf_ref, cnt_ref, carry_ref, *, tm):
    @pl.when(pl.program_id(0) == 0)
    def _():
        carry_ref[...] = jnp.zeros_like(carry_ref)

    xn = _mix_residual(a_ref, p_ref, x_ref, wa_ref, wp_ref, g1_ref)
    xo_ref[...] = xn
    h = _rms(xn, g2_ref[...])
    for s in range(ROW_SPLIT):
        ho_ref[pl.ds(s, tm, stride=ROW_SPLIT), :] = h[:, s * LANES:(s + 1) * LANES]

    h_hi = h.astype(_BF16)
    h_lo = (h - h_hi.astype(_F32)).astype(_BF16)
    wr = wr_ref[...]
    nt = (((1,), (1,)), ((), ()))
    by_hi = lax.dot_general(wr, h_hi, nt, preferred_element_type=_F32)
    by_lo = lax.dot_general(wr, h_lo, nt, preferred_element_type=_F32)
    logits = by_hi[0:N_EXPERTS] + by_hi[N_EXPERTS:] + by_lo[0:N_EXPERTS]
    sub = lax.broadcasted_iota(jnp.int32, (N_EXPERTS, tm), 0)
    v1 = jnp.max(logits, axis=0, keepdims=True)
    e1 = jnp.min(jnp.where(logits == v1, sub, N_EXPERTS), axis=0, keepdims=True)
    rest = jnp.where(sub == e1, -jnp.inf, logits)
    v2 = jnp.max(rest, axis=0, keepdims=True)
    e2 = jnp.min(jnp.where(rest == v2, sub, N_EXPERTS), axis=0, keepdims=True)
    ex = jnp.exp(v2 - v1)
    gate1 = 1.0 / (1.0 + ex)
    gate2 = ex * gate1

    sel1 = sub == e1
    sel2 = sub == e2
    member = jnp.logical_or(sel1, sel2).astype(_F32)
    earlier = (lax.broadcasted_iota(jnp.int32, (tm, tm), 0)
               < lax.broadcasted_iota(jnp.int32, (tm, tm), 1)).astype(_BF16)
    member_lhs = jnp.concatenate([member, jnp.zeros_like(member)], axis=0).astype(_BF16)
    carry = carry_ref[:, 0:1]
    before = jnp.dot(member_lhs, earlier, preferred_element_type=_F32)[0:N_EXPERTS] + carry
    r1 = jnp.sum(jnp.where(sel1, before, 0.0), axis=0, keepdims=True).astype(jnp.int32)
    r2 = jnp.sum(jnp.where(sel2, before, 0.0), axis=0, keepdims=True).astype(jnp.int32)
    total = carry + jnp.sum(member, axis=1, keepdims=True)
    carry_ref[...] = jnp.broadcast_to(total, carry_ref.shape)
    cnt_ref[...] = jnp.broadcast_to(total, cnt_ref.shape)

    ri_ref[...] = jnp.where(sub == 0, e1, jnp.where(sub == 1, e2, jnp.where(sub == 2, r1, r2)))
    rf_ref[...] = jnp.where(sub == 0, gate1, gate2)


def _out_proj(a2, p2, x2, wo, g1, g2, *, layer, w_router=None, dense=None, side_cast=None):
    t = x2.shape[0]
    tm = TOKEN_TILE if w_router is None else WIDE_TOKEN_TILE
    full = lambda a: pl.BlockSpec(a.shape, lambda i: (0,) * a.ndim)
    of_layer = lambda a: _of_layer(a, layer)
    attn_w, pool_w = a2.shape[1], p2.shape[1]
    assert attn_w % pool_w == 0
    wa_spec = pl.BlockSpec((None, attn_w, D_MODEL), lambda i: (layer, 0, 0))
    wp_spec = pl.BlockSpec((None, pool_w, D_MODEL), lambda i: (layer, attn_w // pool_w, 0))
    resident = lambda a: pl.BlockSpec(a.shape, lambda i: (0,) * a.ndim, pipeline_mode=pl.Buffered(1))
    row = lambda w: pl.BlockSpec((tm, w), lambda i: (i, 0))
    in_specs = [row(attn_w), row(pool_w), row(D_MODEL), wa_spec, wp_spec, of_layer(g1), of_layer(g2)]
    if w_router is None:
        wg, wu, wd, g3 = dense
        in_specs = in_specs + [resident(wg), resident(wu), resident(wd), of_layer(g3)]
        out_specs, out_shape, extra = [row(D_MODEL)], [jax.ShapeDtypeStruct((t, D_MODEL), _F32)], []
        if side_cast is not None:
            rows, cols = side_cast.shape
            assert rows % ((t // tm) * 2 * SUBLANES) == 0
            spec = pl.BlockSpec((rows // (t // tm), cols), lambda i: (i, 0))
            in_specs, extra = in_specs + [spec], [side_cast]
            out_specs.append(spec)
            out_shape.append(jax.ShapeDtypeStruct(side_cast.shape, _BF16))
        return pl.pallas_call(
            _out_dense_kernel,
            grid=(t // tm,),
            in_specs=in_specs,
            out_specs=out_specs,
            out_shape=out_shape,
            compiler_params=_params(("arbitrary",), 56),
            name="out_proj_dense_ffn",
        )(a2, p2, x2, wo, wo, g1, g2, wg, wu, wd, g3, *extra)
    return pl.pallas_call(
        functools.partial(_out_moe_kernel, tm=tm),
        grid=(t // tm,),
        in_specs=in_specs + [full(w_router)],
        out_specs=[row(D_MODEL),
                   pl.BlockSpec((tm * ROW_SPLIT, LANES), lambda i: (i, 0)),
                   pl.BlockSpec((N_EXPERTS, tm), lambda i: (0, i)),
                   pl.BlockSpec((N_EXPERTS, tm), lambda i: (0, i)),
                   pl.BlockSpec((N_EXPERTS, LANES), lambda i: (0, 0))],
        out_shape=[jax.ShapeDtypeStruct((t, D_MODEL), _F32),
                   jax.ShapeDtypeStruct((t * ROW_SPLIT, LANES), _F32),
                   jax.ShapeDtypeStruct((N_EXPERTS, t), jnp.int32),
                   jax.ShapeDtypeStruct((N_EXPERTS, t), _F32),
                   jax.ShapeDtypeStruct((N_EXPERTS, LANES), _F32)],
        scratch_shapes=[pltpu.VMEM((N_EXPERTS, LANES), _F32)],
        compiler_params=_params(("arbitrary",), 48),
        name="out_proj_router",
    )(a2, p2, x2, wo, wo, g1, g2, w_router)


def _expert_kernel(te_ref, ch_ref, nv_ref, src_ref, h_hbm, wg_ref, wu_ref, wd_ref, y_ref,
                   buf_ref, xb_ref, acc_ref, sem, *, tm, n_tiles, n_steps):
    i = pl.program_id(0)
    c = pl.program_id(1)
    last_c = n_steps - 1
    nv = nv_ref[0]
    rows = tm * ROW_SPLIT

    def start_row(tile_base, slot, r):
        tok = src_ref[tile_base + r]
        pltpu.make_async_copy(
            h_hbm.at[pl.ds(pl.multiple_of(tok * ROW_SPLIT, ROW_SPLIT), ROW_SPLIT), :],
            buf_ref.at[slot, pl.ds(pl.multiple_of(r * ROW_SPLIT, ROW_SPLIT), ROW_SPLIT), :],
            sem.at[slot]).start()

    def wait_slot(slot):
        pltpu.make_async_copy(h_hbm.at[pl.ds(0, rows), :], buf_ref.at[slot], sem.at[slot]).wait()

    @pl.when(jnp.logical_and(i == 0, c == 0))
    def _():
        def body(r, carry):
            start_row(0, 0, r)
            return carry

        lax.fori_loop(0, tm, body, 0)

    @pl.when(jnp.logical_and(c == 0, i == nv))
    def _():
        wait_slot(i % 2)

    def swiglu_part(lo=0, hi=None):
        x = xb_ref[...]
        part = None
        for k0 in range(lo, wg_ref.shape[1] if hi is None else hi, EXPERT_FF_SUB):
            k1 = k0 + EXPERT_FF_SUB
            gate = jnp.dot(x, wg_ref[:, k0:k1], preferred_element_type=_F32)
            up = jnp.dot(x, wu_ref[:, k0:k1], preferred_element_type=_F32)
            hm = _silu_mul(gate, up).astype(_BF16)
            down = jnp.dot(hm, wd_ref[k0:k1, :], preferred_element_type=_F32)
            part = down if part is None else part + down
        return part

    def take_rows_and_look_ahead():
        slot = i % 2
        wait_slot(slot)
        for s in range(ROW_SPLIT):
            xb_ref[:, s * LANES:(s + 1) * LANES] = (
                buf_ref[slot, pl.ds(s, tm, stride=ROW_SPLIT), :].astype(_BF16))
        next_base = jnp.minimum(i + 1, n_tiles - 1) * tm
        for r in range(tm):
            start_row(next_base, 1 - slot, r)

    def store_rows(total):
        for s in range(ROW_SPLIT):
            y_ref[pl.ds(s, tm, stride=ROW_SPLIT), :] = total[:, s * LANES:(s + 1) * LANES]

    if n_steps == 1:
        half = wg_ref.shape[1] // 2

        @pl.when(i < nv)
        def _():
            take_rows_and_look_ahead()
            acc_ref[...] = swiglu_part(0, half)

        @pl.when(jnp.logical_and(i < nv, c == last_c))
        def _():
            store_rows(acc_ref[...] + swiglu_part(half))
    else:

        @pl.when(jnp.logical_and(i < nv, c == 0))
        def _():
            take_rows_and_look_ahead()
            acc_ref[...] = swiglu_part()

        if n_steps > 2:
            @pl.when(jnp.logical_and(i < nv, jnp.logical_and(c > 0, c < last_c)))
            def _():
                acc_ref[...] += swiglu_part()

        @pl.when(jnp.logical_and(i < nv, c == last_c))
        def _():
            store_rows(acc_ref[...] + swiglu_part())

    @pl.when(jnp.logical_and(c == last_c, i >= nv))
    def _():
        y_ref[...] = jnp.zeros_like(y_ref)

    @pl.when(jnp.logical_and(jnp.logical_and(i == n_tiles - 1, c == last_c), nv == n_tiles))
    def _():
        wait_slot(n_tiles % 2)


def _expert_ffn(tile_expert, step_chunk, n_valid, src_tok, h_rows, wg, wu, wd):
    tm = EXPERT_TILE
    n_tiles = tile_expert.shape[0]
    steps = EXPERT_FF_STEPS
    fc = wg.shape[2] // steps

    col_map = lambda i, c, te, ch, nv, src: (te[i], 0, ch[i * steps + c])
    row_map = lambda i, c, te, ch, nv, src: (te[i], ch[i * steps + c], 0)
    grid_spec = pltpu.PrefetchScalarGridSpec(
        num_scalar_prefetch=4,
        grid=(n_tiles, steps),
        in_specs=[pl.BlockSpec(memory_space=pl.ANY),
                  pl.BlockSpec((None, D_MODEL, fc), col_map),
                  pl.BlockSpec((None, D_MODEL, fc), col_map),
                  pl.BlockSpec((None, fc, D_MODEL), row_map)],
        out_specs=pl.BlockSpec((tm * ROW_SPLIT, LANES), lambda i, c, te, ch, nv, src: (i, 0)),
        scratch_shapes=[pltpu.VMEM((2, tm * ROW_SPLIT, LANES), _F32),
                        pltpu.VMEM((tm, D_MODEL), _BF16),
                        pltpu.VMEM((tm, D_MODEL), _F32),
                        pltpu.SemaphoreType.DMA((2,))],
    )
    return pl.pallas_call(
        functools.partial(_expert_kernel, tm=tm, n_tiles=n_tiles, n_steps=steps),
        grid_spec=grid_spec,
        out_shape=jax.ShapeDtypeStruct((n_tiles * tm * ROW_SPLIT, LANES), _F32),
        compiler_params=_params(("arbitrary", "arbitrary"), 60),
        name="expert_ffn",
    )(tile_expert, step_chunk, n_valid, src_tok, h_rows, wg, wu, wd)


def _combine_kernel(p1_ref, p2_ref, y_hbm, x_ref, rf_ref, g_ref, o_ref, buf_ref, sem, *, tm, n_tiles):
    i = pl.program_id(0)
    rows = tm * ROW_SPLIT

    def start_pair(tile_base, slot, r):
        if isinstance(r, int):
            dst = pl.ds(r * ROW_SPLIT, ROW_SPLIT)
        else:
            dst = pl.ds(pl.multiple_of(r * ROW_SPLIT, ROW_SPLIT), ROW_SPLIT)
        for k, pos_ref in enumerate((p1_ref, p2_ref)):
            pos = pos_ref[tile_base + r]
            pltpu.make_async_copy(
                y_hbm.at[pl.ds(pl.multiple_of(pos * ROW_SPLIT, ROW_SPLIT), ROW_SPLIT), :],
                buf_ref.at[slot, k, dst, :], sem.at[slot]).start(priority=k)

    def wait_slot(slot):
        for k in range(2):
            pltpu.make_async_copy(y_hbm.at[pl.ds(0, rows), :], buf_ref.at[slot, k], sem.at[slot]).wait()

    @pl.when(i == 0)
    def _():
        def body(r, carry):
            start_pair(0, 0, r)
            return carry

        lax.fori_loop(0, tm, body, 0)

    for slot in range(2):
        @pl.when(i % 2 == slot)
        def _():
            wait_slot(slot)
            next_base = jnp.minimum(i + 1, n_tiles - 1) * tm
            for r in range(tm):
                start_pair(next_base, 1 - slot, r)
            rf = rf_ref[...]
            g1 = rf[:, 0:1]
            g2 = rf[:, 1:2]
            parts = []
            for s in range(ROW_SPLIT):
                y1 = buf_ref[slot, 0, pl.ds(s, tm, stride=ROW_SPLIT), :]
                y2 = buf_ref[slot, 1, pl.ds(s, tm, stride=ROW_SPLIT), :]
                parts.append(g1 * y1 + g2 * y2)
            f = jnp.concatenate(parts, axis=1)
            o_ref[...] = x_ref[...] + _rms(f, g_ref[...])

    @pl.when(i == n_tiles - 1)
    def _():
        wait_slot(n_tiles % 2)


def _combine(pos1, pos2, y_rows, x2, route_f, g):
    t = x2.shape[0]
    tm = TOKEN_TILE
    grid_spec = pltpu.PrefetchScalarGridSpec(
        num_scalar_prefetch=2,
        grid=(t // tm,),
        in_specs=[pl.BlockSpec(memory_space=pl.ANY),
                  pl.BlockSpec((tm, D_MODEL), lambda i, p1, p2: (i, 0)),
                  pl.BlockSpec((tm, route_f.shape[1]), lambda i, p1, p2: (i, 0)),
                  pl.BlockSpec(g.shape, lambda i, p1, p2: (0, 0))],
        out_specs=pl.BlockSpec((tm, D_MODEL), lambda i, p1, p2: (i, 0)),
        scratch_shapes=[pltpu.VMEM((2, 2, tm * ROW_SPLIT, LANES), _F32),
                        pltpu.SemaphoreType.DMA((2,))],
    )
    return pl.pallas_call(
        functools.partial(_combine_kernel, tm=tm, n_tiles=t // tm),
        grid_spec=grid_spec,
        out_shape=jax.ShapeDtypeStruct((t, D_MODEL), _F32),
        compiler_params=_params(("arbitrary",), 40),
        name="combine",
    )(pos1, pos2, y_rows, x2, route_f, g)


def _swap_halves(w):
    half = w.shape[-1] // 2
    return jnp.concatenate([w[..., half:], w[..., :half]], axis=-1)


def _layer_weights(w_in, w_uq, w_ukv, pool_w, w_out):
    depth = w_in.shape[0]
    kr = w_in[..., _C_KR:_C_KR + ROPE_DIM]
    krs = _swap_halves(kr)
    win = jnp.concatenate([w_in[..., :_C_KR], kr, kr, krs, krs, w_in[..., _C_KR + ROPE_DIM:]], axis=-1)
    q_rope = w_uq[..., NOPE_DIM:]
    wq = jnp.concatenate([w_uq[..., :NOPE_DIM], q_rope, _swap_halves(q_rope)], axis=-1)
    wq = wq.reshape(depth, Q_LORA, N_HEADS * HEAD_W)
    wkv = jnp.concatenate([w_ukv[..., :NOPE_DIM].reshape(depth, KV_LORA, N_HEADS * NOPE_DIM),
                           w_ukv[..., NOPE_DIM:].reshape(depth, KV_LORA, N_HEADS * V_DIM)], axis=-1)
    same_group = jnp.eye(len(POOL_WINDOWS), dtype=pool_w.dtype)[:, None, :, None]
    pw = (pool_w[:, :, :, None, :] * same_group).reshape(depth, POOL_WIDTH, POOL_WIDTH)
    return tuple(a.astype(_BF16) for a in (win, wq, wkv, pw, w_out))


def _rope_table(seq):
    inv = 1.0 / (ROPE_THETA ** (jnp.arange(0, ROPE_DIM, 2, dtype=_F32) / ROPE_DIM))
    ang = jnp.arange(seq, dtype=_F32)[:, None] * inv[None, :]
    cos, sin = jnp.cos(ang), jnp.sin(ang)
    cos2 = jnp.concatenate([cos, cos], axis=1)
    sin2 = jnp.concatenate([-sin, sin], axis=1)
    scale = (NOPE_DIM + ROPE_DIM) ** -0.5 * 1.4426950408889634
    qmul = jnp.concatenate([jnp.full((seq, NOPE_DIM), scale, _F32), scale * cos2, scale * sin2], axis=1)
    return jnp.concatenate([qmul, cos2, cos2, sin2, sin2], axis=1)


def _routing_plan(route_i, counts, t):
    tm = EXPERT_TILE
    n_tiles = (2 * t) // tm + N_EXPERTS
    cnt = counts[:, 0].astype(jnp.int32)
    padded = ((cnt + tm - 1) // tm) * tm
    ends = jnp.cumsum(padded)
    base = ends - padded
    n_valid = (ends[-1] // tm).astype(jnp.int32)
    tile = jnp.arange(n_tiles, dtype=jnp.int32)
    tile_expert = jnp.minimum(jnp.sum((tile * tm)[:, None] >= ends[None, :], axis=1), N_EXPERTS - 1)
    first_tile = jnp.sum(jnp.where(tile_expert[:, None] == jnp.arange(N_EXPERTS)[None, :],
                                   (base // tm)[None, :], 0), axis=1)
    backward = ((tile - first_tile) % 2 == 1)[:, None]
    steps = jnp.arange(EXPERT_FF_STEPS, dtype=jnp.int32)[None, :]
    step_chunk = jnp.where(backward, EXPERT_FF_STEPS - 1 - steps, steps)
    used = (tile < n_valid)[:, None]
    last = n_valid - 1
    tile_expert = jnp.where(used[:, 0], tile_expert, tile_expert[last])
    step_chunk = jnp.where(used, step_chunk, step_chunk[last, EXPERT_FF_STEPS - 1])
    e1, e2, pos1, pos2 = (route_i[k] for k in range(4))
    for k in range(N_EXPERTS):
        pos1 = pos1 + jnp.where(e1 == k, base[k], 0)
        pos2 = pos2 + jnp.where(e2 == k, base[k], 0)
    src = _invert_positions(pos1, pos2, n_tiles * tm)
    return (tile_expert.astype(jnp.int32), step_chunk.reshape(-1).astype(jnp.int32), n_valid.reshape(1),
            src, pos1, pos2)


def _invert_kernel(p1_ref, p2_ref, zeros_hbm, src_ref, *, unroll):
    pltpu.sync_copy(zeros_hbm, src_ref)

    def fill(j, carry):
        for u in range(unroll):
            tok = j * unroll + u
            src_ref[p1_ref[tok]] = tok
            src_ref[p2_ref[tok]] = tok
        return carry

    lax.fori_loop(0, p1_ref.shape[0] // unroll, fill, 0)


def _invert_positions(pos1, pos2, n_slots):
    unroll = 8
    assert n_slots % unroll == 0 and pos1.shape[0] % unroll == 0
    smem = pl.BlockSpec(memory_space=pltpu.SMEM)
    return pl.pallas_call(
        functools.partial(_invert_kernel, unroll=unroll),
        in_specs=[smem, smem, pl.BlockSpec(memory_space=pl.ANY)],
        out_specs=smem,
        out_shape=jax.ShapeDtypeStruct((n_slots,), jnp.int32),
        name="invert_positions",
    )(pos1, pos2, jnp.zeros((n_slots,), jnp.int32))


def kernel(x, attn_pre_g, attn_post_g, ffn_pre_g, ffn_post_g, w_in, q_norm_g, kv_norm_g, w_uq, w_ukv,
           pool_w, pool_scale, w_out, w_gate_d, w_up_d, w_down_d, w_router, w_gate_e, w_up_e, w_down_e):
    b, seq, d = x.shape
    t = b * seq
    depth = w_in.shape[0]
    assert d == D_MODEL and seq % WIDE_TOKEN_TILE == 0 and seq % ATTN_TILE == 0
    tab = _rope_table(seq)
    win, wq, wkv, pw, wo = _layer_weights(w_in, w_uq, w_ukv, pool_w, w_out)
    rows = lambda a: a.reshape(depth, 1, -1)
    attn_pre_g, attn_post_g, ffn_pre_g, ffn_post_g, q_norm_g, kv_norm_g, pool_scale = map(
        rows, (attn_pre_g, attn_post_g, ffn_pre_g, ffn_post_g, q_norm_g, kv_norm_g, pool_scale))
    x2 = x.reshape(t, d)
    for l in range(depth):
        q, k, v, p = _in_proj(x2, attn_pre_g, win, q_norm_g, kv_norm_g, wq, wkv, tab, pw, pool_scale,
                              layer=l, seq=seq)
        i = l // 2
        if l % 2 == 1:
            casts = ((w_up_e[i].reshape(-1, w_up_e.shape[-1]),) + CAST_STEPS_WIDE,)
        else:
            casts = tuple((w,) + CAST_STEPS_REST for w in (w_gate_d[i], w_up_d[i], w_down_d[i]))
            if l + 1 < depth:
                casts += ((w_gate_e[(l + 1) // 2].reshape(-1, w_gate_e.shape[-1]),) + CAST_STEPS_WIDE,)
        a, converted = _attention(q.reshape(b, seq, -1), k.reshape(b, seq, -1), v.reshape(b, seq, -1), casts)
        a2 = a.reshape(t, -1)
        if l % 2 == 0:
            has_next = l + 1 < depth
            outs = _out_proj(a2, p, x2, wo, attn_post_g, ffn_pre_g, layer=l,
                             dense=tuple(converted[0:3]) + (ffn_post_g,),
                             side_cast=w_down_e[(l + 1) // 2].reshape(-1, d) if has_next else None)
            x2 = outs[0]
            if has_next:
                next_gate = converted[3].reshape(w_gate_e.shape[1:])
                next_down = outs[1].reshape(w_down_e.shape[1:])
        else:
            wr_t = w_router[i].T
            wr_hi = wr_t.astype(_BF16)
            wr = jnp.concatenate([wr_hi, (wr_t - wr_hi.astype(_F32)).astype(_BF16)], axis=0)
            x2, h_rows, route_i, route_f, counts = _out_proj(
                a2, p, x2, wo, attn_post_g, ffn_pre_g, layer=l, w_router=wr)
            tile_expert, step_chunk, n_valid, src, pos1, pos2 = _routing_plan(route_i, counts, t)
            y_rows = _expert_ffn(tile_expert, step_chunk, n_valid, src, h_rows, next_gate,
                                 converted[0].reshape(w_up_e.shape[1:]), next_down)
            x2 = _combine(pos1, pos2, y_rows, x2, route_f[0:2].T, ffn_post_g[l])
    return x2.reshape(b, seq, d)
```

```python
import functools

import jax
import jax.numpy as jnp
from jax import lax
from jax.experimental import pallas as pl
from jax.experimental.pallas import tpu as pltpu

D_MODEL = 1024
CHUNK = 64
EPS = 1e-6
N_HEADS = 6
NOPE_DIM = 128
ROPE_DIM = 64
V_DIM = 128
Q_LORA = 384
KV_LORA = 256
ROPE_THETA = 10000.0
POOL_WINDOWS = (2, 4, 8, 16)
POOL_GROUP_DIM = 64
POOL_WIDTH = len(POOL_WINDOWS) * POOL_GROUP_DIM
N_EXPERTS = 8
MAX_WINDOW = max(POOL_WINDOWS)

LANES = 128
SUBLANES = 8
HEAD_W = 2 * LANES
ROW_SPLIT = D_MODEL // LANES

_C_Q = 0
_C_KV = Q_LORA
_C_KR = Q_LORA + KV_LORA
_C_KRS = _C_KR + LANES
_C_POOL = _C_KRS + LANES
IN_AUG = _C_POOL + POOL_WIDTH

TOKEN_TILE = 512
WIDE_TOKEN_TILE = 1024
IN_SUB_ROWS = 256
ATTN_TILE = 256
ATTN_HEADS_PER_STEP = 2
EXPERT_TILE = 512
EXPERT_FF_STEPS = 1
EXPERT_FF_SUB = 256
CAST_STEPS_WIDE = (0, 2)
CAST_STEPS_REST = (2, 3)

_F32 = jnp.float32
_BF16 = jnp.bfloat16
_MIB = 1024 * 1024


def _rms(x, g):
    return x * lax.rsqrt(jnp.mean(x * x, axis=-1, keepdims=True) + EPS) * g


def _silu_mul(g, u):
    hg = 0.5 * g
    return (hg + hg * jnp.tanh(hg)) * u


def _params(sem, vmem_mib, fuse_inputs=None):
    return pltpu.CompilerParams(dimension_semantics=sem, vmem_limit_bytes=vmem_mib * _MIB,
                                allow_input_fusion=fuse_inputs)


def _in_kernel(x_ref, g_ref, win_ref, qg_ref, kvg_ref, wq_ref, wkv_ref, tab_ref, pw_ref, ps_ref,
               q_ref, k_ref, v_ref, p_ref, ext_ref, *, tm, tiles_per_seq):
    i = pl.program_id(0)

    @pl.when(i % tiles_per_seq == 0)
    def _():
        ext_ref[0:MAX_WINDOW, :] = jnp.zeros((MAX_WINDOW, POOL_WIDTH), _F32)

    for r0 in range(0, tm, IN_SUB_ROWS):
        rows = slice(r0, r0 + IN_SUB_ROWS)
        h = _rms(x_ref[rows, :], g_ref[...]).astype(_BF16)
        z = jnp.dot(h, win_ref[...], preferred_element_type=_F32)
        cq = _rms(z[:, _C_Q:_C_Q + Q_LORA], qg_ref[...]).astype(_BF16)
        ckv = _rms(z[:, _C_KV:_C_KV + KV_LORA], kvg_ref[...]).astype(_BF16)
        q = jnp.dot(cq, wq_ref[...], preferred_element_type=_F32)
        kv = jnp.dot(ckv, wkv_ref[...], preferred_element_type=_F32)
        tab = tab_ref[rows, :]
        qmul = tab[:, 0:HEAD_W]
        cc = tab[:, HEAD_W:HEAD_W + LANES]
        ss = tab[:, HEAD_W + LANES:HEAD_W + 2 * LANES]
        krr = (z[:, _C_KR:_C_KR + LANES] * cc + z[:, _C_KRS:_C_KRS + LANES] * ss).astype(_BF16)
        for hd in range(N_HEADS):
            q_ref[rows, hd * HEAD_W:(hd + 1) * HEAD_W] = (q[:, hd * HEAD_W:(hd + 1) * HEAD_W] * qmul).astype(_BF16)
            k_ref[rows, hd * HEAD_W:hd * HEAD_W + NOPE_DIM] = kv[:, hd * NOPE_DIM:(hd + 1) * NOPE_DIM].astype(_BF16)
            k_ref[rows, hd * HEAD_W + NOPE_DIM:(hd + 1) * HEAD_W] = krr
        v_ref[rows, :] = kv[:, N_HEADS * NOPE_DIM:].astype(_BF16)
        ext_ref[MAX_WINDOW + r0:MAX_WINDOW + r0 + IN_SUB_ROWS, :] = z[:, _C_POOL:_C_POOL + POOL_WIDTH]

    u = ext_ref[MAX_WINDOW:MAX_WINDOW + tm, :]

    def window_sum(k0, k1, lo):
        acc = ext_ref[MAX_WINDOW - k0:MAX_WINDOW - k0 + tm, lo:lo + LANES]
        for k in range(k0 + 1, k1):
            acc = acc + ext_ref[MAX_WINDOW - k:MAX_WINDOW - k + tm, lo:lo + LANES]
        return acc

    t_pos = (i % tiles_per_seq) * tm + lax.broadcasted_iota(jnp.int32, (tm, LANES), 0)
    cnt = (t_pos + 1).astype(_F32)
    first_group = lax.broadcasted_iota(jnp.int32, (tm, LANES), 1) < POOL_GROUP_DIM
    w0, w1, w2, w3 = POOL_WINDOWS
    s_a = window_sum(0, w0, 0)
    s_b = s_a + window_sum(w0, w1, 0)
    pooled_lo = jnp.where(first_group, s_a / jnp.minimum(cnt, float(w0)), s_b / jnp.minimum(cnt, float(w1)))
    s_c = window_sum(0, w2, LANES)
    s_d = s_c + window_sum(w2, w3, LANES)
    pooled_hi = jnp.where(first_group, s_c / jnp.minimum(cnt, float(w2)), s_d / jnp.minimum(cnt, float(w3)))
    d = jnp.concatenate([pooled_lo - u[:, :LANES], pooled_hi - u[:, LANES:]], axis=1).astype(_BF16)
    y = jnp.dot(d, pw_ref[...], preferred_element_type=_F32) * ps_ref[...]
    p_ref[...] = y.astype(_BF16)
    ext_ref[0:MAX_WINDOW, :] = ext_ref[tm:tm + MAX_WINDOW, :]


def _of_layer(a, layer):
    return pl.BlockSpec((None,) + a.shape[1:], lambda i: (layer,) + (0,) * (a.ndim - 1))


def _in_proj(x2, g, win, qg, kvg, wq, wkv, tab, pw, ps, *, layer, seq):
    t = x2.shape[0]
    tm = WIDE_TOKEN_TILE
    tps = seq // tm
    full = lambda a: _of_layer(a, layer)
    return pl.pallas_call(
        functools.partial(_in_kernel, tm=tm, tiles_per_seq=tps),
        grid=(t // tm,),
        in_specs=[pl.BlockSpec((tm, D_MODEL), lambda i: (i, 0)),
                  full(g), full(win), full(qg), full(kvg), full(wq), full(wkv),
                  pl.BlockSpec((tm, tab.shape[1]), lambda i: (i % tps, 0)),
                  full(pw), full(ps)],
        out_specs=[pl.BlockSpec((tm, N_HEADS * HEAD_W), lambda i: (i, 0)),
                   pl.BlockSpec((tm, N_HEADS * HEAD_W), lambda i: (i, 0)),
                   pl.BlockSpec((tm, N_HEADS * V_DIM), lambda i: (i, 0)),
                   pl.BlockSpec((tm, POOL_WIDTH), lambda i: (i, 0))],
        out_shape=[jax.ShapeDtypeStruct((t, N_HEADS * HEAD_W), _BF16),
                   jax.ShapeDtypeStruct((t, N_HEADS * HEAD_W), _BF16),
                   jax.ShapeDtypeStruct((t, N_HEADS * V_DIM), _BF16),
                   jax.ShapeDtypeStruct((t, POOL_WIDTH), _BF16)],
        scratch_shapes=[pltpu.VMEM((tm + MAX_WINDOW, POOL_WIDTH), _F32)],
        compiler_params=_params(("arbitrary",), 48,
                                fuse_inputs=[False, False, True, False, False, True, True, True, True, False]),
        name="in_proj",
    )(x2, g, win, qg, kvg, wq, wkv, tab, pw, ps)


def _attn_kernel(*refs, tq, nq, cast_heads):
    n_cast = len(cast_heads)
    q_ref, k_ref, v_ref = refs[0:3]
    o_ref = refs[3 + n_cast]
    vaug_ref = refs[-1]
    head = pl.program_id(1)
    for (lo, hi), src_ref, dst_ref in zip(cast_heads, refs[3:3 + n_cast], refs[4 + n_cast:4 + 2 * n_cast]):
        @pl.when(jnp.logical_and(head >= lo, head < hi))
        def _():
            dst_ref[...] = src_ref[...].astype(_BF16)

    seq = v_ref.shape[0]
    n_heads = v_ref.shape[1] // V_DIM
    ones_col = lax.broadcasted_iota(jnp.int32, (seq, LANES), 1) == 0
    for g in range(n_heads):
        vaug_ref[g, :, 0:V_DIM] = v_ref[:, g * V_DIM:(g + 1) * V_DIM]
        vaug_ref[g, :, V_DIM:V_DIM + LANES] = jnp.where(ones_col, 1.0, 0.0).astype(_BF16)
    row_chunk = lax.broadcasted_iota(jnp.int32, (tq, tq), 0) // CHUNK
    col_chunk = lax.broadcasted_iota(jnp.int32, (tq, tq), 1) // CHUNK
    visible = col_chunk <= row_chunk
    nt = (((1,), (1,)), ((), ()))

    def scores(g, i):
        d0 = i * tq
        cols = slice(g * HEAD_W, (g + 1) * HEAD_W)
        s = lax.dot_general(q_ref[d0:d0 + tq, cols], k_ref[0:d0 + tq, cols], nt, preferred_element_type=_F32)
        s_diag = jnp.where(visible, s[:, d0:], -jnp.inf)
        return s_diag if i == 0 else jnp.concatenate([s[:, :d0], s_diag], axis=1)

    items = [(g, i) for i in range(nq - 1, -1, -1) for g in range(n_heads)]
    s_next = scores(*items[0])
    for n, (g, i) in enumerate(items):
        s = s_next
        d0 = i * tq
        p = jnp.exp2(s - jnp.max(s, axis=1, keepdims=True)).astype(_BF16)
        if n + 1 < len(items):
            s_next = scores(*items[n + 1])
        o = jnp.dot(p, vaug_ref[g, 0:d0 + tq, :], preferred_element_type=_F32)
        o_ref[d0:d0 + tq, g * V_DIM:(g + 1) * V_DIM] = (o[:, 0:V_DIM] / o[:, V_DIM:V_DIM + 1]).astype(_BF16)


def _attention(q, k, v, casts=()):
    b, seq, _ = q.shape
    tq = ATTN_TILE
    hp = ATTN_HEADS_PER_STEP
    head = lambda w: pl.BlockSpec((None, seq, hp * w), lambda bb, hh: (bb, 0, hh))
    cast_specs, cast_shapes = [], []
    for w, lo, hi in casts:
        n_h = hi - lo
        rows, cols = w.shape
        assert rows % (b * n_h * 2 * SUBLANES) == 0
        block_of = lambda bb, hh, lo=lo, n_h=n_h: (bb * n_h + jnp.clip(hh - lo, 0, n_h - 1), 0)
        cast_specs.append(pl.BlockSpec((rows // (b * n_h), cols), block_of))
        cast_shapes.append(jax.ShapeDtypeStruct(w.shape, _BF16))
    outs = pl.pallas_call(
        functools.partial(_attn_kernel, tq=tq, nq=seq // tq, cast_heads=tuple((lo, hi) for _, lo, hi in casts)),
        grid=(b, N_HEADS // hp),
        in_specs=[head(HEAD_W), head(HEAD_W), head(V_DIM)] + cast_specs,
        out_specs=[head(V_DIM)] + cast_specs,
        out_shape=[jax.ShapeDtypeStruct((b, seq, N_HEADS * V_DIM), _BF16)] + cast_shapes,
        scratch_shapes=[pltpu.VMEM((hp, seq, V_DIM + LANES), _BF16)],
        compiler_params=_params(("arbitrary", "arbitrary"), 56),
        name="attention",
    )(q, k, v, *(w for w, _, _ in casts))
    return outs[0], outs[1:]


def _mix_residual(a_ref, p_ref, x_ref, wa_ref, wp_ref, g1_ref):
    mix = (jnp.dot(a_ref[...], wa_ref[...], preferred_element_type=_F32)
           + jnp.dot(p_ref[...], wp_ref[...], preferred_element_type=_F32))
    return x_ref[...] + _rms(mix, g1_ref[...])


def _out_dense_kernel(a_ref, p_ref, x_ref, wa_ref, wp_ref, g1_ref, g2_ref,
                      wg_ref, wu_ref, wd_ref, g3_ref, *rest):
    o_ref = rest[-2] if len(rest) == 3 else rest[0]
    if len(rest) == 3:
        rest[2][...] = rest[0][...].astype(_BF16)
    xn = _mix_residual(a_ref, p_ref, x_ref, wa_ref, wp_ref, g1_ref)
    h = _rms(xn, g2_ref[...]).astype(_BF16)
    gate = jnp.dot(h, wg_ref[...], preferred_element_type=_F32)
    up = jnp.dot(h, wu_ref[...], preferred_element_type=_F32)
    hm = _silu_mul(gate, up).astype(_BF16)
    f = jnp.dot(hm, wd_ref[...], preferred_element_type=_F32)
    o_ref[...] = xn + _rms(f, g3_ref[...])


def _out_moe_kernel(a_ref, p_ref, x_ref, wa_ref, wp_ref, g1_ref, g2_ref, wr_ref,
                    xo_ref, ho_ref, ri_ref, rf_ref, cnt_ref, carry_ref, *, tm):
    @pl.when(pl.program_id(0) == 0)
    def _():
        carry_ref[...] = jnp.zeros_like(carry_ref)

    xn = _mix_residual(a_ref, p_ref, x_ref, wa_ref, wp_ref, g1_ref)
    xo_ref[...] = xn
    h = _rms(xn, g2_ref[...])
    for s in range(ROW_SPLIT):
        ho_ref[pl.ds(s, tm, stride=ROW_SPLIT), :] = h[:, s * LANES:(s + 1) * LANES]

    h_hi = h.astype(_BF16)
    h_lo = (h - h_hi.astype(_F32)).astype(_BF16)
    wr = wr_ref[...]
    nt = (((1,), (1,)), ((), ()))
    by_hi = lax.dot_general(wr, h_hi, nt, preferred_element_type=_F32)
    by_lo = lax.dot_general(wr, h_lo, nt, preferred_element_type=_F32)
    logits = by_hi[0:N_EXPERTS] + by_hi[N_EXPERTS:] + by_lo[0:N_EXPERTS]
    sub = lax.broadcasted_iota(jnp.int32, (N_EXPERTS, tm), 0)
    v1 = jnp.max(logits, axis=0, keepdims=True)
    e1 = jnp.min(jnp.where(logits == v1, sub, N_EXPERTS), axis=0, keepdims=True)
    rest = jnp.where(sub == e1, -jnp.inf, logits)
    v2 = jnp.max(rest, axis=0, keepdims=True)
    e2 = jnp.min(jnp.where(rest == v2, sub, N_EXPERTS), axis=0, keepdims=True)
    ex = jnp.exp(v2 - v1)
    gate1 = 1.0 / (1.0 + ex)
    gate2 = ex * gate1

    sel1 = sub == e1
    sel2 = sub == e2
    member = jnp.logical_or(sel1, sel2).astype(_F32)
    earlier = (lax.broadcasted_iota(jnp.int32, (tm, tm), 0)
               < lax.broadcasted_iota(jnp.int32, (tm, tm), 1)).astype(_BF16)
    member_lhs = jnp.concatenate([member, jnp.zeros_like(member)], axis=0).astype(_BF16)
    carry = carry_ref[:, 0:1]
    before = jnp.dot(member_lhs, earlier, preferred_element_type=_F32)[0:N_EXPERTS] + carry
    r1 = jnp.sum(jnp.where(sel1, before, 0.0), axis=0, keepdims=True).astype(jnp.int32)
    r2 = jnp.sum(jnp.where(sel2, before, 0.0), axis=0, keepdims=True).astype(jnp.int32)
    total = carry + jnp.sum(member, axis=1, keepdims=True)
    carry_ref[...] = jnp.broadcast_to(total, carry_ref.shape)
    cnt_ref[...] = jnp.broadcast_to(total, cnt_ref.shape)

    ri_ref[...] = jnp.where(sub == 0, e1, jnp.where(sub == 1, e2, jnp.where(sub == 2, r1, r2)))
    rf_ref[...] = jnp.where(sub == 0, gate1, gate2)


def _out_proj(a2, p2, x2, wo, g1, g2, *, layer, w_router=None, dense=None, side_cast=None):
    t = x2.shape[0]
    tm = TOKEN_TILE if w_router is None else WIDE_TOKEN_TILE
    full = lambda a: pl.BlockSpec(a.shape, lambda i: (0,) * a.ndim)
    of_layer = lambda a: _of_layer(a, layer)
    attn_w, pool_w = a2.shape[1], p2.shape[1]
    assert attn_w % pool_w == 0
    wa_spec = pl.BlockSpec((None, attn_w, D_MODEL), lambda i: (layer, 0, 0))
    wp_spec = pl.BlockSpec((None, pool_w, D_MODEL), lambda i: (layer, attn_w // pool_w, 0))
    resident = lambda a: pl.BlockSpec(a.shape, lambda i: (0,) * a.ndim, pipeline_mode=pl.Buffered(1))
    row = lambda w: pl.BlockSpec((tm, w), lambda i: (i, 0))
    in_specs = [row(attn_w), row(pool_w), row(D_MODEL), wa_spec, wp_spec, of_layer(g1), of_layer(g2)]
    if w_router is None:
        wg, wu, wd, g3 = dense
        in_specs = in_specs + [resident(wg), resident(wu), resident(wd), of_layer(g3)]
        out_specs, out_shape, extra = [row(D_MODEL)], [jax.ShapeDtypeStruct((t, D_MODEL), _F32)], []
        if side_cast is not None:
            rows, cols = side_cast.shape
            assert rows % ((t // tm) * 2 * SUBLANES) == 0
            spec = pl.BlockSpec((rows // (t // tm), cols), lambda i: (i, 0))
            in_specs, extra = in_specs + [spec], [side_cast]
            out_specs.append(spec)
            out_shape.append(jax.ShapeDtypeStruct(side_cast.shape, _BF16))
        return pl.pallas_call(
            _out_dense_kernel,
            grid=(t // tm,),
            in_specs=in_specs,
            out_specs=out_specs,
            out_shape=out_shape,
            compiler_params=_params(("arbitrary",), 56),
            name="out_proj_dense_ffn",
        )(a2, p2, x2, wo, wo, g1, g2, wg, wu, wd, g3, *extra)
    return pl.pallas_call(
        functools.partial(_out_moe_kernel, tm=tm),
        grid=(t // tm,),
        in_specs=in_specs + [full(w_router)],
        out_specs=[row(D_MODEL),
                   pl.BlockSpec((tm * ROW_SPLIT, LANES), lambda i: (i, 0)),
                   pl.BlockSpec((N_EXPERTS, tm), lambda i: (0, i)),
                   pl.BlockSpec((N_EXPERTS, tm), lambda i: (0, i)),
                   pl.BlockSpec((N_EXPERTS, LANES), lambda i: (0, 0))],
        out_shape=[jax.ShapeDtypeStruct((t, D_MODEL), _F32),
                   jax.ShapeDtypeStruct((t * ROW_SPLIT, LANES), _F32),
                   jax.ShapeDtypeStruct((N_EXPERTS, t), jnp.int32),
                   jax.ShapeDtypeStruct((N_EXPERTS, t), _F32),
                   jax.ShapeDtypeStruct((N_EXPERTS, LANES), _F32)],
        scratch_shapes=[pltpu.VMEM((N_EXPERTS, LANES), _F32)],
        compiler_params=_params(("arbitrary",), 48),
        name="out_proj_router",
    )(a2, p2, x2, wo, wo, g1, g2, w_router)


def _expert_kernel(te_ref, ch_ref, nv_ref, src_ref, h_hbm, wg_ref, wu_ref, wd_ref, y_ref,
                   buf_ref, xb_ref, acc_ref, sem, *, tm, n_tiles, n_steps):
    i = pl.program_id(0)
    c = pl.program_id(1)
    last_c = n_steps - 1
    nv = nv_ref[0]
    rows = tm * ROW_SPLIT

    def start_row(tile_base, slot, r):
        tok = src_ref[tile_base + r]
        pltpu.make_async_copy(
            h_hbm.at[pl.ds(pl.multiple_of(tok * ROW_SPLIT, ROW_SPLIT), ROW_SPLIT), :],
            buf_ref.at[slot, pl.ds(pl.multiple_of(r * ROW_SPLIT, ROW_SPLIT), ROW_SPLIT), :],
            sem.at[slot]).start()

    def wait_slot(slot):
        pltpu.make_async_copy(h_hbm.at[pl.ds(0, rows), :], buf_ref.at[slot], sem.at[slot]).wait()

    @pl.when(jnp.logical_and(i == 0, c == 0))
    def _():
        def body(r, carry):
            start_row(0, 0, r)
            return carry

        lax.fori_loop(0, tm, body, 0)

    @pl.when(jnp.logical_and(c == 0, i == nv))
    def _():
        wait_slot(i % 2)

    def swiglu_part(lo=0, hi=None):
        x = xb_ref[...]
        part = None
        for k0 in range(lo, wg_ref.shape[1] if hi is None else hi, EXPERT_FF_SUB):
            k1 = k0 + EXPERT_FF_SUB
            gate = jnp.dot(x, wg_ref[:, k0:k1], preferred_element_type=_F32)
            up = jnp.dot(x, wu_ref[:, k0:k1], preferred_element_type=_F32)
            hm = _silu_mul(gate, up).astype(_BF16)
            down = jnp.dot(hm, wd_ref[k0:k1, :], preferred_element_type=_F32)
            part = down if part is None else part + down
        return part

    def take_rows_and_look_ahead():
        slot = i % 2
        wait_slot(slot)
        for s in range(ROW_SPLIT):
            xb_ref[:, s * LANES:(s + 1) * LANES] = (
                buf_ref[slot, pl.ds(s, tm, stride=ROW_SPLIT), :].astype(_BF16))
        next_base = jnp.minimum(i + 1, n_tiles - 1) * tm
        for r in range(tm):
            start_row(next_base, 1 - slot, r)

    def store_rows(total):
        for s in range(ROW_SPLIT):
            y_ref[pl.ds(s, tm, stride=ROW_SPLIT), :] = total[:, s * LANES:(s + 1) * LANES]

    if n_steps == 1:
        half = wg_ref.shape[1] // 2

        @pl.when(i < nv)
        def _():
            take_rows_and_look_ahead()
            acc_ref[...] = swiglu_part(0, half)

        @pl.when(jnp.logical_and(i < nv, c == last_c))
        def _():
            store_rows(acc_ref[...] + swiglu_part(half))
    else:

        @pl.when(jnp.logical_and(i < nv, c == 0))
        def _():
            take_rows_and_look_ahead()
            acc_ref[...] = swiglu_part()

        if n_steps > 2:
            @pl.when(jnp.logical_and(i < nv, jnp.logical_and(c > 0, c < last_c)))
            def _():
                acc_ref[...] += swiglu_part()

        @pl.when(jnp.logical_and(i < nv, c == last_c))
        def _():
            store_rows(acc_ref[...] + swiglu_part())

    @pl.when(jnp.logical_and(c == last_c, i >= nv))
    def _():
        y_ref[...] = jnp.zeros_like(y_ref)

    @pl.when(jnp.logical_and(jnp.logical_and(i == n_tiles - 1, c == last_c), nv == n_tiles))
    def _():
        wait_slot(n_tiles % 2)


def _expert_ffn(tile_expert, step_chunk, n_valid, src_tok, h_rows, wg, wu, wd):
    tm = EXPERT_TILE
    n_tiles = tile_expert.shape[0]
    steps = EXPERT_FF_STEPS
    fc = wg.shape[2] // steps

    col_map = lambda i, c, te, ch, nv, src: (te[i], 0, ch[i * steps + c])
    row_map = lambda i, c, te, ch, nv, src: (te[i], ch[i * steps + c], 0)
    grid_spec = pltpu.PrefetchScalarGridSpec(
        num_scalar_prefetch=4,
        grid=(n_tiles, steps),
        in_specs=[pl.BlockSpec(memory_space=pl.ANY),
                  pl.BlockSpec((None, D_MODEL, fc), col_map),
                  pl.BlockSpec((None, D_MODEL, fc), col_map),
                  pl.BlockSpec((None, fc, D_MODEL), row_map)],
        out_specs=pl.BlockSpec((tm * ROW_SPLIT, LANES), lambda i, c, te, ch, nv, src: (i, 0)),
        scratch_shapes=[pltpu.VMEM((2, tm * ROW_SPLIT, LANES), _F32),
                        pltpu.VMEM((tm, D_MODEL), _BF16),
                        pltpu.VMEM((tm, D_MODEL), _F32),
                        pltpu.SemaphoreType.DMA((2,))],
    )
    return pl.pallas_call(
        functools.partial(_expert_kernel, tm=tm, n_tiles=n_tiles, n_steps=steps),
        grid_spec=grid_spec,
        out_shape=jax.ShapeDtypeStruct((n_tiles * tm * ROW_SPLIT, LANES), _F32),
        compiler_params=_params(("arbitrary", "arbitrary"), 60),
        name="expert_ffn",
    )(tile_expert, step_chunk, n_valid, src_tok, h_rows, wg, wu, wd)


def _combine_kernel(p1_ref, p2_ref, y_hbm, x_ref, rf_ref, g_ref, o_ref, buf_ref, sem, *, tm, n_tiles):
    i = pl.program_id(0)
    rows = tm * ROW_SPLIT

    def start_pair(tile_base, slot, r):
        if isinstance(r, int):
            dst = pl.ds(r * ROW_SPLIT, ROW_SPLIT)
        else:
            dst = pl.ds(pl.multiple_of(r * ROW_SPLIT, ROW_SPLIT), ROW_SPLIT)
        for k, pos_ref in enumerate((p1_ref, p2_ref)):
            pos = pos_ref[tile_base + r]
            pltpu.make_async_copy(
                y_hbm.at[pl.ds(pl.multiple_of(pos * ROW_SPLIT, ROW_SPLIT), ROW_SPLIT), :],
                buf_ref.at[slot, k, dst, :], sem.at[slot]).start(priority=k)

    def wait_slot(slot):
        for k in range(2):
            pltpu.make_async_copy(y_hbm.at[pl.ds(0, rows), :], buf_ref.at[slot, k], sem.at[slot]).wait()

    @pl.when(i == 0)
    def _():
        def body(r, carry):
            start_pair(0, 0, r)
            return carry

        lax.fori_loop(0, tm, body, 0)

    for slot in range(2):
        @pl.when(i % 2 == slot)
        def _():
            wait_slot(slot)
            next_base = jnp.minimum(i + 1, n_tiles - 1) * tm
            for r in range(tm):
                start_pair(next_base, 1 - slot, r)
            rf = rf_ref[...]
            g1 = rf[:, 0:1]
            g2 = rf[:, 1:2]
            parts = []
            for s in range(ROW_SPLIT):
                y1 = buf_ref[slot, 0, pl.ds(s, tm, stride=ROW_SPLIT), :]
                y2 = buf_ref[slot, 1, pl.ds(s, tm, stride=ROW_SPLIT), :]
                parts.append(g1 * y1 + g2 * y2)
            f = jnp.concatenate(parts, axis=1)
            o_ref[...] = x_ref[...] + _rms(f, g_ref[...])

    @pl.when(i == n_tiles - 1)
    def _():
        wait_slot(n_tiles % 2)


def _combine(pos1, pos2, y_rows, x2, route_f, g):
    t = x2.shape[0]
    tm = TOKEN_TILE
    grid_spec = pltpu.PrefetchScalarGridSpec(
        num_scalar_prefetch=2,
        grid=(t // tm,),
        in_specs=[pl.BlockSpec(memory_space=pl.ANY),
                  pl.BlockSpec((tm, D_MODEL), lambda i, p1, p2: (i, 0)),
                  pl.BlockSpec((tm, route_f.shape[1]), lambda i, p1, p2: (i, 0)),
                  pl.BlockSpec(g.shape, lambda i, p1, p2: (0, 0))],
        out_specs=pl.BlockSpec((tm, D_MODEL), lambda i, p1, p2: (i, 0)),
        scratch_shapes=[pltpu.VMEM((2, 2, tm * ROW_SPLIT, LANES), _F32),
                        pltpu.SemaphoreType.DMA((2,))],
    )
    return pl.pallas_call(
        functools.partial(_combine_kernel, tm=tm, n_tiles=t // tm),
        grid_spec=grid_spec,
        out_shape=jax.ShapeDtypeStruct((t, D_MODEL), _F32),
        compiler_params=_params(("arbitrary",), 40),
        name="combine",
    )(pos1, pos2, y_rows, x2, route_f, g)


def _swap_halves(w):
    half = w.shape[-1] // 2
    return jnp.concatenate([w[..., half:], w[..., :half]], axis=-1)


def _layer_weights(w_in, w_uq, w_ukv, pool_w, w_out):
    depth = w_in.shape[0]
    kr = w_in[..., _C_KR:_C_KR + ROPE_DIM]
    krs = _swap_halves(kr)
    win = jnp.concatenate([w_in[..., :_C_KR], kr, kr, krs, krs, w_in[..., _C_KR + ROPE_DIM:]], axis=-1)
    q_rope = w_uq[..., NOPE_DIM:]
    wq = jnp.concatenate([w_uq[..., :NOPE_DIM], q_rope, _swap_halves(q_rope)], axis=-1)
    wq = wq.reshape(depth, Q_LORA, N_HEADS * HEAD_W)
    wkv = jnp.concatenate([w_ukv[..., :NOPE_DIM].reshape(depth, KV_LORA, N_HEADS * NOPE_DIM),
                           w_ukv[..., NOPE_DIM:].reshape(depth, KV_LORA, N_HEADS * V_DIM)], axis=-1)
    same_group = jnp.eye(len(POOL_WINDOWS), dtype=pool_w.dtype)[:, None, :, None]
    pw = (pool_w[:, :, :, None, :] * same_group).reshape(depth, POOL_WIDTH, POOL_WIDTH)
    return tuple(a.astype(_BF16) for a in (win, wq, wkv, pw, w_out))


def _rope_table(seq):
    inv = 1.0 / (ROPE_THETA ** (jnp.arange(0, ROPE_DIM, 2, dtype=_F32) / ROPE_DIM))
    ang = jnp.arange(seq, dtype=_F32)[:, None] * inv[None, :]
    cos, sin = jnp.cos(ang), jnp.sin(ang)
    cos2 = jnp.concatenate([cos, cos], axis=1)
    sin2 = jnp.concatenate([-sin, sin], axis=1)
    scale = (NOPE_DIM + ROPE_DIM) ** -0.5 * 1.4426950408889634
    qmul = jnp.concatenate([jnp.full((seq, NOPE_DIM), scale, _F32), scale * cos2, scale * sin2], axis=1)
    return jnp.concatenate([qmul, cos2, cos2, sin2, sin2], axis=1)


def _routing_plan(route_i, counts, t):
    tm = EXPERT_TILE
    n_tiles = (2 * t) // tm + N_EXPERTS
    cnt = counts[:, 0].astype(jnp.int32)
    padded = ((cnt + tm - 1) // tm) * tm
    ends = jnp.cumsum(padded)
    base = ends - padded
    n_valid = (ends[-1] // tm).astype(jnp.int32)
    tile = jnp.arange(n_tiles, dtype=jnp.int32)
    tile_expert = jnp.minimum(jnp.sum((tile * tm)[:, None] >= ends[None, :], axis=1), N_EXPERTS - 1)
    first_tile = jnp.sum(jnp.where(tile_expert[:, None] == jnp.arange(N_EXPERTS)[None, :],
                                   (base // tm)[None, :], 0), axis=1)
    backward = ((tile - first_tile) % 2 == 1)[:, None]
    steps = jnp.arange(EXPERT_FF_STEPS, dtype=jnp.int32)[None, :]
    step_chunk = jnp.where(backward, EXPERT_FF_STEPS - 1 - steps, steps)
    used = (tile < n_valid)[:, None]
    last = n_valid - 1
    tile_expert = jnp.where(used[:, 0], tile_expert, tile_expert[last])
    step_chunk = jnp.where(used, step_chunk, step_chunk[last, EXPERT_FF_STEPS - 1])
    e1, e2, pos1, pos2 = (route_i[k] for k in range(4))
    for k in range(N_EXPERTS):
        pos1 = pos1 + jnp.where(e1 == k, base[k], 0)
        pos2 = pos2 + jnp.where(e2 == k, base[k], 0)
    src = _invert_positions(pos1, pos2, n_tiles * tm)
    return (tile_expert.astype(jnp.int32), step_chunk.reshape(-1).astype(jnp.int32), n_valid.reshape(1),
            src, pos1, pos2)


def _invert_kernel(p1_ref, p2_ref, zeros_hbm, src_ref, *, unroll):
    pltpu.sync_copy(zeros_hbm, src_ref)

    def fill(j, carry):
        for u in range(unroll):
            tok = j * unroll + u
            src_ref[p1_ref[tok]] = tok
            src_ref[p2_ref[tok]] = tok
        return carry

    lax.fori_loop(0, p1_ref.shape[0] // unroll, fill, 0)


def _invert_positions(pos1, pos2, n_slots):
    unroll = 8
    assert n_slots % unroll == 0 and pos1.shape[0] % unroll == 0
    smem = pl.BlockSpec(memory_space=pltpu.SMEM)
    return pl.pallas_call(
        functools.partial(_invert_kernel, unroll=unroll),
        in_specs=[smem, smem, pl.BlockSpec(memory_space=pl.ANY)],
        out_specs=smem,
        out_shape=jax.ShapeDtypeStruct((n_slots,), jnp.int32),
        name="invert_positions",
    )(pos1, pos2, jnp.zeros((n_slots,), jnp.int32))


def kernel(x, attn_pre_g, attn_post_g, ffn_pre_g, ffn_post_g, w_in, q_norm_g, kv_norm_g, w_uq, w_ukv,
           pool_w, pool_scale, w_out, w_gate_d, w_up_d, w_down_d, w_router, w_gate_e, w_up_e, w_down_e):
    b, seq, d = x.shape
    t = b * seq
    depth = w_in.shape[0]
    assert d == D_MODEL and seq % WIDE_TOKEN_TILE == 0 and seq % ATTN_TILE == 0
    tab = _rope_table(seq)
    win, wq, wkv, pw, wo = _layer_weights(w_in, w_uq, w_ukv, pool_w, w_out)
    rows = lambda a: a.reshape(depth, 1, -1)
    attn_pre_g, attn_post_g, ffn_pre_g, ffn_post_g, q_norm_g, kv_norm_g, pool_scale = map(
        rows, (attn_pre_g, attn_post_g, ffn_pre_g, ffn_post_g, q_norm_g, kv_norm_g, pool_scale))
    x2 = x.reshape(t, d)
    for l in range(depth):
        q, k, v, p = _in_proj(x2, attn_pre_g, win, q_norm_g, kv_norm_g, wq, wkv, tab, pw, pool_scale,
                              layer=l, seq=seq)
        i = l // 2
        if l % 2 == 1:
            casts = ((w_up_e[i].reshape(-1, w_up_e.shape[-1]),) + CAST_STEPS_WIDE,)
        else:
            casts = tuple((w,) + CAST_STEPS_REST for w in (w_gate_d[i], w_up_d[i], w_down_d[i]))
            if l + 1 < depth:
                casts += ((w_gate_e[(l + 1) // 2].reshape(-1, w_gate_e.shape[-1]),) + CAST_STEPS_WIDE,)
        a, converted = _attention(q.reshape(b, seq, -1), k.reshape(b, seq, -1), v.reshape(b, seq, -1), casts)
        a2 = a.reshape(t, -1)
        if l % 2 == 0:
            has_next = l + 1 < depth
            outs = _out_proj(a2, p, x2, wo, attn_post_g, ffn_pre_g, layer=l,
                             dense=tuple(converted[0:3]) + (ffn_post_g,),
                             side_cast=w_down_e[(l + 1) // 2].reshape(-1, d) if has_next else None)
            x2 = outs[0]
            if has_next:
                next_gate = converted[3].reshape(w_gate_e.shape[1:])
                next_down = outs[1].reshape(w_down_e.shape[1:])
        else:
            wr_t = w_router[i].T
            wr_hi = wr_t.astype(_BF16)
            wr = jnp.concatenate([wr_hi, (wr_t - wr_hi.astype(_F32)).astype(_BF16)], axis=0)
            x2, h_rows, route_i, route_f, counts = _out_proj(
                a2, p, x2, wo, attn_post_g, ffn_pre_g, layer=l, w_router=wr)
            tile_expert, step_chunk, n_valid, src, pos1, pos2 = _routing_plan(route_i, counts, t)
            y_rows = _expert_ffn(tile_expert, step_chunk, n_valid, src, h_rows, next_gate,
                                 converted[0].reshape(w_up_e.shape[1:]), next_down)
            x2 = _combine(pos1, pos2, y_rows, x2, route_f[0:2].T, ffn_post_g[l])
    return x2.reshape(b, seq, d)
```
